```python
import jax, jax.numpy as jnp
from jax import lax
import numpy as np

D_MODEL = 2048
BATCH = 16
SEQ = 2048
DEPTH = 2

N_MIXERS = 2
N_MLA_LAYERS = (DEPTH + 1) // 2
N_MLSTM_LAYERS = DEPTH // 2

MLA_HEADS = 16
QK_NOPE_DIM = 128
QK_ROPE_DIM = 64
V_HEAD_DIM = 128
Q_LORA_RANK = 512
KV_LORA_RANK = 512
ROPE_THETA = 10000.0
Q_BLOCK = 128

MLSTM_HEADS = 8
MLSTM_QK_DIM = D_MODEL // (2 * MLSTM_HEADS)
MLSTM_V_DIM = D_MODEL // MLSTM_HEADS
CHUNK = 64
GATE_SOFTCAP = 15.0

N_EXPERTS = 32
TOP_K = 4
D_EXPERT = D_MODEL
SWIGLU_ALPHA = 1.702
SWIGLU_LIMIT = 7.0
ROW_BLOCK = 128

DEEPNORM_ALPHA = (2 * DEPTH) ** 0.25
DEEPNORM_BETA = (8 * DEPTH) ** -0.25
NORM_EPS = 1e-6

kernel_name = 'hybrid_mla_mlstm_moe_deepnorm_adaln'


def rms_norm(x, g):
    x32 = x.astype(jnp.float32)
    y = x32 * lax.rsqrt(jnp.mean(x32 * x32, axis=-1, keepdims=True) + NORM_EPS)
    return (y * g.astype(jnp.float32)).astype(x.dtype)


def layer_norm(x, g, b):
    x32 = x.astype(jnp.float32)
    xc = x32 - jnp.mean(x32, axis=-1, keepdims=True)
    var = jnp.mean(xc * xc, axis=-1, keepdims=True)
    y = xc * lax.rsqrt(var + NORM_EPS) * g.astype(jnp.float32) + b.astype(jnp.float32)
    return y.astype(x.dtype)


def apply_rope(x, cos, sin):
    half = x.shape[-1] // 2
    x1 = x[..., :half].astype(jnp.float32)
    x2 = x[..., half:].astype(jnp.float32)
    return jnp.concatenate([x1 * cos - x2 * sin, x2 * cos + x1 * sin], axis=-1).astype(x.dtype)


def mla_mixer(h, positions, w_in, q_norm, kv_norm, w_uq, w_ukv, w_o):
    B, S, _ = h.shape
    H = MLA_HEADS
    lat = h @ w_in
    c_q = rms_norm(lat[..., :Q_LORA_RANK], q_norm)
    c_kv = rms_norm(lat[..., Q_LORA_RANK:Q_LORA_RANK + KV_LORA_RANK], kv_norm)
    k_rope = lat[..., Q_LORA_RANK + KV_LORA_RANK:]
    q = (c_q @ w_uq).reshape(B, S, H, QK_NOPE_DIM + QK_ROPE_DIM)
    kv = (c_kv @ w_ukv).reshape(B, S, H, QK_NOPE_DIM + V_HEAD_DIM)
    q_nope, q_rope = q[..., :QK_NOPE_DIM], q[..., QK_NOPE_DIM:]
    k_nope, v = kv[..., :QK_NOPE_DIM], kv[..., QK_NOPE_DIM:]
    inv_freq = ROPE_THETA ** (-jnp.arange(0, QK_ROPE_DIM, 2, dtype=jnp.float32) / QK_ROPE_DIM)
    ang = positions.astype(jnp.float32)[..., None] * inv_freq
    cos, sin = jnp.cos(ang), jnp.sin(ang)
    q_rope = apply_rope(q_rope, cos[:, :, None, :], sin[:, :, None, :])
    k_rope = apply_rope(k_rope, cos, sin)
    scale = (QK_NOPE_DIM + QK_ROPE_DIM) ** -0.5
    nb = S // Q_BLOCK
    qn_blk = q_nope.reshape(B, nb, Q_BLOCK, H, QK_NOPE_DIM).transpose(1, 0, 2, 3, 4)
    qr_blk = q_rope.reshape(B, nb, Q_BLOCK, H, QK_ROPE_DIM).transpose(1, 0, 2, 3, 4)
    key_idx = jnp.arange(S)

    def attend(args):
        qn, qr, blk = args
        s = jnp.einsum('bqhd,bkhd->bhqk', qn, k_nope) + jnp.einsum('bqhd,bkd->bhqk', qr, k_rope)
        s = s.astype(jnp.float32) * scale
        q_idx = blk * Q_BLOCK + jnp.arange(Q_BLOCK)
        s = jnp.where(key_idx[None, :] <= q_idx[:, None], s, -jnp.inf)
        p = jax.nn.softmax(s, axis=-1).astype(v.dtype)
        return jnp.einsum('bhqk,bkhd->bqhd', p, v)

    o = lax.map(attend, (qn_blk, qr_blk, jnp.arange(nb)))
    o = o.transpose(1, 0, 2, 3, 4).reshape(B, S, H * V_HEAD_DIM)
    return o @ w_o


def mlstm_mixer(h, w_in, b_gates, head_norm, w_out):
    B, S, _ = h.shape
    H, DK, DV, L = MLSTM_HEADS, MLSTM_QK_DIM, MLSTM_V_DIM, CHUNK
    nc = S // L
    o1 = H * DK
    o2 = 2 * H * DK
    o3 = o2 + H * DV
    o4 = o3 + H * DV
    proj = h @ w_in
    q, k, v, o_pre = proj[..., :o1], proj[..., o1:o2], proj[..., o2:o3], proj[..., o3:o4]
    gates = (proj[..., o4:] + b_gates).astype(jnp.float32)
    gates = GATE_SOFTCAP * jnp.tanh(gates / GATE_SOFTCAP)
    log_i = gates[..., :H]
    log_f = jax.nn.log_sigmoid(gates[..., H:])

    def to_chunks(t, d):
        return t.astype(jnp.float32).reshape(B, nc, L, H, d).transpose(1, 0, 3, 2, 4)

    def gate_chunks(g):
        return g.reshape(B, nc, L, H).transpose(1, 0, 3, 2)

    qc = to_chunks(q, DK) * (DK ** -0.5)
    kc = to_chunks(k, DK)
    vc = to_chunks(v, DV)
    causal = jnp.tril(jnp.ones((L, L), dtype=bool))

    def step(carry, xs):
        C, n, m = carry
        qb, kb, vb, lf, li = xs
        b = jnp.cumsum(lf, axis=-1)
        d = b[..., :, None] - b[..., None, :] + li[..., None, :]
        d = jnp.where(causal, d, -jnp.inf)
        m_inter = b + m[..., None]
        m_t = jnp.maximum(m_inter, jnp.max(d, axis=-1))
        w = jnp.exp(d - m_t[..., None])
        inter = jnp.exp(m_inter - m_t)
        qk = jnp.einsum('bhtd,bhsd->bhts', qb, kb) * w
        num = inter[..., None] * jnp.einsum('bhtd,bhde->bhte', qb, C) + jnp.einsum('bhts,bhse->bhte', qk, vb)
        den = inter * jnp.einsum('bhtd,bhd->bht', qb, n) + jnp.sum(qk, axis=-1)
        h_t = num / jnp.maximum(jnp.abs(den), jnp.exp(-m_t))[..., None]
        m_new = m_t[..., -1]
        decay = jnp.exp(b[..., -1] + m - m_new)
        ws = jnp.exp(b[..., -1:] - b + li - m_new[..., None])
        C_new = decay[..., None, None] * C + jnp.einsum('bhs,bhsd,bhse->bhde', ws, kb, vb)
        n_new = decay[..., None] * n + jnp.einsum('bhs,bhsd->bhd', ws, kb)
        return (C_new, n_new, m_new), h_t

    init = (jnp.zeros((B, H, DK, DV), jnp.float32), jnp.zeros((B, H, DK), jnp.float32),
            jnp.zeros((B, H), jnp.float32))
    _, hs = lax.scan(step, init, (qc, kc, vc, gate_chunks(log_f), gate_chunks(log_i)))
    hs = hs.transpose(1, 0, 3, 2, 4).reshape(B, S, H, DV)
    hs = rms_norm(hs, head_norm.reshape(H, DV)).reshape(B, S, H * DV).astype(h.dtype)
    return (jax.nn.sigmoid(o_pre) * hs) @ w_out


def moe_ffn(h, w_router, b_router, w_gu, b_gu, w_down, b_down):
    B, S, D = h.shape
    T = B * S
    M = T * TOP_K
    xt = h.reshape(T, D)
    logits = (xt @ w_router + b_router).astype(jnp.float32)
    top_vals, top_idx = lax.top_k(logits, TOP_K)
    gates = jax.nn.softmax(top_vals, axis=-1).astype(h.dtype)
    e_flat = top_idx.reshape(M)
    order = jnp.argsort(e_flat)
    sorted_e = e_flat[order]
    counts = jnp.bincount(e_flat, length=N_EXPERTS)
    starts = jnp.cumsum(counts) - counts
    padded = ((counts + ROW_BLOCK - 1) // ROW_BLOCK) * ROW_BLOCK
    pends = jnp.cumsum(padded)
    pstarts = pends - padded
    dest_sorted = pstarts[sorted_e] + (jnp.arange(M) - starts[sorted_e])
    dest = jnp.zeros((M,), jnp.int32).at[order].set(dest_sorted.astype(jnp.int32))
    n_blocks = -(-M // ROW_BLOCK) + N_EXPERTS
    n_rows = n_blocks * ROW_BLOCK
    row_tok = jnp.full((n_rows,), T, jnp.int32).at[dest].set(jnp.arange(M, dtype=jnp.int32) // TOP_K)
    x_pad = jnp.concatenate([xt, jnp.zeros((1, D), xt.dtype)], axis=0)
    x_rows = x_pad[row_tok].reshape(n_blocks, ROW_BLOCK, D)
    block_start = jnp.arange(n_blocks) * ROW_BLOCK
    block_e = jnp.minimum(jnp.searchsorted(pends, block_start, side='right'), N_EXPERTS - 1)

    def expert_block(args):
        xb, e = args
        gu = xb @ w_gu[e] + b_gu[e]
        g = jnp.minimum(gu[:, :D_EXPERT], SWIGLU_LIMIT)
        u = jnp.clip(gu[:, D_EXPERT:], -SWIGLU_LIMIT, SWIGLU_LIMIT)
        act = g * jax.nn.sigmoid(SWIGLU_ALPHA * g) * (u + 1.0)
        return act @ w_down[e] + b_down[e]

    out_rows = lax.map(expert_block, (x_rows, block_e)).reshape(n_rows, D)
    y = jnp.sum(out_rows[dest].reshape(T, TOP_K, D) * gates[..., None], axis=1)
    return y.reshape(B, S, D)


def setup_inputs(seed: int = 0) -> dict:
    key = jax.random.key(seed)
    ks = jax.random.split(key, 24)
    f32 = jnp.float32

    def nrm(k, shape, scale):
        return jax.random.normal(k, shape, f32) * scale

    D = D_MODEL
    H = MLA_HEADS
    mla_in = Q_LORA_RANK + KV_LORA_RANK + QK_ROPE_DIM
    mh, dk, dv = MLSTM_HEADS, MLSTM_QK_DIM, MLSTM_V_DIM
    mlstm_in = 2 * mh * dk + 2 * mh * dv + 2 * mh
    x = nrm(ks[0], (BATCH, SEQ, D), 1.0)
    c = nrm(ks[1], (BATCH, D), 1.0)
    positions = (jax.random.randint(ks[2], (BATCH, 1), 0, 4096, dtype=jnp.int32)
                 + jnp.arange(SEQ, dtype=jnp.int32)[None, :])
    mla_w_in = nrm(ks[3], (N_MLA_LAYERS, D, mla_in), D ** -0.5)
    mla_q_norm = 1.0 + nrm(ks[4], (N_MLA_LAYERS, Q_LORA_RANK), 0.1)
    mla_kv_norm = 1.0 + nrm(ks[5], (N_MLA_LAYERS, KV_LORA_RANK), 0.1)
    mla_w_uq = nrm(ks[6], (N_MLA_LAYERS, Q_LORA_RANK, H * (QK_NOPE_DIM + QK_ROPE_DIM)), Q_LORA_RANK ** -0.5)
    mla_w_ukv = nrm(ks[7], (N_MLA_LAYERS, KV_LORA_RANK, H * (QK_NOPE_DIM + V_HEAD_DIM)), KV_LORA_RANK ** -0.5)
    mla_w_o = nrm(ks[8], (N_MLA_LAYERS, H * V_HEAD_DIM, D), (H * V_HEAD_DIM) ** -0.5 * DEEPNORM_BETA)
    mlstm_w_in = nrm(ks[9], (N_MLSTM_LAYERS, D, mlstm_in), D ** -0.5)
    i_bias = nrm(ks[10], (N_MLSTM_LAYERS, mh), 0.1)
    f_bias = 3.0 + nrm(ks[11], (N_MLSTM_LAYERS, mh), 0.5)
    mlstm_b_gates = jnp.concatenate([i_bias, f_bias], axis=-1)
    mlstm_head_norm = 1.0 + nrm(ks[12], (N_MLSTM_LAYERS, mh * dv), 0.1)
    mlstm_w_out = nrm(ks[13], (N_MLSTM_LAYERS, mh * dv, D), (mh * dv) ** -0.5 * DEEPNORM_BETA)
    moe_w_router = nrm(ks[14], (DEPTH, D, N_EXPERTS), D ** -0.5)
    moe_b_router = nrm(ks[15], (DEPTH, N_EXPERTS), 0.01)
    moe_w_gu = nrm(ks[16], (DEPTH, N_EXPERTS, D, 2 * D_EXPERT), D ** -0.5)
    moe_b_gu = nrm(ks[17], (DEPTH, N_EXPERTS, 2 * D_EXPERT), 0.02)
    moe_w_down = nrm(ks[18], (DEPTH, N_EXPERTS, D_EXPERT, D), D_EXPERT ** -0.5 * DEEPNORM_BETA)
    moe_b_down = nrm(ks[19], (DEPTH, N_EXPERTS, D), 0.02)
    ada_w = nrm(ks[20], (DEPTH, 2, D, 3 * D), D ** -0.5)
    ada_b = nrm(ks[21], (DEPTH, 2, 3 * D), 0.02)
    ln_g = 1.0 + nrm(ks[22], (DEPTH, 2, D), 0.1)
    ln_b = nrm(ks[23], (DEPTH, 2, D), 0.02)
    return {'x': x, 'c': c, 'positions': positions,
            'mla_w_in': mla_w_in, 'mla_q_norm': mla_q_norm, 'mla_kv_norm': mla_kv_norm,
            'mla_w_uq': mla_w_uq, 'mla_w_ukv': mla_w_ukv, 'mla_w_o': mla_w_o,
            'mlstm_w_in': mlstm_w_in, 'mlstm_b_gates': mlstm_b_gates,
            'mlstm_head_norm': mlstm_head_norm, 'mlstm_w_out': mlstm_w_out,
            'moe_w_router': moe_w_router, 'moe_b_router': moe_b_router,
            'moe_w_gu': moe_w_gu, 'moe_b_gu': moe_b_gu,
            'moe_w_down': moe_w_down, 'moe_b_down': moe_b_down,
            'ada_w': ada_w, 'ada_b': ada_b, 'ln_g': ln_g, 'ln_b': ln_b}


def reference(x, c, positions, mla_w_in, mla_q_norm, mla_kv_norm, mla_w_uq, mla_w_ukv, mla_w_o,
              mlstm_w_in, mlstm_b_gates, mlstm_head_norm, mlstm_w_out,
              moe_w_router, moe_b_router, moe_w_gu, moe_b_gu, moe_w_down, moe_b_down,
              ada_w, ada_b, ln_g, ln_b):
    cond = jax.nn.silu(c)
    for i in range(DEPTH):
        mod = jnp.einsum('bd,sde->sbe', cond, ada_w[i]) + ada_b[i][:, None, :]
        shift, scale, gate = jnp.split(mod, 3, axis=-1)
        h = x * (1.0 + scale[0][:, None, :]) + shift[0][:, None, :]
        j = i // N_MIXERS
        if i % N_MIXERS == 0:
            y = mla_mixer(h, positions, mla_w_in[j], mla_q_norm[j], mla_kv_norm[j],
                          mla_w_uq[j], mla_w_ukv[j], mla_w_o[j])
        else:
            y = mlstm_mixer(h, mlstm_w_in[j], mlstm_b_gates[j], mlstm_head_norm[j], mlstm_w_out[j])
        x = layer_norm(DEEPNORM_ALPHA * x + gate[0][:, None, :] * y, ln_g[i, 0], ln_b[i, 0])
        h = x * (1.0 + scale[1][:, None, :]) + shift[1][:, None, :]
        y = moe_ffn(h, moe_w_router[i], moe_b_router[i], moe_w_gu[i], moe_b_gu[i],
                    moe_w_down[i], moe_b_down[i])
        x = layer_norm(DEEPNORM_ALPHA * x + gate[1][:, None, :] * y, ln_g[i, 1], ln_b[i, 1])
    return x
```

```python
import functools

import jax
import jax.numpy as jnp
from jax import lax
from jax.experimental import pallas as pl
from jax.experimental.pallas import tpu as pltpu

F32 = jnp.float32
BF16 = jnp.bfloat16

D_MODEL = 2048
DEPTH = 2

MLA_HEADS = 16
QK_NOPE_DIM = 128
QK_ROPE_DIM = 64
V_HEAD_DIM = 128
Q_LORA_RANK = 512
KV_LORA_RANK = 512
ROPE_THETA = 10000.0

MLSTM_HEADS = 8
MLSTM_QK_DIM = D_MODEL // (2 * MLSTM_HEADS)
MLSTM_V_DIM = D_MODEL // MLSTM_HEADS
GATE_SOFTCAP = 15.0

N_EXPERTS = 32
TOP_K = 4
D_EXPERT = D_MODEL
SWIGLU_ALPHA = 1.702
SWIGLU_LIMIT = 7.0

DEEPNORM_ALPHA = (2 * DEPTH) ** 0.25
NORM_EPS = 1e-6

LANES = 128
V7X_VMEM_BYTES = 64 * 1024 * 1024
NEG_INF = float("-inf")


def _vmem_limit(estimate_bytes):
    return int(min(estimate_bytes + (12 << 20), V7X_VMEM_BYTES - (8 << 20)))


def _tiles(batch, seq):
    t = batch * seq
    cfg = dict(
        tm_tok=min(256, seq),
        tq=min(256, seq),
        chunk=min(128, seq),
        heads_per_step=2,
        tm_moe=min(512, t * TOP_K // 8),
        tf=512,
        tn=1024,
        r_gather=min(256, t * TOP_K // 8),
        tc=min(128, seq),
        tn_ada=768,
    )
    return cfg


def _adaln_kernel(c_ref, w_ref, b_ref, o_ref):
    c = c_ref[...]
    cond = c * jax.nn.sigmoid(c)
    o_ref[...] = jnp.dot(cond.astype(BF16), w_ref[...].astype(BF16),
                         preferred_element_type=F32) + b_ref[...]


def _adaln(c, ada_w, ada_b, tn):
    b, d = c.shape
    n_mod = ada_w.shape[0] * ada_w.shape[1]
    w = ada_w.reshape(n_mod, d, 3 * d)
    bias = ada_b.reshape(n_mod, 1, 3 * d)
    out = pl.pallas_call(
        _adaln_kernel,
        grid=(n_mod, 3 * d // tn),
        in_specs=[
            pl.BlockSpec((b, d), lambda i, j: (0, 0)),
            pl.BlockSpec((None, d, tn), lambda i, j: (i, 0, j)),
            pl.BlockSpec((None, 1, tn), lambda i, j: (i, 0, j)),
        ],
        out_specs=pl.BlockSpec((None, b, tn), lambda i, j: (i, 0, j)),
        out_shape=jax.ShapeDtypeStruct((n_mod, b, 3 * d), F32),
        compiler_params=pltpu.CompilerParams(
            dimension_semantics=("arbitrary", "arbitrary"),
            vmem_limit_bytes=_vmem_limit(2 * d * tn * 4 + d * tn * 2)),
        name="adaln",
    )(c, w, bias)
    return out.reshape(n_mod, b, 1, 3 * d)


def _mod_spec(k, part, tiles_per_batch):
    return pl.BlockSpec((None, None, 1, D_MODEL),
                        lambda i, *_: (k, i // tiles_per_batch, 0, part))


def _row_spec(k):
    return pl.BlockSpec((None, 1, D_MODEL), lambda i, *_: (k, 0, 0))


def _resident(shape):
    nd = len(shape)
    return pl.BlockSpec(shape, lambda *_: (0,) * nd, pipeline_mode=pl.Buffered(1))


def _mla_proj_kernel(x_ref, shift_ref, scale_ref, pos_ref, rope_ref, win_ref, qnorm_ref, kvnorm_ref,
                     wqn_ref, wqr_ref, wqs_ref, wkn_ref, wv_ref,
                     qn_ref, qr_ref, kn_ref, v_ref, kr_ref):
    h = x_ref[...] * (1.0 + scale_ref[...]) + shift_ref[...]
    lat = jnp.dot(h.astype(BF16), win_ref[...], preferred_element_type=F32)
    cq = lat[:, :Q_LORA_RANK]
    ckv = lat[:, Q_LORA_RANK:Q_LORA_RANK + KV_LORA_RANK]
    kr = lat[:, Q_LORA_RANK + KV_LORA_RANK:Q_LORA_RANK + KV_LORA_RANK + LANES]
    kr_sw = lat[:, Q_LORA_RANK + KV_LORA_RANK + LANES:]
    cq = cq * lax.rsqrt(jnp.mean(cq * cq, axis=-1, keepdims=True) + NORM_EPS) * qnorm_ref[...]
    ckv = ckv * lax.rsqrt(jnp.mean(ckv * ckv, axis=-1, keepdims=True) + NORM_EPS) * kvnorm_ref[...]
    cq = cq.astype(BF16)
    ckv = ckv.astype(BF16)

    ang = pos_ref[...].astype(F32) * rope_ref[0:1, :]
    cos_t = jnp.cos(ang) * rope_ref[1:2, :]
    sin_t = jnp.sin(ang) * rope_ref[2:3, :]

    qn_ref[...] = jnp.dot(cq, wqn_ref[...], preferred_element_type=F32).astype(BF16)
    q_rope = jnp.dot(cq, wqr_ref[...], preferred_element_type=F32)
    q_swap = jnp.dot(cq, wqs_ref[...], preferred_element_type=F32)
    cos_h = jnp.tile(cos_t, (1, MLA_HEADS))
    sin_h = jnp.tile(sin_t, (1, MLA_HEADS))
    qr_ref[...] = (q_rope * cos_h + q_swap * sin_h).astype(BF16)
    kn_ref[...] = jnp.dot(ckv, wkn_ref[...], preferred_element_type=F32).astype(BF16)
    v_ref[...] = jnp.dot(ckv, wv_ref[...], preferred_element_type=F32).astype(BF16)
    kr_ref[...] = (kr * cos_t + kr_sw * sin_t).astype(BF16)


def _rope_table():
    half = QK_ROPE_DIM // 2
    inv_freq = ROPE_THETA ** (-jnp.arange(0, QK_ROPE_DIM, 2, dtype=F32) / QK_ROPE_DIM)
    zeros = jnp.zeros((LANES - QK_ROPE_DIM,), F32)
    freq_row = jnp.concatenate([inv_freq, inv_freq, zeros])
    cos_mask = jnp.concatenate([jnp.ones((QK_ROPE_DIM,), F32), zeros])
    sin_sign = jnp.concatenate([-jnp.ones((half,), F32), jnp.ones((half,), F32), zeros])
    pad = jnp.zeros((5, LANES), F32)
    return jnp.concatenate([jnp.stack([freq_row, cos_mask, sin_sign]), pad], axis=0)


def _mla_weights(w_in, w_uq, w_ukv):
    d = w_in.shape[0]
    h = MLA_HEADS
    half = QK_ROPE_DIM // 2
    lat = Q_LORA_RANK + KV_LORA_RANK
    kr = w_in[:, lat:]
    zpad = jnp.zeros((d, LANES - QK_ROPE_DIM), w_in.dtype)
    win = jnp.concatenate([w_in[:, :lat], kr, zpad, kr[:, half:], kr[:, :half], zpad], axis=1)
    wq = w_uq.reshape(Q_LORA_RANK, h, QK_NOPE_DIM + QK_ROPE_DIM)
    wqn = wq[:, :, :QK_NOPE_DIM].reshape(Q_LORA_RANK, h * QK_NOPE_DIM)
    rope = wq[:, :, QK_NOPE_DIM:]
    zq = jnp.zeros((Q_LORA_RANK, h, LANES - QK_ROPE_DIM), w_uq.dtype)
    wqr = jnp.concatenate([rope, zq], axis=-1).reshape(Q_LORA_RANK, h * LANES)
    wqs = jnp.concatenate([rope[..., half:], rope[..., :half], zq], axis=-1).reshape(Q_LORA_RANK, h * LANES)
    wkv = w_ukv.reshape(KV_LORA_RANK, h, QK_NOPE_DIM + V_HEAD_DIM)
    wkn = wkv[:, :, :QK_NOPE_DIM].reshape(KV_LORA_RANK, h * QK_NOPE_DIM)
    wv = wkv[:, :, QK_NOPE_DIM:].reshape(KV_LORA_RANK, h * V_HEAD_DIM)
    return tuple(a.astype(BF16) for a in (win, wqn, wqr, wqs, wkn, wv))


def _mla_proj(x2d, mod, k_mod, positions, w_in, q_norm, kv_norm, w_uq, w_ukv, seq, tm):
    t, d = x2d.shape
    tiles_per_batch = seq // tm
    win, wqn, wqr, wqs, wkn, wv = _mla_weights(w_in, w_uq, w_ukv)
    hn = MLA_HEADS * LANES
    tok = lambda n: pl.BlockSpec((tm, n), lambda i: (i, 0))
    weights_bytes = 2 * (win.size + wqn.size + wqr.size + wqs.size + wkn.size + wv.size)
    est = weights_bytes + 2 * tm * d * 4 + 2 * (4 * tm * hn * 2 + tm * LANES * 2) + 8 * tm * hn * 4
    outs = pl.pallas_call(
        _mla_proj_kernel,
        grid=(t // tm,),
        in_specs=[
            tok(d),
            _mod_spec(k_mod, 0, tiles_per_batch),
            _mod_spec(k_mod, 1, tiles_per_batch),
            pl.BlockSpec((tm, 1), lambda i: (i, 0)),
            _resident((8, LANES)),
            _resident(win.shape),
            _resident((1, Q_LORA_RANK)),
            _resident((1, KV_LORA_RANK)),
            _resident(wqn.shape), _resident(wqr.shape), _resident(wqs.shape),
            _resident(wkn.shape), _resident(wv.shape),
        ],
        out_specs=[tok(hn), tok(hn), tok(hn), tok(hn), tok(LANES)],
        out_shape=[jax.ShapeDtypeStruct((t, hn), BF16)] * 4 + [jax.ShapeDtypeStruct((t, LANES), BF16)],
        compiler_params=pltpu.CompilerParams(
            dimension_semantics=("arbitrary",), vmem_limit_bytes=_vmem_limit(est)),
        name="mla_proj",
    )(x2d, mod, mod, positions.reshape(t, 1), _rope_table(), win,
      q_norm.reshape(1, -1), kv_norm.reshape(1, -1), wqn, wqr, wqs, wkn, wv)
    return outs


def _attn_kernel(qn_ref, qr_ref, kn_ref, kr_ref, v_ref, o_ref, *, tq, scale):
    seq = qn_ref.shape[0]
    row = lax.broadcasted_iota(jnp.int32, (tq, tq), 0)
    col = lax.broadcasted_iota(jnp.int32, (tq, tq), 1)
    causal = col <= row

    def q_block(qi, _):
        r0 = pl.multiple_of(qi * tq, tq)
        q = jnp.concatenate([qn_ref[pl.ds(r0, tq), :], qr_ref[pl.ds(r0, tq), :]], axis=-1)

        def kv_step(j, carry, masked):
            m, l, acc = carry
            c0 = pl.multiple_of(j * tq, tq)
            k = jnp.concatenate([kn_ref[pl.ds(c0, tq), :], kr_ref[pl.ds(c0, tq), :]], axis=-1)
            s = lax.dot_general(q, k, (((1,), (1,)), ((), ())), preferred_element_type=F32) * scale
            if masked:
                s = jnp.where(causal, s, NEG_INF)
            m_new = jnp.maximum(m, jnp.max(s, axis=-1, keepdims=True))
            alpha = jnp.exp(m - m_new)
            p = jnp.exp(s - m_new)
            l = alpha * l + jnp.sum(p, axis=-1, keepdims=True)
            acc = alpha * acc + jnp.dot(p.astype(BF16), v_ref[pl.ds(c0, tq), :],
                                        preferred_element_type=F32)
            return m_new, l, acc

        init = (jnp.full((tq, 1), NEG_INF, F32), jnp.zeros((tq, 1), F32),
                jnp.zeros((tq, V_HEAD_DIM), F32))
        carry = lax.fori_loop(0, qi, lambda j, c: kv_step(j, c, False), init)
        _, l, acc = kv_step(qi, carry, True)
        o_ref[pl.ds(r0, tq), :] = (acc / l).astype(BF16)
        return 0

    lax.fori_loop(0, seq // tq, q_block, 0)


def _attention(qn, qr, kn, kr, v, batch, seq, tq):
    t = qn.shape[0]
    head = lambda: pl.BlockSpec((seq, LANES), lambda b, h: (b, h))
    scale = (QK_NOPE_DIM + QK_ROPE_DIM) ** -0.5
    est = 2 * 6 * seq * LANES * 2 + 6 * tq * tq * 4
    return pl.pallas_call(
        functools.partial(_attn_kernel, tq=tq, scale=scale),
        grid=(batch, MLA_HEADS),
        in_specs=[head(), head(), head(), pl.BlockSpec((seq, LANES), lambda b, h: (b, 0)), head()],
        out_specs=head(),
        out_shape=jax.ShapeDtypeStruct((t, MLA_HEADS * V_HEAD_DIM), BF16),
        compiler_params=pltpu.CompilerParams(
            dimension_semantics=("arbitrary", "arbitrary"), vmem_limit_bytes=_vmem_limit(est)),
        name="mla_attention",
    )(qn, qr, kn, kr, v)


def _layer_norm(z, g, b):
    mu = jnp.mean(z, axis=-1, keepdims=True)
    zc = z - mu
    var = jnp.mean(zc * zc, axis=-1, keepdims=True)
    return zc * lax.rsqrt(var + NORM_EPS) * g + b


def _post_kernel(a_ref, x_ref, gate_ref, lng_ref, lnb_ref, shift_ref, scale_ref, wo_ref, wr_ref, br_ref,
                 x1_ref, h2_ref, topi_ref, topg_ref):
    y = jnp.dot(a_ref[...], wo_ref[...], preferred_element_type=F32)
    z = DEEPNORM_ALPHA * x_ref[...] + gate_ref[...] * y
    x1 = _layer_norm(z, lng_ref[...], lnb_ref[...])
    x1_ref[...] = x1
    h2 = x1 * (1.0 + scale_ref[...]) + shift_ref[...]
    h2_ref[...] = h2
    logits = jnp.dot(h2, wr_ref[...], precision=lax.Precision.HIGHEST,
                     preferred_element_type=F32) + br_ref[...]
    tm = logits.shape[0]
    lane = lax.broadcasted_iota(jnp.int32, (tm, LANES), 1)
    vals, idxs = [], []
    for _ in range(TOP_K):
        mk = jnp.max(logits, axis=-1, keepdims=True)
        ik = jnp.min(jnp.where(logits == mk, lane, LANES), axis=-1, keepdims=True)
        vals.append(mk)
        idxs.append(ik)
        logits = jnp.where(lane == ik, NEG_INF, logits)
    exps = [jnp.exp(v - vals[0]) for v in vals]
    den = exps[0]
    for e in exps[1:]:
        den = den + e
    topi = jnp.zeros((tm, LANES), jnp.int32)
    topg = jnp.zeros((tm, LANES), F32)
    for k in range(TOP_K):
        topi = jnp.where(lane == k, idxs[k], topi)
        topg = jnp.where(lane == k, exps[k] / den, topg)
    topi_ref[...] = topi
    topg_ref[...] = topg


def _post(a, x2d, mod, k_mix, k_ffn, ln_g, ln_b, w_o, w_router, b_router, seq, tm):
    t, d = x2d.shape
    tiles_per_batch = seq // tm
    wr = jnp.concatenate([w_router, jnp.zeros((d, LANES - N_EXPERTS), F32)], axis=1)
    br = jnp.concatenate([b_router, jnp.full((LANES - N_EXPERTS,), NEG_INF, F32)]).reshape(1, LANES)
    tok = lambda n: pl.BlockSpec((tm, n), lambda i: (i, 0))
    est = d * d * 2 + d * LANES * 4 + 2 * tm * d * (2 + 4 + 4 + 4) + 6 * tm * d * 4
    return pl.pallas_call(
        _post_kernel,
        grid=(t // tm,),
        in_specs=[
            tok(d), tok(d),
            _mod_spec(k_mix, 2, tiles_per_batch),
            _resident((1, d)), _resident((1, d)),
            _mod_spec(k_ffn, 0, tiles_per_batch),
            _mod_spec(k_ffn, 1, tiles_per_batch),
            _resident((d, d)), _resident((d, LANES)), _resident((1, LANES)),
        ],
        out_specs=[tok(d), tok(d), tok(LANES), tok(LANES)],
        out_shape=[jax.ShapeDtypeStruct((t, d), F32), jax.ShapeDtypeStruct((t, d), F32),
                   jax.ShapeDtypeStruct((t, LANES), jnp.int32), jax.ShapeDtypeStruct((t, LANES), F32)],
        compiler_params=pltpu.CompilerParams(
            dimension_semantics=("arbitrary",), vmem_limit_bytes=_vmem_limit(est)),
        name="mixer_out_ln_router",
    )(a, x2d, mod, ln_g.reshape(1, d), ln_b.reshape(1, d), mod, mod, w_o.astype(BF16), wr, br)


def _route(topi, tm_moe):
    t = topi.shape[0]
    m = t * TOP_K
    e_flat = topi[:, :TOP_K].reshape(m)
    onehot = (e_flat[:, None] == jnp.arange(N_EXPERTS, dtype=jnp.int32)[None, :]).astype(jnp.int32)
    csum = jnp.cumsum(onehot, axis=0)
    counts = csum[-1]
    rank = jnp.take_along_axis(csum, e_flat[:, None], axis=1)[:, 0] - 1
    padded = ((counts + tm_moe - 1) // tm_moe) * tm_moe
    pends = jnp.cumsum(padded)
    pstarts = pends - padded
    dest = (pstarts[e_flat] + rank).astype(jnp.int32)
    n_tiles = m // tm_moe + N_EXPERTS
    row_tok = jnp.zeros((n_tiles * tm_moe,), jnp.int32).at[dest].set(
        jnp.arange(m, dtype=jnp.int32) // TOP_K)
    tile_start = jnp.arange(n_tiles, dtype=jnp.int32) * tm_moe
    tile_e = jnp.minimum(jnp.searchsorted(pends, tile_start, side="right"), N_EXPERTS - 1).astype(jnp.int32)
    n_used = (pends[-1] // tm_moe).astype(jnp.int32).reshape(1)
    return dest, row_tok, tile_e, n_used


def _gather_issue(idx_ref, src_hbm, buf, sem, slot, n_rows):
    def body(r, carry):
        tok = idx_ref[0, 0, r]
        pltpu.make_async_copy(src_hbm.at[pl.ds(tok, 1), :], buf.at[slot, pl.ds(r, 1), :],
                              sem.at[slot]).start()
        return carry
    lax.fori_loop(0, n_rows, body, 0)


def _gather_wait(src_hbm, buf, sem, slot, n_rows):
    pltpu.make_async_copy(src_hbm.at[pl.ds(0, n_rows), :], buf.at[slot], sem.at[slot]).wait()


def _dispatch_kernel(idx_ref, nxt_ref, src_hbm, o_ref, buf, sem, *, n_rows):
    i = pl.program_id(0)
    n = pl.num_programs(0)
    slot = lax.rem(i, 2)

    @pl.when(i == 0)
    def _():
        _gather_issue(idx_ref, src_hbm, buf, sem, 0, n_rows)

    @pl.when(i + 1 < n)
    def _():
        _gather_issue(nxt_ref, src_hbm, buf, sem, 1 - slot, n_rows)

    _gather_wait(src_hbm, buf, sem, slot, n_rows)
    o_ref[...] = buf[slot].astype(BF16)


def _dispatch(h2, row_tok, r):
    t, d = h2.shape
    n_rows = row_tok.shape[0]
    n = n_rows // r
    idx = row_tok.reshape(n, 1, r)
    smem = lambda f: pl.BlockSpec((1, 1, r), f, memory_space=pltpu.SMEM)
    return pl.pallas_call(
        functools.partial(_dispatch_kernel, n_rows=r),
        grid=(n,),
        in_specs=[smem(lambda i: (i, 0, 0)),
                  smem(lambda i: (jnp.minimum(i + 1, n - 1), 0, 0)),
                  pl.BlockSpec(memory_space=pl.ANY)],
        out_specs=pl.BlockSpec((r, d), lambda i: (i, 0)),
        out_shape=jax.ShapeDtypeStruct((n_rows, d), BF16),
        scratch_shapes=[pltpu.VMEM((2, r, d), F32), pltpu.SemaphoreType.DMA((2,))],
        compiler_params=pltpu.CompilerParams(
            dimension_semantics=("arbitrary",),
            vmem_limit_bytes=_vmem_limit(2 * r * d * 4 + 2 * r * d * 2)),
        name="moe_dispatch_gather",
    )(idx, idx, h2)


def _expert_changed(te_ref, i):
    prev = te_ref[jnp.maximum(i - 1, 0)]
    return jnp.logical_or(i == 0, te_ref[i] != prev)


def _gemm1_kernel(te_ref, nu_ref, x_ref, wg_ref, wu_ref, bg_ref, bu_ref, o_ref, wg_bf, wu_bf):
    i = pl.program_id(1)

    @pl.when(_expert_changed(te_ref, i))
    def _():
        wg_bf[...] = wg_ref[...].astype(BF16)
        wu_bf[...] = wu_ref[...].astype(BF16)

    @pl.when(i < nu_ref[0])
    def _():
        x = x_ref[...]
        g = jnp.dot(x, wg_bf[...], preferred_element_type=F32) + bg_ref[...]
        u = jnp.dot(x, wu_bf[...], preferred_element_type=F32) + bu_ref[...]
        g = jnp.minimum(g, SWIGLU_LIMIT)
        u = jnp.clip(u, -SWIGLU_LIMIT, SWIGLU_LIMIT)
        o_ref[...] = (g * jax.nn.sigmoid(SWIGLU_ALPHA * g) * (u + 1.0)).astype(BF16)

    @pl.when(i >= nu_ref[0])
    def _():
        o_ref[...] = jnp.zeros(o_ref.shape, o_ref.dtype)


def _gemm1(x_rows, tile_e, n_used, w_gu, b_gu, layer, tm, tf):
    n_rows, d = x_rows.shape
    f = D_EXPERT
    nf = f // tf
    n_tiles = n_rows // tm
    bias = b_gu.reshape(b_gu.shape[0], N_EXPERTS, 1, 2 * f)
    wspec = lambda off: pl.BlockSpec((None, None, d, tf), lambda j, i, te, nu: (layer, te[i], 0, j + off))
    bspec = lambda off: pl.BlockSpec((None, None, 1, tf), lambda j, i, te, nu: (layer, te[i], 0, j + off))
    est = 2 * 2 * d * tf * 4 + 2 * d * tf * 2 + 2 * tm * d * 2 + 2 * tm * tf * 2 + 4 * tm * tf * 4
    return pl.pallas_call(
        _gemm1_kernel,
        grid_spec=pltpu.PrefetchScalarGridSpec(
            num_scalar_prefetch=2,
            grid=(nf, n_tiles),
            in_specs=[pl.BlockSpec((tm, d), lambda j, i, te, nu: (i, 0)),
                      wspec(0), wspec(nf), bspec(0), bspec(nf)],
            out_specs=pl.BlockSpec((tm, tf), lambda j, i, te, nu: (i, j)),
            scratch_shapes=[pltpu.VMEM((d, tf), BF16), pltpu.VMEM((d, tf), BF16)]),
        out_shape=jax.ShapeDtypeStruct((n_rows, f), BF16),
        compiler_params=pltpu.CompilerParams(
            dimension_semantics=("arbitrary", "arbitrary"), vmem_limit_bytes=_vmem_limit(est)),
        name="moe_gate_up_swiglu",
    )(tile_e, n_used, x_rows, w_gu, w_gu, bias, bias)


def _gemm2_kernel(te_ref, nu_ref, a_ref, w_ref, b_ref, o_ref, w_bf):
    i = pl.program_id(1)

    @pl.when(_expert_changed(te_ref, i))
    def _():
        w_bf[...] = w_ref[...].astype(BF16)

    @pl.when(i < nu_ref[0])
    def _():
        o_ref[...] = jnp.dot(a_ref[...], w_bf[...], preferred_element_type=F32) + b_ref[...]

    @pl.when(i >= nu_ref[0])
    def _():
        o_ref[...] = jnp.zeros(o_ref.shape, o_ref.dtype)


def _gemm2(act, tile_e, n_used, w_down, b_down, layer, tm, tn):
    n_rows, f = act.shape
    d = D_MODEL
    n_tiles = n_rows // tm
    bias = b_down.reshape(b_down.shape[0], N_EXPERTS, 1, d)
    est = 2 * f * tn * 4 + f * tn * 2 + 2 * tm * f * 2 + 2 * tm * tn * 4 + 2 * tm * tn * 4
    return pl.pallas_call(
        _gemm2_kernel,
        grid_spec=pltpu.PrefetchScalarGridSpec(
            num_scalar_prefetch=2,
            grid=(d // tn, n_tiles),
            in_specs=[pl.BlockSpec((tm, f), lambda j, i, te, nu: (i, 0)),
                      pl.BlockSpec((None, None, f, tn), lambda j, i, te, nu: (layer, te[i], 0, j)),
                      pl.BlockSpec((None, None, 1, tn), lambda j, i, te, nu: (layer, te[i], 0, j))],
            out_specs=pl.BlockSpec((tm, tn), lambda j, i, te, nu: (i, j)),
            scratch_shapes=[pltpu.VMEM((f, tn), BF16)]),
        out_shape=jax.ShapeDtypeStruct((n_rows, d), F32),
        compiler_params=pltpu.CompilerParams(
            dimension_semantics=("arbitrary", "arbitrary"), vmem_limit_bytes=_vmem_limit(est)),
        name="moe_down",
    )(tile_e, n_used, act, w_down, bias)


def _combine_kernel(idx_ref, nxt_ref, rows_hbm, g_ref, x_ref, gate_ref, lng_ref, lnb_ref, o_ref,
                    buf, sem, *, tc):
    i = pl.program_id(0)
    n = pl.num_programs(0)
    slot = lax.rem(i, 2)
    n_rows = TOP_K * tc

    @pl.when(i == 0)
    def _():
        _gather_issue(idx_ref, rows_hbm, buf, sem, 0, n_rows)

    @pl.when(i + 1 < n)
    def _():
        _gather_issue(nxt_ref, rows_hbm, buf, sem, 1 - slot, n_rows)

    _gather_wait(rows_hbm, buf, sem, slot, n_rows)
    g = g_ref[...]
    y = g[:, 0:1] * buf[slot, pl.ds(0, tc), :]
    for k in range(1, TOP_K):
        y = y + g[:, k:k + 1] * buf[slot, pl.ds(k * tc, tc), :]
    z = DEEPNORM_ALPHA * x_ref[...] + gate_ref[...] * y
    o_ref[...] = _layer_norm(z, lng_ref[...], lnb_ref[...])


def _combine(out_rows, dest, topg, x1, mod, k_ffn, ln_g, ln_b, seq, tc):
    t, d = x1.shape
    n = t // tc
    tiles_per_batch = seq // tc
    idx = dest.reshape(n, tc, TOP_K).transpose(0, 2, 1).reshape(n, 1, TOP_K * tc)
    smem = lambda f: pl.BlockSpec((1, 1, TOP_K * tc), f, memory_space=pltpu.SMEM)
    tok = lambda w: pl.BlockSpec((tc, w), lambda i: (i, 0))
    return pl.pallas_call(
        functools.partial(_combine_kernel, tc=tc),
        grid=(n,),
        in_specs=[smem(lambda i: (i, 0, 0)),
                  smem(lambda i: (jnp.minimum(i + 1, n - 1), 0, 0)),
                  pl.BlockSpec(memory_space=pl.ANY),
                  tok(LANES), tok(d),
                  _mod_spec(k_ffn, 2, tiles_per_batch),
                  _resident((1, d)), _resident((1, d))],
        out_specs=tok(d),
        out_shape=jax.ShapeDtypeStruct((t, d), F32),
        scratch_shapes=[pltpu.VMEM((2, TOP_K * tc, d), F32), pltpu.SemaphoreType.DMA((2,))],
        compiler_params=pltpu.CompilerParams(
            dimension_semantics=("arbitrary",),
            vmem_limit_bytes=_vmem_limit(2 * TOP_K * tc * d * 4 + 6 * tc * d * 4)),
        name="moe_combine_ln",
    )(idx, idx, out_rows, topg, x1, mod, ln_g.reshape(1, d), ln_b.reshape(1, d))


def _moe(h2, topi, topg, x1, mod, k_ffn, ln_g, ln_b, w_gu, b_gu, w_down, b_down, layer, seq, cfg):
    dest, row_tok, tile_e, n_used = _route(topi, cfg["tm_moe"])
    x_rows = _dispatch(h2, row_tok, cfg["r_gather"])
    act = _gemm1(x_rows, tile_e, n_used, w_gu, b_gu, layer, cfg["tm_moe"], cfg["tf"])
    out_rows = _gemm2(act, tile_e, n_used, w_down, b_down, layer, cfg["tm_moe"], cfg["tn"])
    return _combine(out_rows, dest, topg, x1, mod, k_ffn, ln_g, ln_b, seq, cfg["tc"])


def _softcap(g):
    return GATE_SOFTCAP * jnp.tanh(g / GATE_SOFTCAP)


def _log_sigmoid(x):
    return jnp.minimum(x, 0.0) - jnp.log(1.0 + jnp.exp(-jnp.abs(x)))


def _mlstm_proj_kernel(x_ref, shift_ref, scale_ref, wq_ref, wk_ref, wv_ref, wo_ref, wg_ref, wgt_ref,
                       bg_ref, bgt_ref, q_ref, k_ref, v_ref, og_ref, gl_ref, glt_ref):
    h = (x_ref[...] * (1.0 + scale_ref[...]) + shift_ref[...]).astype(BF16)
    q_ref[...] = (jnp.dot(h, wq_ref[...], preferred_element_type=F32) * MLSTM_QK_DIM ** -0.5).astype(BF16)
    k_ref[...] = jnp.dot(h, wk_ref[...], preferred_element_type=F32).astype(BF16)
    v_ref[...] = jnp.dot(h, wv_ref[...], preferred_element_type=F32).astype(BF16)
    og_ref[...] = jax.nn.sigmoid(jnp.dot(h, wo_ref[...], preferred_element_type=F32)).astype(BF16)
    g = _softcap(jnp.dot(h, wg_ref[...], preferred_element_type=F32) + bg_ref[...])
    lane = lax.broadcasted_iota(jnp.int32, g.shape, 1)
    gl_ref[...] = jnp.where(lane < MLSTM_HEADS, g,
                            jnp.where(lane < 2 * MLSTM_HEADS, _log_sigmoid(g), 0.0))
    gt = _softcap(lax.dot_general(wgt_ref[...], h, (((1,), (1,)), ((), ())),
                                  preferred_element_type=F32) + bgt_ref[...])
    sub = lax.broadcasted_iota(jnp.int32, gt.shape, 0)
    glt_ref[...] = jnp.where(sub < MLSTM_HEADS, gt, _log_sigmoid(gt))


def _mlstm_proj(x2d, mod, k_mod, w_in, b_gates, seq, tm):
    t, d = x2d.shape
    tiles_per_batch = seq // tm
    hk = MLSTM_HEADS * MLSTM_QK_DIM
    hv = MLSTM_HEADS * MLSTM_V_DIM
    ng = 2 * MLSTM_HEADS
    o1, o2, o3, o4 = hk, 2 * hk, 2 * hk + hv, 2 * hk + 2 * hv
    wq, wk, wv, wo = (w_in[:, a:b].astype(BF16) for a, b in ((0, o1), (o1, o2), (o2, o3), (o3, o4)))
    wg = jnp.concatenate([w_in[:, o4:], jnp.zeros((d, LANES - ng), F32)], axis=1).astype(BF16)
    wgt = w_in[:, o4:].T.astype(BF16)
    bg = jnp.concatenate([b_gates, jnp.zeros((LANES - ng,), F32)]).reshape(1, LANES)
    bgt = b_gates.reshape(ng, 1)
    tok = lambda n: pl.BlockSpec((tm, n), lambda i: (i, 0))
    est = 2 * d * (2 * hk + 2 * hv + LANES + ng) + 2 * tm * d * 4 + 2 * tm * (2 * hk + 2 * hv) * 2 \
        + 6 * tm * hv * 4
    return pl.pallas_call(
        _mlstm_proj_kernel,
        grid=(t // tm,),
        in_specs=[tok(d), _mod_spec(k_mod, 0, tiles_per_batch), _mod_spec(k_mod, 1, tiles_per_batch),
                  _resident(wq.shape), _resident(wk.shape), _resident(wv.shape), _resident(wo.shape),
                  _resident(wg.shape), _resident(wgt.shape), _resident((1, LANES)), _resident((ng, 1))],
        out_specs=[tok(hk), tok(hk), tok(hv), tok(hv), tok(LANES),
                   pl.BlockSpec((ng, tm), lambda i: (0, i))],
        out_shape=[jax.ShapeDtypeStruct((t, hk), BF16), jax.ShapeDtypeStruct((t, hk), BF16),
                   jax.ShapeDtypeStruct((t, hv), BF16), jax.ShapeDtypeStruct((t, hv), BF16),
                   jax.ShapeDtypeStruct((t, LANES), F32), jax.ShapeDtypeStruct((ng, t), F32)],
        compiler_params=pltpu.CompilerParams(
            dimension_semantics=("arbitrary",), vmem_limit_bytes=_vmem_limit(est)),
        name="mlstm_proj",
    )(x2d, mod, mod, wq, wk, wv, wo, wg, wgt, bg, bgt)


def _mlstm_scan_kernel(q_ref, k_ref, v_ref, og_ref, gl_ref, glt_ref, hn_ref, o_ref,
                       c_ref, n_ref, m_ref, *, chunk, heads_per_step):
    seq = q_ref.shape[0]
    L = chunk
    dk, dv = MLSTM_QK_DIM, MLSTM_V_DIM
    ng = 2 * MLSTM_HEADS
    head0 = pl.program_id(1) * heads_per_step
    hi = lax.Precision.HIGHEST

    c_ref[...] = jnp.zeros(c_ref.shape, F32)
    n_ref[...] = jnp.zeros(n_ref.shape, F32)
    m_ref[...] = jnp.zeros(m_ref.shape, F32)

    row = lax.broadcasted_iota(jnp.int32, (L, L), 0)
    col = lax.broadcasted_iota(jnp.int32, (L, L), 1)
    causal = col <= row
    tri = causal.astype(F32)
    tri_t = (row <= col).astype(F32)
    lane = lax.broadcasted_iota(jnp.int32, (L, LANES), 1)
    sub = lax.broadcasted_iota(jnp.int32, (ng, L), 0)

    def chunk_body(c, carry):
        r0 = pl.multiple_of(c * L, L)
        gl = gl_ref[pl.ds(r0, L), :]
        glt = glt_ref[:, pl.ds(r0, L)]
        for hh in range(heads_per_step):
            head = head0 + hh
            li_col = jnp.sum(jnp.where(lane == head, gl, 0.0), axis=-1, keepdims=True)
            lf_col = jnp.sum(jnp.where(lane == head + MLSTM_HEADS, gl, 0.0), axis=-1, keepdims=True)
            li_row = jnp.sum(jnp.where(sub == head, glt, 0.0), axis=0, keepdims=True)
            lf_row = jnp.sum(jnp.where(sub == head + MLSTM_HEADS, glt, 0.0), axis=0, keepdims=True)
            b_col = jnp.dot(tri, jnp.broadcast_to(lf_col, (L, LANES)), precision=hi,
                            preferred_element_type=F32)[:, 0:1]
            b_row = jnp.dot(jnp.broadcast_to(lf_row, (8, L)), tri_t, precision=hi,
                            preferred_element_type=F32)[0:1, :]
            m_prev = m_ref[hh]
            d = jnp.where(causal, b_col - b_row + li_row, NEG_INF)
            m_t = jnp.maximum(b_col + m_prev, jnp.max(d, axis=-1, keepdims=True))
            w = jnp.exp(d - m_t)
            inter = jnp.exp(b_col + m_prev - m_t)
            q = q_ref[pl.ds(r0, L), hh * dk:(hh + 1) * dk]
            k = k_ref[pl.ds(r0, L), hh * dk:(hh + 1) * dk]
            v = v_ref[pl.ds(r0, L), hh * dv:(hh + 1) * dv]
            qk = lax.dot_general(q, k, (((1,), (1,)), ((), ())), preferred_element_type=F32) * w
            c_state = c_ref[hh]
            n_state = n_ref[hh]
            num = inter * jnp.dot(q, c_state.astype(BF16), preferred_element_type=F32) \
                + jnp.dot(qk.astype(BF16), v, preferred_element_type=F32)
            den = inter * jnp.sum(q.astype(F32) * n_state, axis=-1, keepdims=True) \
                + jnp.sum(qk, axis=-1, keepdims=True)
            hs = num / jnp.maximum(jnp.abs(den), jnp.exp(-m_t))
            m_new = m_t[L - 1:L, :]
            b_last = b_col[L - 1:L, :]
            decay = jnp.exp(b_last + m_prev - m_new)
            ws = jnp.exp(b_last - b_col + li_col - m_new)
            kw = k.astype(F32) * ws
            c_ref[hh] = decay * c_state + jnp.dot(kw.T.astype(BF16), v, preferred_element_type=F32)
            n_ref[hh] = decay * n_state + jnp.sum(kw, axis=0, keepdims=True)
            m_ref[hh] = m_new
            y = hs * lax.rsqrt(jnp.mean(hs * hs, axis=-1, keepdims=True) + NORM_EPS) \
                * hn_ref[:, hh * dv:(hh + 1) * dv]
            o_ref[pl.ds(r0, L), hh * dv:(hh + 1) * dv] = (
                og_ref[pl.ds(r0, L), hh * dv:(hh + 1) * dv].astype(F32) * y).astype(BF16)
        return carry

    lax.fori_loop(0, seq // L, chunk_body, 0)


def _mlstm_scan(q, k, v, og, gl, glt, head_norm, batch, seq, chunk, heads_per_step):
    t = q.shape[0]
    hps = heads_per_step
    dk, dv = MLSTM_QK_DIM, MLSTM_V_DIM
    ng = 2 * MLSTM_HEADS
    est = 2 * seq * hps * (2 * dk + 3 * dv) * 2 + 2 * seq * LANES * 4 + 2 * ng * seq * 4 \
        + hps * dk * dv * 4 + 16 * chunk * chunk * 4
    return pl.pallas_call(
        functools.partial(_mlstm_scan_kernel, chunk=chunk, heads_per_step=hps),
        grid=(batch, MLSTM_HEADS // hps),
        in_specs=[pl.BlockSpec((seq, hps * dk), lambda b, g: (b, g)),
                  pl.BlockSpec((seq, hps * dk), lambda b, g: (b, g)),
                  pl.BlockSpec((seq, hps * dv), lambda b, g: (b, g)),
                  pl.BlockSpec((seq, hps * dv), lambda b, g: (b, g)),
                  pl.BlockSpec((seq, LANES), lambda b, g: (b, 0)),
                  pl.BlockSpec((ng, seq), lambda b, g: (0, b)),
                  pl.BlockSpec((1, hps * dv), lambda b, g: (0, g))],
        out_specs=pl.BlockSpec((seq, hps * dv), lambda b, g: (b, g)),
        out_shape=jax.ShapeDtypeStruct((t, MLSTM_HEADS * dv), BF16),
        scratch_shapes=[pltpu.VMEM((hps, dk, dv), F32), pltpu.VMEM((hps, 1, dk), F32),
                        pltpu.VMEM((hps, 1, 1), F32)],
        compiler_params=pltpu.CompilerParams(
            dimension_semantics=("arbitrary", "arbitrary"), vmem_limit_bytes=_vmem_limit(est)),
        name="mlstm_scan",
    )(q, k, v, og, gl, glt, head_norm.reshape(1, -1))


def kernel(x, c, positions, mla_w_in, mla_q_norm, mla_kv_norm, mla_w_uq, mla_w_ukv, mla_w_o, mlstm_w_in, mlstm_b_gates, mlstm_head_norm, mlstm_w_out, moe_w_router, moe_b_router, moe_w_gu, moe_b_gu, moe_w_down, moe_b_down, ada_w, ada_b, ln_g, ln_b):
    batch, seq, d = x.shape
    assert d == D_MODEL and ada_w.shape[0] == DEPTH
    cfg = _tiles(batch, seq)
    t = batch * seq
    tm = cfg["tm_tok"]
    mod = _adaln(c, ada_w, ada_b, cfg["tn_ada"])
    lng = ln_g.reshape(2 * DEPTH, d)
    lnb = ln_b.reshape(2 * DEPTH, d)
    xt = x.reshape(t, d)
    for i in range(DEPTH):
        j = i // 2
        k_mix, k_ffn = 2 * i, 2 * i + 1
        if i % 2 == 0:
            qn, qr, kn, v, kr = _mla_proj(xt, mod, k_mix, positions, mla_w_in[j], mla_q_norm[j],
                                          mla_kv_norm[j], mla_w_uq[j], mla_w_ukv[j], seq, tm)
            a = _attention(qn, qr, kn, kr, v, batch, seq, cfg["tq"])
            w_o = mla_w_o[j]
        else:
            q, k, v, og, gl, glt = _mlstm_proj(xt, mod, k_mix, mlstm_w_in[j], mlstm_b_gates[j], seq, tm)
            a = _mlstm_scan(q, k, v, og, gl, glt, mlstm_head_norm[j], batch, seq,
                            cfg["chunk"], cfg["heads_per_step"])
            w_o = mlstm_w_out[j]
        x1, h2, topi, topg = _post(a, xt, mod, k_mix, k_ffn, lng[k_mix], lnb[k_mix], w_o,
                                   moe_w_router[i], moe_b_router[i], seq, tm)
        xt = _moe(h2, topi, topg, x1, mod, k_ffn, lng[k_ffn], lnb[k_ffn],
                  moe_w_gu, moe_b_gu, moe_w_down, moe_b_down, i, seq, cfg)
    return xt.reshape(batch, seq, d)
```

```python
import functools

import jax
import jax.numpy as jnp
from jax import lax
from jax.experimental import pallas as pl
from jax.experimental.pallas import tpu as pltpu

F32 = jnp.float32
BF16 = jnp.bfloat16

D_MODEL = 2048
DEPTH = 2

MLA_HEADS = 16
QK_NOPE_DIM = 128
QK_ROPE_DIM = 64
V_HEAD_DIM = 128
Q_LORA_RANK = 512
KV_LORA_RANK = 512
ROPE_THETA = 10000.0

MLSTM_HEADS = 8
MLSTM_QK_DIM = D_MODEL // (2 * MLSTM_HEADS)
MLSTM_V_DIM = D_MODEL // MLSTM_HEADS
GATE_SOFTCAP = 15.0

N_EXPERTS = 32
TOP_K = 4
D_EXPERT = D_MODEL
SWIGLU_ALPHA = 1.702
SWIGLU_LIMIT = 7.0

DEEPNORM_ALPHA = (2 * DEPTH) ** 0.25
NORM_EPS = 1e-6

LANES = 128
V7X_VMEM_BYTES = 64 * 1024 * 1024
NEG_INF = float("-inf")


def _vmem_limit(estimate_bytes):
    return int(min(estimate_bytes + (12 << 20), V7X_VMEM_BYTES - (8 << 20)))


def _tiles(batch, seq):
    t = batch * seq
    cfg = dict(
        tm_tok=min(256, seq),
        tq=min(256, seq),
        chunk=min(128, seq),
        heads_per_step=2,
        tm_moe=min(512, t * TOP_K // 8),
        tf=512,
        tn=1024,
        tc=min(128, seq),
        tn_ada=768,
    )
    return cfg


def _adaln_kernel(c_ref, w_ref, b_ref, o_ref):
    c = c_ref[...]
    cond = c * jax.nn.sigmoid(c)
    o_ref[...] = jnp.dot(cond.astype(BF16), w_ref[...].astype(BF16),
                         preferred_element_type=F32) + b_ref[...]


def _adaln(c, ada_w, ada_b, tn):
    b, d = c.shape
    n_mod = ada_w.shape[0] * ada_w.shape[1]
    w = ada_w.reshape(n_mod, d, 3 * d)
    bias = ada_b.reshape(n_mod, 1, 3 * d)
    out = pl.pallas_call(
        _adaln_kernel,
        grid=(n_mod, 3 * d // tn),
        in_specs=[
            pl.BlockSpec((b, d), lambda i, j: (0, 0)),
            pl.BlockSpec((None, d, tn), lambda i, j: (i, 0, j)),
            pl.BlockSpec((None, 1, tn), lambda i, j: (i, 0, j)),
        ],
        out_specs=pl.BlockSpec((None, b, tn), lambda i, j: (i, 0, j)),
        out_shape=jax.ShapeDtypeStruct((n_mod, b, 3 * d), F32),
        compiler_params=pltpu.CompilerParams(
            dimension_semantics=("arbitrary", "arbitrary"),
            vmem_limit_bytes=_vmem_limit(2 * d * tn * 4 + d * tn * 2)),
        name="adaln",
    )(c, w, bias)
    return out.reshape(n_mod, b, 1, 3 * d)


def _mod_spec(k, part, tiles_per_batch):
    return pl.BlockSpec((None, None, 1, D_MODEL),
                        lambda i, *_: (k, i // tiles_per_batch, 0, part))


def _row_spec(k):
    return pl.BlockSpec((None, 1, D_MODEL), lambda i, *_: (k, 0, 0))


def _resident(shape):
    nd = len(shape)
    return pl.BlockSpec(shape, lambda *_: (0,) * nd, pipeline_mode=pl.Buffered(1))


def _mla_proj_kernel(x_ref, shift_ref, scale_ref, pos_ref, rope_ref, win_ref, qnorm_ref, kvnorm_ref,
                     wqn_ref, wqr_ref, wqs_ref, wkn_ref, wv_ref,
                     qn_ref, qr_ref, kn_ref, v_ref, kr_ref):
    h = x_ref[...] * (1.0 + scale_ref[...]) + shift_ref[...]
    lat = jnp.dot(h.astype(BF16), win_ref[...], preferred_element_type=F32)
    cq = lat[:, :Q_LORA_RANK]
    ckv = lat[:, Q_LORA_RANK:Q_LORA_RANK + KV_LORA_RANK]
    kr = lat[:, Q_LORA_RANK + KV_LORA_RANK:Q_LORA_RANK + KV_LORA_RANK + LANES]
    kr_sw = lat[:, Q_LORA_RANK + KV_LORA_RANK + LANES:]
    cq = cq * lax.rsqrt(jnp.mean(cq * cq, axis=-1, keepdims=True) + NORM_EPS) * qnorm_ref[...]
    ckv = ckv * lax.rsqrt(jnp.mean(ckv * ckv, axis=-1, keepdims=True) + NORM_EPS) * kvnorm_ref[...]
    cq = cq.astype(BF16)
    ckv = ckv.astype(BF16)

    ang = pos_ref[...].astype(F32) * rope_ref[0:1, :]
    cos_t = jnp.cos(ang) * rope_ref[1:2, :]
    sin_t = jnp.sin(ang) * rope_ref[2:3, :]

    qn_ref[...] = jnp.dot(cq, wqn_ref[...], preferred_element_type=F32).astype(BF16)
    q_rope = jnp.dot(cq, wqr_ref[...], preferred_element_type=F32)
    q_swap = jnp.dot(cq, wqs_ref[...], preferred_element_type=F32)
    cos_h = jnp.tile(cos_t, (1, MLA_HEADS))
    sin_h = jnp.tile(sin_t, (1, MLA_HEADS))
    qr_ref[...] = (q_rope * cos_h + q_swap * sin_h).astype(BF16)
    kn_ref[...] = jnp.dot(ckv, wkn_ref[...], preferred_element_type=F32).astype(BF16)
    v_ref[...] = jnp.dot(ckv, wv_ref[...], preferred_element_type=F32).astype(BF16)
    kr_ref[...] = (kr * cos_t + kr_sw * sin_t).astype(BF16)


def _rope_table():
    half = QK_ROPE_DIM // 2
    inv_freq = ROPE_THETA ** (-jnp.arange(0, QK_ROPE_DIM, 2, dtype=F32) / QK_ROPE_DIM)
    zeros = jnp.zeros((LANES - QK_ROPE_DIM,), F32)
    freq_row = jnp.concatenate([inv_freq, inv_freq, zeros])
    cos_mask = jnp.concatenate([jnp.ones((QK_ROPE_DIM,), F32), zeros])
    sin_sign = jnp.concatenate([-jnp.ones((half,), F32), jnp.ones((half,), F32), zeros])
    pad = jnp.zeros((5, LANES), F32)
    return jnp.concatenate([jnp.stack([freq_row, cos_mask, sin_sign]), pad], axis=0)


def _mla_weights(w_in, w_uq, w_ukv):
    d = w_in.shape[0]
    h = MLA_HEADS
    half = QK_ROPE_DIM // 2
    lat = Q_LORA_RANK + KV_LORA_RANK
    kr = w_in[:, lat:]
    zpad = jnp.zeros((d, LANES - QK_ROPE_DIM), w_in.dtype)
    win = jnp.concatenate([w_in[:, :lat], kr, zpad, kr[:, half:], kr[:, :half], zpad], axis=1)
    wq = w_uq.reshape(Q_LORA_RANK, h, QK_NOPE_DIM + QK_ROPE_DIM)
    wqn = wq[:, :, :QK_NOPE_DIM].reshape(Q_LORA_RANK, h * QK_NOPE_DIM)
    rope = wq[:, :, QK_NOPE_DIM:]
    zq = jnp.zeros((Q_LORA_RANK, h, LANES - QK_ROPE_DIM), w_uq.dtype)
    wqr = jnp.concatenate([rope, zq], axis=-1).reshape(Q_LORA_RANK, h * LANES)
    wqs = jnp.concatenate([rope[..., half:], rope[..., :half], zq], axis=-1).reshape(Q_LORA_RANK, h * LANES)
    wkv = w_ukv.reshape(KV_LORA_RANK, h, QK_NOPE_DIM + V_HEAD_DIM)
    wkn = wkv[:, :, :QK_NOPE_DIM].reshape(KV_LORA_RANK, h * QK_NOPE_DIM)
    wv = wkv[:, :, QK_NOPE_DIM:].reshape(KV_LORA_RANK, h * V_HEAD_DIM)
    return tuple(a.astype(BF16) for a in (win, wqn, wqr, wqs, wkn, wv))


def _mla_proj(x2d, mod, k_mod, positions, w_in, q_norm, kv_norm, w_uq, w_ukv, seq, tm):
    t, d = x2d.shape
    tiles_per_batch = seq // tm
    win, wqn, wqr, wqs, wkn, wv = _mla_weights(w_in, w_uq, w_ukv)
    hn = MLA_HEADS * LANES
    tok = lambda n: pl.BlockSpec((tm, n), lambda i: (i, 0))
    weights_bytes = 2 * (win.size + wqn.size + wqr.size + wqs.size + wkn.size + wv.size)
    est = weights_bytes + 2 * tm * d * 4 + 2 * (4 * tm * hn * 2 + tm * LANES * 2) + 8 * tm * hn * 4
    outs = pl.pallas_call(
        _mla_proj_kernel,
        grid=(t // tm,),
        in_specs=[
            tok(d),
            _mod_spec(k_mod, 0, tiles_per_batch),
            _mod_spec(k_mod, 1, tiles_per_batch),
            pl.BlockSpec((tm, 1), lambda i: (i, 0)),
            _resident((8, LANES)),
            _resident(win.shape),
            _resident((1, Q_LORA_RANK)),
            _resident((1, KV_LORA_RANK)),
            _resident(wqn.shape), _resident(wqr.shape), _resident(wqs.shape),
            _resident(wkn.shape), _resident(wv.shape),
        ],
        out_specs=[tok(hn), tok(hn), tok(hn), tok(hn), tok(LANES)],
        out_shape=[jax.ShapeDtypeStruct((t, hn), BF16)] * 4 + [jax.ShapeDtypeStruct((t, LANES), BF16)],
        compiler_params=pltpu.CompilerParams(
            dimension_semantics=("arbitrary",), vmem_limit_bytes=_vmem_limit(est)),
        name="mla_proj",
    )(x2d, mod, mod, positions.reshape(t, 1), _rope_table(), win,
      q_norm.reshape(1, -1), kv_norm.reshape(1, -1), wqn, wqr, wqs, wkn, wv)
    return outs


def _attn_kernel(qn_ref, qr_ref, kn_ref, kr_ref, v_ref, o_ref, s_ref, *, tq, scale):
    seq = qn_ref.shape[0]
    row = lax.broadcasted_iota(jnp.int32, (tq, tq), 0)
    col = lax.broadcasted_iota(jnp.int32, (tq, tq), 1)
    causal = col <= row
    exp2_scale = scale * 1.4426950408889634

    def fold_lanes(a):
        return [a[:, g * LANES:(g + 1) * LANES] for g in range(tq // LANES)]

    for qi in range(seq // tq):
        rows = slice(qi * tq, (qi + 1) * tq)
        q = jnp.concatenate([qn_ref[rows, :], qr_ref[rows, :]], axis=-1)
        m_acc = jnp.full((tq, LANES), NEG_INF, F32)
        for j in range(qi + 1):
            cols = slice(j * tq, (j + 1) * tq)
            k = jnp.concatenate([kn_ref[cols, :], kr_ref[cols, :]], axis=-1)
            s = lax.dot_general(q, k, (((1,), (1,)), ((), ())), preferred_element_type=F32)
            if j == qi:
                s = jnp.where(causal, s, NEG_INF)
            s_ref[:, cols] = s
            for part in fold_lanes(s):
                m_acc = jnp.maximum(m_acc, part)
        m = jnp.max(m_acc, axis=-1, keepdims=True)
        l_acc = jnp.zeros((tq, LANES), F32)
        acc = jnp.zeros((tq, V_HEAD_DIM), F32)
        for j in range(qi + 1):
            cols = slice(j * tq, (j + 1) * tq)
            p = jnp.exp2((s_ref[:, cols] - m) * exp2_scale)
            for part in fold_lanes(p):
                l_acc = l_acc + part
            acc = acc + jnp.dot(p.astype(BF16), v_ref[cols, :], preferred_element_type=F32)
        l = jnp.sum(l_acc, axis=-1, keepdims=True)
        o_ref[rows, :] = (acc / l).astype(BF16)


def _attention(qn, qr, kn, kr, v, batch, seq, tq):
    t = qn.shape[0]
    head = lambda: pl.BlockSpec((seq, LANES), lambda b, h: (b, h))
    scale = (QK_NOPE_DIM + QK_ROPE_DIM) ** -0.5
    est = 2 * 6 * seq * LANES * 2 + tq * seq * 4 + 6 * tq * tq * 4
    return pl.pallas_call(
        functools.partial(_attn_kernel, tq=tq, scale=scale),
        grid=(batch, MLA_HEADS),
        in_specs=[head(), head(), head(), pl.BlockSpec((seq, LANES), lambda b, h: (b, 0)), head()],
        out_specs=head(),
        out_shape=jax.ShapeDtypeStruct((t, MLA_HEADS * V_HEAD_DIM), BF16),
        scratch_shapes=[pltpu.VMEM((tq, seq), F32)],
        compiler_params=pltpu.CompilerParams(
            dimension_semantics=("arbitrary", "arbitrary"), vmem_limit_bytes=_vmem_limit(est)),
        name="mla_attention",
    )(qn, qr, kn, kr, v)


SLABS = D_MODEL // LANES


def _store_slabs(ref, val, first_slab=0):
    rows = val.shape[0]
    for s in range(val.shape[1] // LANES):
        ref[pl.ds(first_slab + s, rows, stride=SLABS), :] = val[:, s * LANES:(s + 1) * LANES]


def _layer_norm(z, g, b):
    mu = jnp.mean(z, axis=-1, keepdims=True)
    zc = z - mu
    var = jnp.mean(zc * zc, axis=-1, keepdims=True)
    return zc * lax.rsqrt(var + NORM_EPS) * g + b


def _post_kernel(a_ref, x_ref, gate_ref, lng_ref, lnb_ref, shift_ref, scale_ref, wo_ref, wr_ref, br_ref,
                 x1_ref, h2_ref, topi_ref, topg_ref):
    y = jnp.dot(a_ref[...], wo_ref[...], preferred_element_type=F32)
    z = DEEPNORM_ALPHA * x_ref[...] + gate_ref[...] * y
    x1 = _layer_norm(z, lng_ref[...], lnb_ref[...])
    x1_ref[...] = x1
    h2 = x1 * (1.0 + scale_ref[...]) + shift_ref[...]
    _store_slabs(h2_ref, h2)
    h_hi = h2.astype(BF16)
    h_lo = (h2 - h_hi.astype(F32)).astype(BF16)
    t_hi = jnp.dot(h_hi, wr_ref[...], preferred_element_type=F32)
    t_lo = jnp.dot(h_lo, wr_ref[:, :LANES], preferred_element_type=F32)
    logits = t_hi[:, :LANES] + (t_hi[:, LANES:] + t_lo) + br_ref[...]
    tm = logits.shape[0]
    lane = lax.broadcasted_iota(jnp.int32, (tm, LANES), 1)
    vals, idxs = [], []
    for _ in range(TOP_K):
        mk = jnp.max(logits, axis=-1, keepdims=True)
        ik = jnp.min(jnp.where(logits == mk, lane, LANES), axis=-1, keepdims=True)
        vals.append(mk)
        idxs.append(ik)
        logits = jnp.where(lane == ik, NEG_INF, logits)
    exps = [jnp.exp(v - vals[0]) for v in vals]
    den = exps[0]
    for e in exps[1:]:
        den = den + e
    topi = jnp.zeros((tm, LANES), jnp.int32)
    topg = jnp.zeros((tm, LANES), F32)
    for k in range(TOP_K):
        topi = jnp.where(lane == k, idxs[k], topi)
        topg = jnp.where(lane == k, exps[k] / den, topg)
    topi_ref[...] = topi
    topg_ref[...] = topg


def _post(a, x2d, mod, k_mix, k_ffn, ln_g, ln_b, w_o, w_router, b_router, seq, tm):
    t, d = x2d.shape
    tiles_per_batch = seq // tm
    wr = jnp.concatenate([w_router, jnp.zeros((d, LANES - N_EXPERTS), F32)], axis=1)
    wr_hi = wr.astype(BF16)
    wr_lo = (wr - wr_hi.astype(F32)).astype(BF16)
    wr = jnp.concatenate([wr_hi, wr_lo], axis=1)
    br = jnp.concatenate([b_router, jnp.full((LANES - N_EXPERTS,), NEG_INF, F32)]).reshape(1, LANES)
    tok = lambda n: pl.BlockSpec((tm, n), lambda i: (i, 0))
    est = d * d * 2 + d * LANES * 4 + 2 * tm * d * (2 + 4 + 4 + 4) + 6 * tm * d * 4
    return pl.pallas_call(
        _post_kernel,
        grid=(t // tm,),
        in_specs=[
            tok(d), tok(d),
            _mod_spec(k_mix, 2, tiles_per_batch),
            _resident((1, d)), _resident((1, d)),
            _mod_spec(k_ffn, 0, tiles_per_batch),
            _mod_spec(k_ffn, 1, tiles_per_batch),
            _resident((d, d)), _resident((d, 2 * LANES)), _resident((1, LANES)),
        ],
        out_specs=[tok(d), pl.BlockSpec((tm * SLABS, LANES), lambda i: (i, 0)), tok(LANES), tok(LANES)],
        out_shape=[jax.ShapeDtypeStruct((t, d), F32), jax.ShapeDtypeStruct((t * SLABS, LANES), F32),
                   jax.ShapeDtypeStruct((t, LANES), jnp.int32), jax.ShapeDtypeStruct((t, LANES), F32)],
        compiler_params=pltpu.CompilerParams(
            dimension_semantics=("arbitrary",), vmem_limit_bytes=_vmem_limit(est)),
        name="mixer_out_ln_router",
    )(a, x2d, mod, ln_g.reshape(1, d), ln_b.reshape(1, d), mod, mod, w_o.astype(BF16), wr, br)


def _route(topi, tm_moe):
    t = topi.shape[0]
    m = t * TOP_K
    e_flat = topi[:, :TOP_K].reshape(m)
    onehot = (e_flat[:, None] == jnp.arange(N_EXPERTS, dtype=jnp.int32)[None, :]).astype(jnp.int32)
    csum = jnp.cumsum(onehot, axis=0)
    counts = csum[-1]
    rank = jnp.take_along_axis(csum, e_flat[:, None], axis=1)[:, 0] - 1
    padded = ((counts + tm_moe - 1) // tm_moe) * tm_moe
    pends = jnp.cumsum(padded)
    pstarts = pends - padded
    dest = (pstarts[e_flat] + rank).astype(jnp.int32)
    n_tiles = m // tm_moe + N_EXPERTS
    row_tok = jnp.zeros((n_tiles * tm_moe,), jnp.int32).at[dest].set(
        jnp.arange(m, dtype=jnp.int32) // TOP_K)
    tile_start = jnp.arange(n_tiles, dtype=jnp.int32) * tm_moe
    tile_e = jnp.minimum(jnp.searchsorted(pends, tile_start, side="right"), N_EXPERTS - 1).astype(jnp.int32)
    n_used = (pends[-1] // tm_moe).astype(jnp.int32).reshape(1)
    return dest, row_tok, tile_e, n_used


def _gather_issue(idx_ref, src_hbm, buf, sem, slot, n_rows):
    def body(r, carry):
        pltpu.make_async_copy(src_hbm.at[idx_ref[0, 0, r]], buf.at[slot, :, r, :], sem.at[slot]).start()
        return carry
    lax.fori_loop(0, n_rows, body, 0, unroll=8)


def _gather_wait(buf, sem, slot):
    pltpu.make_async_copy(buf.at[slot], buf.at[slot], sem.at[slot]).wait()


def _dispatch_kernel(ns_ref, idx_ref, nxt_ref, src_hbm, o_ref, buf, sem, *, n_rows):
    i = pl.program_id(0)
    n_active = ns_ref[0]
    slot = lax.rem(i, 2)

    @pl.when(jnp.logical_and(i == 0, n_active > 0))
    def _():
        _gather_issue(idx_ref, src_hbm, buf, sem, 0, n_rows)

    @pl.when(i + 1 < n_active)
    def _():
        _gather_issue(nxt_ref, src_hbm, buf, sem, 1 - slot, n_rows)

    @pl.when(i < n_active)
    def _():
        _gather_wait(buf, sem, slot)
        o_ref[...] = buf[slot].astype(BF16)

    @pl.when(i >= n_active)
    def _():
        o_ref[...] = jnp.zeros(o_ref.shape, o_ref.dtype)


def _dispatch(h2_slabs, row_tok, n_used, r):
    t, slabs, _ = h2_slabs.shape
    n = row_tok.shape[0] // r
    idx = row_tok.reshape(n, 1, r)
    smem = lambda f: pl.BlockSpec((1, 1, r), f, memory_space=pltpu.SMEM)
    return pl.pallas_call(
        functools.partial(_dispatch_kernel, n_rows=r),
        grid_spec=pltpu.PrefetchScalarGridSpec(
            num_scalar_prefetch=1,
            grid=(n,),
            in_specs=[smem(lambda i, ns: (i, 0, 0)),
                      smem(lambda i, ns: (jnp.minimum(i + 1, n - 1), 0, 0)),
                      pl.BlockSpec(memory_space=pl.ANY)],
            out_specs=pl.BlockSpec((None, slabs, r, LANES), lambda i, ns: (i, 0, 0, 0)),
            scratch_shapes=[pltpu.VMEM((2, slabs, r, LANES), F32), pltpu.SemaphoreType.DMA((2,))]),
        out_shape=jax.ShapeDtypeStruct((n, slabs, r, LANES), BF16),
        compiler_params=pltpu.CompilerParams(
            dimension_semantics=("arbitrary",),
            vmem_limit_bytes=_vmem_limit(2 * r * slabs * LANES * (4 + 2))),
        name="moe_dispatch_gather",
    )(n_used, idx, idx, h2_slabs)


def _expert_changed(te_ref, i):
    prev = te_ref[jnp.maximum(i - 1, 0)]
    return jnp.logical_or(i == 0, te_ref[i] != prev)


def _gemm1_kernel(te_ref, nu_ref, x_ref, wg_ref, wu_ref, bg_ref, bu_ref, o_ref, wg_bf, wu_bf):
    i = pl.program_id(1)

    @pl.when(_expert_changed(te_ref, i))
    def _():
        wg_bf[...] = wg_ref[...].astype(BF16)
        wu_bf[...] = wu_ref[...].astype(BF16)

    @pl.when(i < nu_ref[0])
    def _():
        x = jnp.concatenate([x_ref[s] for s in range(x_ref.shape[0])], axis=-1)
        g = jnp.dot(x, wg_bf[...], preferred_element_type=F32) + bg_ref[...]
        u = jnp.dot(x, wu_bf[...], preferred_element_type=F32) + bu_ref[...]
        g = jnp.minimum(g, SWIGLU_LIMIT)
        u = jnp.clip(u, -SWIGLU_LIMIT, SWIGLU_LIMIT)
        o_ref[...] = (g * jax.nn.sigmoid(SWIGLU_ALPHA * g) * (u + 1.0)).astype(BF16)

    @pl.when(i >= nu_ref[0])
    def _():
        o_ref[...] = jnp.zeros(o_ref.shape, o_ref.dtype)


def _gemm1(x_rows, tile_e, n_used, w_gu, b_gu, layer, tf):
    n_tiles, slabs, tm, _ = x_rows.shape
    d = slabs * LANES
    n_rows = n_tiles * tm
    f = D_EXPERT
    nf = f // tf
    bias = b_gu.reshape(b_gu.shape[0], N_EXPERTS, 1, 2 * f)
    wspec = lambda off: pl.BlockSpec((None, None, d, tf), lambda j, i, te, nu: (layer, te[i], 0, j + off))
    bspec = lambda off: pl.BlockSpec((None, None, 1, tf), lambda j, i, te, nu: (layer, te[i], 0, j + off))
    est = 2 * 2 * d * tf * 4 + 2 * d * tf * 2 + 2 * tm * d * 2 + 2 * tm * tf * 2 + 4 * tm * tf * 4
    return pl.pallas_call(
        _gemm1_kernel,
        grid_spec=pltpu.PrefetchScalarGridSpec(
            num_scalar_prefetch=2,
            grid=(nf, n_tiles),
            in_specs=[pl.BlockSpec((None, slabs, tm, LANES), lambda j, i, te, nu: (i, 0, 0, 0)),
                      wspec(0), wspec(nf), bspec(0), bspec(nf)],
            out_specs=pl.BlockSpec((tm, tf), lambda j, i, te, nu: (i, j)),
            scratch_shapes=[pltpu.VMEM((d, tf), BF16), pltpu.VMEM((d, tf), BF16)]),
        out_shape=jax.ShapeDtypeStruct((n_rows, f), BF16),
        compiler_params=pltpu.CompilerParams(
            dimension_semantics=("arbitrary", "arbitrary"), vmem_limit_bytes=_vmem_limit(est)),
        name="moe_gate_up_swiglu",
    )(tile_e, n_used, x_rows, w_gu, w_gu, bias, bias)


def _gemm2_kernel(te_ref, nu_ref, a_ref, w_ref, b_ref, o_ref, w_bf, *, tn):
    i = pl.program_id(0)

    @pl.when(_expert_changed(te_ref, i))
    def _():
        w_bf[...] = w_ref[...].astype(BF16)

    @pl.when(i < nu_ref[0])
    def _():
        a = a_ref[...]
        for c in range(w_bf.shape[1] // tn):
            cols = slice(c * tn, (c + 1) * tn)
            res = jnp.dot(a, w_bf[:, cols], preferred_element_type=F32) + b_ref[:, cols]
            _store_slabs(o_ref, res, first_slab=c * tn // LANES)

    @pl.when(i >= nu_ref[0])
    def _():
        o_ref[...] = jnp.zeros(o_ref.shape, o_ref.dtype)


def _gemm2(act, tile_e, n_used, w_down, b_down, layer, tm, tn):
    n_rows, f = act.shape
    d = D_MODEL
    n_tiles = n_rows // tm
    bias = b_down.reshape(b_down.shape[0], N_EXPERTS, 1, d)
    est = f * d * 4 + f * d * 2 + 2 * tm * f * 2 + 2 * tm * d * 4 + 2 * tm * tn * 4
    return pl.pallas_call(
        functools.partial(_gemm2_kernel, tn=tn),
        grid_spec=pltpu.PrefetchScalarGridSpec(
            num_scalar_prefetch=2,
            grid=(n_tiles,),
            in_specs=[pl.BlockSpec((tm, f), lambda i, te, nu: (i, 0)),
                      pl.BlockSpec((None, None, f, d), lambda i, te, nu: (layer, te[i], 0, 0),
                                   pipeline_mode=pl.Buffered(1)),
                      pl.BlockSpec((None, None, 1, d), lambda i, te, nu: (layer, te[i], 0, 0))],
            out_specs=pl.BlockSpec((tm * SLABS, LANES), lambda i, te, nu: (i, 0)),
            scratch_shapes=[pltpu.VMEM((f, d), BF16)]),
        out_shape=jax.ShapeDtypeStruct((n_rows * SLABS, LANES), F32),
        compiler_params=pltpu.CompilerParams(
            dimension_semantics=("arbitrary",), vmem_limit_bytes=_vmem_limit(est)),
        name="moe_down",
    )(tile_e, n_used, act, w_down, bias)


def _combine_kernel(idx_ref, nxt_ref, rows_hbm, g_ref, x_ref, gate_ref, lng_ref, lnb_ref, o_ref,
                    buf, sem, *, tc):
    i = pl.program_id(0)
    n = pl.num_programs(0)
    slot = lax.rem(i, 2)
    n_rows = TOP_K * tc

    @pl.when(i == 0)
    def _():
        _gather_issue(idx_ref, rows_hbm, buf, sem, 0, n_rows)

    @pl.when(i + 1 < n)
    def _():
        _gather_issue(nxt_ref, rows_hbm, buf, sem, 1 - slot, n_rows)

    _gather_wait(buf, sem, slot)
    g = g_ref[...]
    gk = [jnp.broadcast_to(g[:, k:k + 1], (tc, LANES)) for k in range(TOP_K)]
    parts = []
    for s in range(buf.shape[1]):
        acc = gk[0] * buf[slot, s, pl.ds(0, tc), :]
        for k in range(1, TOP_K):
            acc = acc + gk[k] * buf[slot, s, pl.ds(k * tc, tc), :]
        parts.append(acc)
    y = jnp.concatenate(parts, axis=-1)
    z = DEEPNORM_ALPHA * x_ref[...] + gate_ref[...] * y
    o_ref[...] = _layer_norm(z, lng_ref[...], lnb_ref[...])


def _combine(out_rows, dest, topg, x1, mod, k_ffn, ln_g, ln_b, seq, tc):
    t, d = x1.shape
    n = t // tc
    tiles_per_batch = seq // tc
    idx = dest.reshape(n, tc, TOP_K).transpose(0, 2, 1).reshape(n, 1, TOP_K * tc)
    smem = lambda f: pl.BlockSpec((1, 1, TOP_K * tc), f, memory_space=pltpu.SMEM)
    tok = lambda w: pl.BlockSpec((tc, w), lambda i: (i, 0))
    return pl.pallas_call(
        functools.partial(_combine_kernel, tc=tc),
        grid=(n,),
        in_specs=[smem(lambda i: (i, 0, 0)),
                  smem(lambda i: (jnp.minimum(i + 1, n - 1), 0, 0)),
                  pl.BlockSpec(memory_space=pl.ANY),
                  tok(LANES), tok(d),
                  _mod_spec(k_ffn, 2, tiles_per_batch),
                  _resident((1, d)), _resident((1, d))],
        out_specs=tok(d),
        out_shape=jax.ShapeDtypeStruct((t, d), F32),
        scratch_shapes=[pltpu.VMEM((2, d // LANES, TOP_K * tc, LANES), F32), pltpu.SemaphoreType.DMA((2,))],
        compiler_params=pltpu.CompilerParams(
            dimension_semantics=("arbitrary",),
            vmem_limit_bytes=_vmem_limit(2 * TOP_K * tc * d * 4 + 6 * tc * d * 4)),
        name="moe_combine_ln",
    )(idx, idx, out_rows, topg, x1, mod, ln_g.reshape(1, d), ln_b.reshape(1, d))


def _moe(h2, topi, topg, x1, mod, k_ffn, ln_g, ln_b, w_gu, b_gu, w_down, b_down, layer, seq, cfg):
    dest, row_tok, tile_e, n_used = _route(topi, cfg["tm_moe"])
    x_rows = _dispatch(h2.reshape(-1, SLABS, LANES), row_tok, n_used, cfg["tm_moe"])
    act = _gemm1(x_rows, tile_e, n_used, w_gu, b_gu, layer, cfg["tf"])
    out_rows = _gemm2(act, tile_e, n_used, w_down, b_down, layer, cfg["tm_moe"], cfg["tn"])
    out_rows = out_rows.reshape(-1, SLABS, LANES)
    return _combine(out_rows, dest, topg, x1, mod, k_ffn, ln_g, ln_b, seq, cfg["tc"])


def _softcap(g):
    return GATE_SOFTCAP * jnp.tanh(g / GATE_SOFTCAP)


def _log_sigmoid(x):
    return jnp.minimum(x, 0.0) - jnp.log(1.0 + jnp.exp(-jnp.abs(x)))


def _mlstm_proj_kernel(x_ref, shift_ref, scale_ref, wq_ref, wk_ref, wv_ref, wo_ref, wg_ref, wgt_ref,
                       bg_ref, bgt_ref, q_ref, k_ref, v_ref, og_ref, gl_ref, glt_ref):
    h = (x_ref[...] * (1.0 + scale_ref[...]) + shift_ref[...]).astype(BF16)
    q_ref[...] = (jnp.dot(h, wq_ref[...], preferred_element_type=F32) * MLSTM_QK_DIM ** -0.5).astype(BF16)
    k_ref[...] = jnp.dot(h, wk_ref[...], preferred_element_type=F32).astype(BF16)
    v_ref[...] = jnp.dot(h, wv_ref[...], preferred_element_type=F32).astype(BF16)
    og_ref[...] = jax.nn.sigmoid(jnp.dot(h, wo_ref[...], preferred_element_type=F32)).astype(BF16)
    g = _softcap(jnp.dot(h, wg_ref[...], preferred_element_type=F32) + bg_ref[...])
    lane = lax.broadcasted_iota(jnp.int32, g.shape, 1)
    gl_ref[...] = jnp.where(lane < MLSTM_HEADS, g,
                            jnp.where(lane < 2 * MLSTM_HEADS, _log_sigmoid(g), 0.0))
    gt = _softcap(lax.dot_general(wgt_ref[...], h, (((1,), (1,)), ((), ())),
                                  preferred_element_type=F32) + bgt_ref[...])
    sub = lax.broadcasted_iota(jnp.int32, gt.shape, 0)
    glt_ref[...] = jnp.where(sub < MLSTM_HEADS, gt, _log_sigmoid(gt))


def _mlstm_proj(x2d, mod, k_mod, w_in, b_gates, seq, tm):
    t, d = x2d.shape
    tiles_per_batch = seq // tm
    hk = MLSTM_HEADS * MLSTM_QK_DIM
    hv = MLSTM_HEADS * MLSTM_V_DIM
    ng = 2 * MLSTM_HEADS
    o1, o2, o3, o4 = hk, 2 * hk, 2 * hk + hv, 2 * hk + 2 * hv
    wq, wk, wv, wo = (w_in[:, a:b].astype(BF16) for a, b in ((0, o1), (o1, o2), (o2, o3), (o3, o4)))
    wg = jnp.concatenate([w_in[:, o4:], jnp.zeros((d, LANES - ng), F32)], axis=1).astype(BF16)
    wgt = w_in[:, o4:].T.astype(BF16)
    bg = jnp.concatenate([b_gates, jnp.zeros((LANES - ng,), F32)]).reshape(1, LANES)
    bgt = b_gates.reshape(ng, 1)
    tok = lambda n: pl.BlockSpec((tm, n), lambda i: (i, 0))
    est = 2 * d * (2 * hk + 2 * hv + LANES + ng) + 2 * tm * d * 4 + 2 * tm * (2 * hk + 2 * hv) * 2 \
        + 6 * tm * hv * 4
    return pl.pallas_call(
        _mlstm_proj_kernel,
        grid=(t // tm,),
        in_specs=[tok(d), _mod_spec(k_mod, 0, tiles_per_batch), _mod_spec(k_mod, 1, tiles_per_batch),
                  _resident(wq.shape), _resident(wk.shape), _resident(wv.shape), _resident(wo.shape),
                  _resident(wg.shape), _resident(wgt.shape), _resident((1, LANES)), _resident((ng, 1))],
        out_specs=[tok(hk), tok(hk), tok(hv), tok(hv), tok(LANES),
                   pl.BlockSpec((ng, tm), lambda i: (0, i))],
        out_shape=[jax.ShapeDtypeStruct((t, hk), BF16), jax.ShapeDtypeStruct((t, hk), BF16),
                   jax.ShapeDtypeStruct((t, hv), BF16), jax.ShapeDtypeStruct((t, hv), BF16),
                   jax.ShapeDtypeStruct((t, LANES), F32), jax.ShapeDtypeStruct((ng, t), F32)],
        compiler_params=pltpu.CompilerParams(
            dimension_semantics=("arbitrary",), vmem_limit_bytes=_vmem_limit(est)),
        name="mlstm_proj",
    )(x2d, mod, mod, wq, wk, wv, wo, wg, wgt, bg, bgt)


def _mlstm_scan_kernel(q_ref, k_ref, v_ref, og_ref, gl_ref, glt_ref, hn_ref, o_ref,
                       c_ref, n_ref, m_ref, *, chunk, heads_per_step):
    seq = q_ref.shape[0]
    L = chunk
    dk, dv = MLSTM_QK_DIM, MLSTM_V_DIM
    ng = 2 * MLSTM_HEADS
    head0 = pl.program_id(1) * heads_per_step
    hi = lax.Precision.HIGHEST

    c_ref[...] = jnp.zeros(c_ref.shape, F32)
    n_ref[...] = jnp.zeros(n_ref.shape, F32)
    m_ref[...] = jnp.zeros(m_ref.shape, F32)

    row = lax.broadcasted_iota(jnp.int32, (L, L), 0)
    col = lax.broadcasted_iota(jnp.int32, (L, L), 1)
    causal = col <= row
    tri = causal.astype(F32)
    tri_t = (row <= col).astype(F32)
    lane = lax.broadcasted_iota(jnp.int32, (L, LANES), 1)
    sub = lax.broadcasted_iota(jnp.int32, (ng, L), 0)

    def chunk_body(c, carry):
        r0 = pl.multiple_of(c * L, L)
        gl = gl_ref[pl.ds(r0, L), :]
        glt = glt_ref[:, pl.ds(r0, L)]
        for hh in range(heads_per_step):
            head = head0 + hh
            li_col = jnp.sum(jnp.where(lane == head, gl, 0.0), axis=-1, keepdims=True)
            lf_col = jnp.sum(jnp.where(lane == head + MLSTM_HEADS, gl, 0.0), axis=-1, keepdims=True)
            li_row = jnp.sum(jnp.where(sub == head, glt, 0.0), axis=0, keepdims=True)
            lf_row = jnp.sum(jnp.where(sub == head + MLSTM_HEADS, glt, 0.0), axis=0, keepdims=True)
            b_col = jnp.dot(tri, jnp.broadcast_to(lf_col, (L, LANES)), precision=hi,
                            preferred_element_type=F32)[:, 0:1]
            b_row = jnp.dot(jnp.broadcast_to(lf_row, (8, L)), tri_t, precision=hi,
                            preferred_element_type=F32)[0:1, :]
            m_prev = m_ref[hh]
            d = jnp.where(causal, b_col - b_row + li_row, NEG_INF)
            m_t = jnp.maximum(b_col + m_prev, jnp.max(d, axis=-1, keepdims=True))
            w = jnp.exp(d - m_t)
            inter = jnp.exp(b_col + m_prev - m_t)
            q = q_ref[pl.ds(r0, L), hh * dk:(hh + 1) * dk]
            k = k_ref[pl.ds(r0, L), hh * dk:(hh + 1) * dk]
            v = v_ref[pl.ds(r0, L), hh * dv:(hh + 1) * dv]
            qk = lax.dot_general(q, k, (((1,), (1,)), ((), ())), preferred_element_type=F32) * w
            c_state = c_ref[hh]
            n_state = n_ref[hh]
            num = inter * jnp.dot(q, c_state.astype(BF16), preferred_element_type=F32) \
                + jnp.dot(qk.astype(BF16), v, preferred_element_type=F32)
            den = inter * jnp.sum(q.astype(F32) * n_state, axis=-1, keepdims=True) \
                + jnp.sum(qk, axis=-1, keepdims=True)
            hs = num / jnp.maximum(jnp.abs(den), jnp.exp(-m_t))
            m_new = m_t[L - 1:L, :]
            b_last = b_col[L - 1:L, :]
            decay = jnp.exp(b_last + m_prev - m_new)
            ws = jnp.exp(b_last - b_col + li_col - m_new)
            kw = k.astype(F32) * ws
            c_ref[hh] = decay * c_state + jnp.dot(kw.T.astype(BF16), v, preferred_element_type=F32)
            n_ref[hh] = decay * n_state + jnp.sum(kw, axis=0, keepdims=True)
            m_ref[hh] = m_new
            y = hs * lax.rsqrt(jnp.mean(hs * hs, axis=-1, keepdims=True) + NORM_EPS) \
                * hn_ref[:, hh * dv:(hh + 1) * dv]
            o_ref[pl.ds(r0, L), hh * dv:(hh + 1) * dv] = (
                og_ref[pl.ds(r0, L), hh * dv:(hh + 1) * dv].astype(F32) * y).astype(BF16)
        return carry

    lax.fori_loop(0, seq // L, chunk_body, 0)


def _mlstm_scan(q, k, v, og, gl, glt, head_norm, batch, seq, chunk, heads_per_step):
    t = q.shape[0]
    hps = heads_per_step
    dk, dv = MLSTM_QK_DIM, MLSTM_V_DIM
    ng = 2 * MLSTM_HEADS
    est = 2 * seq * hps * (2 * dk + 3 * dv) * 2 + 2 * seq * LANES * 4 + 2 * ng * seq * 4 \
        + hps * dk * dv * 4 + 16 * chunk * chunk * 4
    return pl.pallas_call(
        functools.partial(_mlstm_scan_kernel, chunk=chunk, heads_per_step=hps),
        grid=(batch, MLSTM_HEADS // hps),
        in_specs=[pl.BlockSpec((seq, hps * dk), lambda b, g: (b, g)),
                  pl.BlockSpec((seq, hps * dk), lambda b, g: (b, g)),
                  pl.BlockSpec((seq, hps * dv), lambda b, g: (b, g)),
                  pl.BlockSpec((seq, hps * dv), lambda b, g: (b, g)),
                  pl.BlockSpec((seq, LANES), lambda b, g: (b, 0)),
                  pl.BlockSpec((ng, seq), lambda b, g: (0, b)),
                  pl.BlockSpec((1, hps * dv), lambda b, g: (0, g))],
        out_specs=pl.BlockSpec((seq, hps * dv), lambda b, g: (b, g)),
        out_shape=jax.ShapeDtypeStruct((t, MLSTM_HEADS * dv), BF16),
        scratch_shapes=[pltpu.VMEM((hps, dk, dv), F32), pltpu.VMEM((hps, 1, dk), F32),
                        pltpu.VMEM((hps, 1, 1), F32)],
        compiler_params=pltpu.CompilerParams(
            dimension_semantics=("arbitrary", "arbitrary"), vmem_limit_bytes=_vmem_limit(est)),
        name="mlstm_scan",
    )(q, k, v, og, gl, glt, head_norm.reshape(1, -1))


def kernel(x, c, positions, mla_w_in, mla_q_norm, mla_kv_norm, mla_w_uq, mla_w_ukv, mla_w_o, mlstm_w_in, mlstm_b_gates, mlstm_head_norm, mlstm_w_out, moe_w_router, moe_b_router, moe_w_gu, moe_b_gu, moe_w_down, moe_b_down, ada_w, ada_b, ln_g, ln_b):
    batch, seq, d = x.shape
    assert d == D_MODEL and ada_w.shape[0] == DEPTH
    cfg = _tiles(batch, seq)
    t = batch * seq
    tm = cfg["tm_tok"]
    mod = _adaln(c, ada_w, ada_b, cfg["tn_ada"])
    lng = ln_g.reshape(2 * DEPTH, d)
    lnb = ln_b.reshape(2 * DEPTH, d)
    xt = x.reshape(t, d)
    for i in range(DEPTH):
        j = i // 2
        k_mix, k_ffn = 2 * i, 2 * i + 1
        if i % 2 == 0:
            qn, qr, kn, v, kr = _mla_proj(xt, mod, k_mix, positions, mla_w_in[j], mla_q_norm[j],
                                          mla_kv_norm[j], mla_w_uq[j], mla_w_ukv[j], seq, tm)
            a = _attention(qn, qr, kn, kr, v, batch, seq, cfg["tq"])
            w_o = mla_w_o[j]
        else:
            q, k, v, og, gl, glt = _mlstm_proj(xt, mod, k_mix, mlstm_w_in[j], mlstm_b_gates[j], seq, tm)
            a = _mlstm_scan(q, k, v, og, gl, glt, mlstm_head_norm[j], batch, seq,
                            cfg["chunk"], cfg["heads_per_step"])
            w_o = mlstm_w_out[j]
        x1, h2, topi, topg = _post(a, xt, mod, k_mix, k_ffn, lng[k_mix], lnb[k_mix], w_o,
                                   moe_w_router[i], moe_b_router[i], seq, tm)
        xt = _moe(h2, topi, topg, x1, mod, k_ffn, lng[k_ffn], lnb[k_ffn],
                  moe_w_gu, moe_b_gu, moe_w_down, moe_b_down, i, seq, cfg)
    return xt.reshape(batch, seq, d)
```

```python
import functools

import jax
import jax.numpy as jnp
from jax import lax
from jax.experimental import pallas as pl
from jax.experimental.pallas import tpu as pltpu

F32 = jnp.float32
BF16 = jnp.bfloat16

D_MODEL = 2048
DEPTH = 2

MLA_HEADS = 16
QK_NOPE_DIM = 128
QK_ROPE_DIM = 64
V_HEAD_DIM = 128
Q_LORA_RANK = 512
KV_LORA_RANK = 512
ROPE_THETA = 10000.0

MLSTM_HEADS = 8
MLSTM_QK_DIM = D_MODEL // (2 * MLSTM_HEADS)
MLSTM_V_DIM = D_MODEL // MLSTM_HEADS
GATE_SOFTCAP = 15.0

N_EXPERTS = 32
TOP_K = 4
D_EXPERT = D_MODEL
SWIGLU_ALPHA = 1.702
SWIGLU_LIMIT = 7.0

DEEPNORM_ALPHA = (2 * DEPTH) ** 0.25
NORM_EPS = 1e-6

LANES = 128
V7X_VMEM_BYTES = 64 * 1024 * 1024
NEG_INF = float("-inf")


def _vmem_limit(estimate_bytes):
    return int(min(estimate_bytes + (12 << 20), V7X_VMEM_BYTES - (8 << 20)))


def _tiles(batch, seq):
    t = batch * seq
    cfg = dict(
        tm_tok=min(256, seq),
        tm_post=min(512, seq),
        tq=min(256, seq),
        chunk=min(128, seq),
        heads_per_step=2,
        tm_moe=min(512, t * TOP_K // 8),
        tf=1024,
        tc_gemm=512,
        tn=1024,
        tc=min(128, seq),
        tn_ada=768,
    )
    return cfg


def _adaln_kernel(c_ref, w_ref, b_ref, o_ref):
    c = c_ref[...]
    cond = c * jax.nn.sigmoid(c)
    o_ref[...] = jnp.dot(cond.astype(BF16), w_ref[...].astype(BF16),
                         preferred_element_type=F32) + b_ref[...]


def _adaln(c, ada_w, ada_b, tn):
    b, d = c.shape
    n_mod = ada_w.shape[0] * ada_w.shape[1]
    w = ada_w.reshape(n_mod, d, 3 * d)
    bias = ada_b.reshape(n_mod, 1, 3 * d)
    out = pl.pallas_call(
        _adaln_kernel,
        grid=(n_mod, 3 * d // tn),
        in_specs=[
            pl.BlockSpec((b, d), lambda i, j: (0, 0)),
            pl.BlockSpec((None, d, tn), lambda i, j: (i, 0, j)),
            pl.BlockSpec((None, 1, tn), lambda i, j: (i, 0, j)),
        ],
        out_specs=pl.BlockSpec((None, b, tn), lambda i, j: (i, 0, j)),
        out_shape=jax.ShapeDtypeStruct((n_mod, b, 3 * d), F32),
        compiler_params=pltpu.CompilerParams(
            dimension_semantics=("arbitrary", "arbitrary"),
            vmem_limit_bytes=_vmem_limit(2 * d * tn * 4 + d * tn * 2)),
        name="adaln",
    )(c, w, bias)
    return out.reshape(n_mod, b, 1, 3 * d)


def _mod_spec(k, part, tiles_per_batch):
    return pl.BlockSpec((None, None, 1, D_MODEL),
                        lambda i, *_: (k, i // tiles_per_batch, 0, part))


def _row_spec(k):
    return pl.BlockSpec((None, 1, D_MODEL), lambda i, *_: (k, 0, 0))


def _resident(shape):
    nd = len(shape)
    return pl.BlockSpec(shape, lambda *_: (0,) * nd, pipeline_mode=pl.Buffered(1))


def _mla_proj_kernel(x_ref, shift_ref, scale_ref, pos_ref, rope_ref, win_ref, qnorm_ref, kvnorm_ref,
                     wqn_ref, wqr_ref, wqs_ref, wkn_ref, wv_ref,
                     qn_ref, qr_ref, kn_ref, v_ref, kr_ref):
    h = x_ref[...] * (1.0 + scale_ref[...]) + shift_ref[...]
    lat = jnp.dot(h.astype(BF16), win_ref[...], preferred_element_type=F32)
    cq = lat[:, :Q_LORA_RANK]
    ckv = lat[:, Q_LORA_RANK:Q_LORA_RANK + KV_LORA_RANK]
    kr = lat[:, Q_LORA_RANK + KV_LORA_RANK:Q_LORA_RANK + KV_LORA_RANK + LANES]
    kr_sw = lat[:, Q_LORA_RANK + KV_LORA_RANK + LANES:]
    cq = cq * lax.rsqrt(jnp.mean(cq * cq, axis=-1, keepdims=True) + NORM_EPS) * qnorm_ref[...]
    ckv = ckv * lax.rsqrt(jnp.mean(ckv * ckv, axis=-1, keepdims=True) + NORM_EPS) * kvnorm_ref[...]
    cq = cq.astype(BF16)
    ckv = ckv.astype(BF16)

    ang = pos_ref[...].astype(F32) * rope_ref[0:1, :]
    cos_t = jnp.cos(ang) * rope_ref[1:2, :]
    sin_t = jnp.sin(ang) * rope_ref[2:3, :]

    qn_ref[...] = jnp.dot(cq, wqn_ref[...], preferred_element_type=F32).astype(BF16)
    q_rope = jnp.dot(cq, wqr_ref[...], preferred_element_type=F32)
    q_swap = jnp.dot(cq, wqs_ref[...], preferred_element_type=F32)
    cos_h = jnp.tile(cos_t, (1, MLA_HEADS))
    sin_h = jnp.tile(sin_t, (1, MLA_HEADS))
    qr_ref[...] = (q_rope * cos_h + q_swap * sin_h).astype(BF16)
    kn_ref[...] = jnp.dot(ckv, wkn_ref[...], preferred_element_type=F32).astype(BF16)
    v_ref[...] = jnp.dot(ckv, wv_ref[...], preferred_element_type=F32).astype(BF16)
    kr_ref[...] = (kr * cos_t + kr_sw * sin_t).astype(BF16)


def _rope_table():
    half = QK_ROPE_DIM // 2
    inv_freq = ROPE_THETA ** (-jnp.arange(0, QK_ROPE_DIM, 2, dtype=F32) / QK_ROPE_DIM)
    zeros = jnp.zeros((LANES - QK_ROPE_DIM,), F32)
    freq_row = jnp.concatenate([inv_freq, inv_freq, zeros])
    cos_mask = jnp.concatenate([jnp.ones((QK_ROPE_DIM,), F32), zeros])
    sin_sign = jnp.concatenate([-jnp.ones((half,), F32), jnp.ones((half,), F32), zeros])
    pad = jnp.zeros((5, LANES), F32)
    return jnp.concatenate([jnp.stack([freq_row, cos_mask, sin_sign]), pad], axis=0)


def _mla_weights(w_in, w_uq, w_ukv):
    d = w_in.shape[0]
    h = MLA_HEADS
    half = QK_ROPE_DIM // 2
    lat = Q_LORA_RANK + KV_LORA_RANK
    kr = w_in[:, lat:]
    zpad = jnp.zeros((d, LANES - QK_ROPE_DIM), w_in.dtype)
    win = jnp.concatenate([w_in[:, :lat], kr, zpad, kr[:, half:], kr[:, :half], zpad], axis=1)
    wq = w_uq.reshape(Q_LORA_RANK, h, QK_NOPE_DIM + QK_ROPE_DIM)
    wqn = wq[:, :, :QK_NOPE_DIM].reshape(Q_LORA_RANK, h * QK_NOPE_DIM)
    rope = wq[:, :, QK_NOPE_DIM:]
    zq = jnp.zeros((Q_LORA_RANK, h, LANES - QK_ROPE_DIM), w_uq.dtype)
    wqr = jnp.concatenate([rope, zq], axis=-1).reshape(Q_LORA_RANK, h * LANES)
    wqs = jnp.concatenate([rope[..., half:], rope[..., :half], zq], axis=-1).reshape(Q_LORA_RANK, h * LANES)
    wkv = w_ukv.reshape(KV_LORA_RANK, h, QK_NOPE_DIM + V_HEAD_DIM)
    wkn = wkv[:, :, :QK_NOPE_DIM].reshape(KV_LORA_RANK, h * QK_NOPE_DIM)
    wv = wkv[:, :, QK_NOPE_DIM:].reshape(KV_LORA_RANK, h * V_HEAD_DIM)
    return tuple(a.astype(BF16) for a in (win, wqn, wqr, wqs, wkn, wv))


def _mla_proj(x2d, mod, k_mod, positions, w_in, q_norm, kv_norm, w_uq, w_ukv, seq, tm):
    t, d = x2d.shape
    tiles_per_batch = seq // tm
    win, wqn, wqr, wqs, wkn, wv = _mla_weights(w_in, w_uq, w_ukv)
    hn = MLA_HEADS * LANES
    tok = lambda n: pl.BlockSpec((tm, n), lambda i: (i, 0))
    weights_bytes = 2 * (win.size + wqn.size + wqr.size + wqs.size + wkn.size + wv.size)
    est = weights_bytes + 2 * tm * d * 4 + 2 * (4 * tm * hn * 2 + tm * LANES * 2) + 8 * tm * hn * 4
    outs = pl.pallas_call(
        _mla_proj_kernel,
        grid=(t // tm,),
        in_specs=[
            tok(d),
            _mod_spec(k_mod, 0, tiles_per_batch),
            _mod_spec(k_mod, 1, tiles_per_batch),
            pl.BlockSpec((tm, 1), lambda i: (i, 0)),
            _resident((8, LANES)),
            _resident(win.shape),
            _resident((1, Q_LORA_RANK)),
            _resident((1, KV_LORA_RANK)),
            _resident(wqn.shape), _resident(wqr.shape), _resident(wqs.shape),
            _resident(wkn.shape), _resident(wv.shape),
        ],
        out_specs=[tok(hn), tok(hn), tok(hn), tok(hn), tok(LANES)],
        out_shape=[jax.ShapeDtypeStruct((t, hn), BF16)] * 4 + [jax.ShapeDtypeStruct((t, LANES), BF16)],
        compiler_params=pltpu.CompilerParams(
            dimension_semantics=("arbitrary",), vmem_limit_bytes=_vmem_limit(est)),
        name="mla_proj",
    )(x2d, mod, mod, positions.reshape(t, 1), _rope_table(), win,
      q_norm.reshape(1, -1), kv_norm.reshape(1, -1), wqn, wqr, wqs, wkn, wv)
    return outs


def _attn_kernel(qn_ref, qr_ref, kn_ref, kr_ref, v_ref, o_ref, s_ref, *, tq, scale):
    seq = qn_ref.shape[0]
    row = lax.broadcasted_iota(jnp.int32, (tq, tq), 0)
    col = lax.broadcasted_iota(jnp.int32, (tq, tq), 1)
    causal = col <= row
    exp2_scale = scale * 1.4426950408889634

    def fold_lanes(a):
        return [a[:, g * LANES:(g + 1) * LANES] for g in range(tq // LANES)]

    for qi in range(seq // tq):
        rows = slice(qi * tq, (qi + 1) * tq)
        q = jnp.concatenate([qn_ref[rows, :], qr_ref[rows, :]], axis=-1)
        m_acc = jnp.full((tq, LANES), NEG_INF, F32)
        for j in range(qi + 1):
            cols = slice(j * tq, (j + 1) * tq)
            k = jnp.concatenate([kn_ref[cols, :], kr_ref[cols, :]], axis=-1)
            s = lax.dot_general(q, k, (((1,), (1,)), ((), ())), preferred_element_type=F32)
            if j == qi:
                s = jnp.where(causal, s, NEG_INF)
            s_ref[:, cols] = s
            for part in fold_lanes(s):
                m_acc = jnp.maximum(m_acc, part)
        m = jnp.max(m_acc, axis=-1, keepdims=True)
        l_acc = jnp.zeros((tq, LANES), F32)
        acc = jnp.zeros((tq, V_HEAD_DIM), F32)
        for j in range(qi + 1):
            cols = slice(j * tq, (j + 1) * tq)
            p = jnp.exp2((s_ref[:, cols] - m) * exp2_scale)
            for part in fold_lanes(p):
                l_acc = l_acc + part
            acc = acc + jnp.dot(p.astype(BF16), v_ref[cols, :], preferred_element_type=F32)
        l = jnp.sum(l_acc, axis=-1, keepdims=True)
        o_ref[rows, :] = (acc / l).astype(BF16)


def _attention(qn, qr, kn, kr, v, batch, seq, tq):
    t = qn.shape[0]
    head = lambda: pl.BlockSpec((seq, LANES), lambda b, h: (b, h))
    scale = (QK_NOPE_DIM + QK_ROPE_DIM) ** -0.5
    est = 2 * 6 * seq * LANES * 2 + tq * seq * 4 + 6 * tq * tq * 4
    return pl.pallas_call(
        functools.partial(_attn_kernel, tq=tq, scale=scale),
        grid=(batch, MLA_HEADS),
        in_specs=[head(), head(), head(), pl.BlockSpec((seq, LANES), lambda b, h: (b, 0)), head()],
        out_specs=head(),
        out_shape=jax.ShapeDtypeStruct((t, MLA_HEADS * V_HEAD_DIM), BF16),
        scratch_shapes=[pltpu.VMEM((tq, seq), F32)],
        compiler_params=pltpu.CompilerParams(
            dimension_semantics=("arbitrary", "arbitrary"), vmem_limit_bytes=_vmem_limit(est)),
        name="mla_attention",
    )(qn, qr, kn, kr, v)


SLABS = D_MODEL // LANES


def _store_slabs(ref, val, first_slab=0, first_row=0):
    rows = val.shape[0]
    for s in range(val.shape[1] // LANES):
        ref[pl.ds(first_row * SLABS + first_slab + s, rows, stride=SLABS), :] = val[:, s * LANES:(s + 1) * LANES]


def _layer_norm(z, g, b):
    mu = jnp.mean(z, axis=-1, keepdims=True)
    zc = z - mu
    var = jnp.mean(zc * zc, axis=-1, keepdims=True)
    return zc * lax.rsqrt(var + NORM_EPS) * g + b


def _post_kernel(a_ref, x_ref, gate_ref, lng_ref, lnb_ref, shift_ref, scale_ref, wo_ref, wr_ref, br_ref,
                 x1_ref, h2_ref, topi_ref, topg_ref, *, sub):
    starts = range(0, a_ref.shape[0], sub)
    ys = [jnp.dot(a_ref[r0:r0 + sub, :], wo_ref[...], preferred_element_type=F32) for r0 in starts]
    for y, r0 in zip(ys, starts):
        _post_rows(slice(r0, r0 + sub), r0, y, x_ref, gate_ref, lng_ref, lnb_ref, shift_ref, scale_ref,
                   wr_ref, br_ref, x1_ref, h2_ref, topi_ref, topg_ref)


def _post_rows(rows, r0, y, x_ref, gate_ref, lng_ref, lnb_ref, shift_ref, scale_ref, wr_ref,
               br_ref, x1_ref, h2_ref, topi_ref, topg_ref):
    z = DEEPNORM_ALPHA * x_ref[rows, :] + gate_ref[...] * y
    x1 = _layer_norm(z, lng_ref[...], lnb_ref[...])
    x1_ref[rows, :] = x1
    h2 = x1 * (1.0 + scale_ref[...]) + shift_ref[...]
    _store_slabs(h2_ref, h2, first_row=r0)
    h_hi = h2.astype(BF16)
    h_lo = (h2 - h_hi.astype(F32)).astype(BF16)
    t_hi = jnp.dot(h_hi, wr_ref[...], preferred_element_type=F32)
    t_lo = jnp.dot(h_lo, wr_ref[:, :LANES], preferred_element_type=F32)
    logits = t_hi[:, :LANES] + (t_hi[:, LANES:] + t_lo) + br_ref[...]
    tm = logits.shape[0]
    lane = lax.broadcasted_iota(jnp.int32, (tm, LANES), 1)
    vals, idxs = [], []
    for _ in range(TOP_K):
        mk = jnp.max(logits, axis=-1, keepdims=True)
        ik = jnp.min(jnp.where(logits == mk, lane, LANES), axis=-1, keepdims=True)
        vals.append(mk)
        idxs.append(ik)
        logits = jnp.where(lane == ik, NEG_INF, logits)
    exps = [jnp.exp(v - vals[0]) for v in vals]
    den = exps[0]
    for e in exps[1:]:
        den = den + e
    topi = jnp.zeros((tm, LANES), jnp.int32)
    topg = jnp.zeros((tm, LANES), F32)
    for k in range(TOP_K):
        topi = jnp.where(lane == k, idxs[k], topi)
        topg = jnp.where(lane == k, exps[k] / den, topg)
    topi_ref[rows, :] = topi
    topg_ref[rows, :] = topg


def _post(a, x2d, mod, k_mix, k_ffn, ln_g, ln_b, w_o, w_router, b_router, seq, tm):
    t, d = x2d.shape
    tiles_per_batch = seq // tm
    wr = jnp.concatenate([w_router, jnp.zeros((d, LANES - N_EXPERTS), F32)], axis=1)
    wr_hi = wr.astype(BF16)
    wr_lo = (wr - wr_hi.astype(F32)).astype(BF16)
    wr = jnp.concatenate([wr_hi, wr_lo], axis=1)
    br = jnp.concatenate([b_router, jnp.full((LANES - N_EXPERTS,), NEG_INF, F32)]).reshape(1, LANES)
    tok = lambda n: pl.BlockSpec((tm, n), lambda i: (i, 0))
    sub = min(256, tm)
    est = d * d * 2 + d * LANES * 4 + 2 * tm * d * (2 + 4 + 4 + 4) + 6 * sub * d * 4
    return pl.pallas_call(
        functools.partial(_post_kernel, sub=sub),
        grid=(t // tm,),
        in_specs=[
            tok(d), tok(d),
            _mod_spec(k_mix, 2, tiles_per_batch),
            _resident((1, d)), _resident((1, d)),
            _mod_spec(k_ffn, 0, tiles_per_batch),
            _mod_spec(k_ffn, 1, tiles_per_batch),
            _resident((d, d)), _resident((d, 2 * LANES)), _resident((1, LANES)),
        ],
        out_specs=[tok(d), pl.BlockSpec((tm * SLABS, LANES), lambda i: (i, 0)), tok(LANES), tok(LANES)],
        out_shape=[jax.ShapeDtypeStruct((t, d), F32), jax.ShapeDtypeStruct((t * SLABS, LANES), F32),
                   jax.ShapeDtypeStruct((t, LANES), jnp.int32), jax.ShapeDtypeStruct((t, LANES), F32)],
        compiler_params=pltpu.CompilerParams(
            dimension_semantics=("arbitrary",), vmem_limit_bytes=_vmem_limit(est)),
        name="mixer_out_ln_router",
    )(a, x2d, mod, ln_g.reshape(1, d), ln_b.reshape(1, d), mod, mod, w_o.astype(BF16), wr, br)


def _route(topi, tm_moe):
    t = topi.shape[0]
    m = t * TOP_K
    e_tk = topi[:, :TOP_K]
    a = jnp.sum((e_tk[:, :, None] == jnp.arange(N_EXPERTS, dtype=jnp.int32)).astype(F32), axis=1)
    blk = min(256, t)
    ab = a.reshape(t // blk, blk, N_EXPERTS)
    tri = jnp.tril(jnp.ones((blk, blk), F32))
    within = jnp.einsum("ts,bse->bte", tri, ab)
    btot = within[:, -1, :]
    boff = jnp.cumsum(btot, axis=0) - btot
    excl = (within + boff[:, None, :] - ab).reshape(t, N_EXPERTS)
    counts = (boff[-1] + btot[-1]).astype(jnp.int32)
    rank = jnp.take_along_axis(excl, e_tk, axis=1).astype(jnp.int32)
    padded = ((counts + tm_moe - 1) // tm_moe) * tm_moe
    pends = jnp.cumsum(padded)
    pstarts = pends - padded
    dest = (pstarts[e_tk] + rank).astype(jnp.int32).reshape(m)
    n_tiles = m // tm_moe + N_EXPERTS
    row_tok = jnp.zeros((n_tiles * tm_moe,), jnp.int32).at[dest].set(
        jnp.arange(m, dtype=jnp.int32) // TOP_K, unique_indices=True)
    tile_start = (pstarts // tm_moe).astype(jnp.int32)
    tile_count = (padded // tm_moe).astype(jnp.int32)
    n_used = (pends[-1] // tm_moe).astype(jnp.int32).reshape(1)
    return dest, row_tok, tile_start, tile_count, n_used


def _gather_issue(idx_ref, src_hbm, buf, sem, slot, n_rows):
    def body(i, carry):
        for u in range(2):
            r = 2 * i + u
            pltpu.make_async_copy(src_hbm.at[idx_ref[0, 0, r]], buf.at[slot, :, r, :],
                                  sem.at[slot]).start(priority=u)
        return carry
    lax.fori_loop(0, n_rows // 2, body, 0, unroll=4)


def _gather_wait(buf, sem, slot):
    pltpu.make_async_copy(buf.at[slot], buf.at[slot], sem.at[slot]).wait()


def _dispatch_kernel(ns_ref, idx_ref, nxt_ref, src_hbm, o_ref, buf, sem, *, n_rows):
    i = pl.program_id(0)
    n_active = ns_ref[0]
    slot = lax.rem(i, 2)

    @pl.when(jnp.logical_and(i == 0, n_active > 0))
    def _():
        _gather_issue(idx_ref, src_hbm, buf, sem, 0, n_rows)

    @pl.when(i + 1 < n_active)
    def _():
        _gather_issue(nxt_ref, src_hbm, buf, sem, 1 - slot, n_rows)

    @pl.when(i < n_active)
    def _():
        _gather_wait(buf, sem, slot)
        o_ref[...] = buf[slot].astype(BF16)

    @pl.when(i >= n_active)
    def _():
        o_ref[...] = jnp.zeros(o_ref.shape, o_ref.dtype)


def _dispatch(h2_slabs, row_tok, n_used, r):
    t, slabs, _ = h2_slabs.shape
    n = row_tok.shape[0] // r
    idx = row_tok.reshape(n, 1, r)
    smem = lambda f: pl.BlockSpec((1, 1, r), f, memory_space=pltpu.SMEM)
    return pl.pallas_call(
        functools.partial(_dispatch_kernel, n_rows=r),
        grid_spec=pltpu.PrefetchScalarGridSpec(
            num_scalar_prefetch=1,
            grid=(n,),
            in_specs=[smem(lambda i, ns: (i, 0, 0)),
                      smem(lambda i, ns: (jnp.minimum(i + 1, n - 1), 0, 0)),
                      pl.BlockSpec(memory_space=pl.ANY)],
            out_specs=pl.BlockSpec((None, slabs, r, LANES), lambda i, ns: (i, 0, 0, 0)),
            scratch_shapes=[pltpu.VMEM((2, slabs, r, LANES), F32), pltpu.SemaphoreType.DMA((2,))]),
        out_shape=jax.ShapeDtypeStruct((n, slabs, r, LANES), BF16),
        compiler_params=pltpu.CompilerParams(
            dimension_semantics=("arbitrary",),
            vmem_limit_bytes=_vmem_limit(2 * r * slabs * LANES * (4 + 2))),
        name="moe_dispatch_gather",
    )(n_used, idx, idx, h2_slabs)


def _expert_weights(step, n_steps, copies_for, stage, w_bf):
    @pl.when(step == 0)
    def _():
        for c in copies_for(step):
            c.start()

    for c in copies_for(step):
        c.wait()
    w_bf[...] = stage[...].astype(BF16)

    @pl.when(step + 1 < n_steps)
    def _():
        for c in copies_for(step + 1):
            c.start()


def _tile_loop(n_tiles, in_copies, out_copy, compute):
    def start_in(k, slot):
        for c in in_copies(k, slot):
            c.start()

    @pl.when(n_tiles > 0)
    def _():
        start_in(0, 0)

    def body(k, carry):
        slot = lax.rem(k, 2)
        for c in in_copies(k, slot):
            c.wait()

        @pl.when(k + 1 < n_tiles)
        def _():
            start_in(k + 1, 1 - slot)

        @pl.when(k >= 2)
        def _():
            out_copy(k - 2, slot).wait()

        compute(slot)
        out_copy(k, slot).start()
        return carry

    lax.fori_loop(0, n_tiles, body, 0)

    @pl.when(n_tiles >= 2)
    def _():
        out_copy(n_tiles - 2, lax.rem(n_tiles, 2)).wait()

    @pl.when(n_tiles >= 1)
    def _():
        out_copy(n_tiles - 1, lax.rem(n_tiles + 1, 2)).wait()


def _zero_tail_tiles(first, n_total, zero_src, dst_for, sem):
    def body(t, carry):
        cp = pltpu.make_async_copy(zero_src, dst_for(t), sem)
        cp.start()
        cp.wait()
        return carry
    lax.fori_loop(first, n_total, body, 0)


def _gemm1_kernel(ts_ref, nt_ref, nu_ref, bg_ref, bu_ref, w_hbm, x_hbm, o_hbm,
                  stage, w_bf, xbuf, obuf, wsem, xsem, osem, *, layer, tf, tc):
    j = pl.program_id(0)
    e = pl.program_id(1)
    n_e = pl.num_programs(1)
    f = D_EXPERT

    def weight_copies(step):
        jj = step // n_e
        ee = step - jj * n_e
        return [pltpu.make_async_copy(
            w_hbm.at[layer, ee, :, pl.ds(pl.multiple_of(half * f + jj * tf, tf), tf)],
            stage.at[half], wsem.at[half]) for half in range(2)]

    _expert_weights(j * n_e + e, pl.num_programs(0) * n_e, weight_copies, stage, w_bf)

    t0 = ts_ref[e]

    def x_copies(k, slot):
        return [pltpu.make_async_copy(x_hbm.at[t0 + k], xbuf.at[slot], xsem.at[slot])]

    def o_copy(k, slot):
        return pltpu.make_async_copy(obuf.at[slot], o_hbm.at[j, t0 + k], osem.at[slot])

    def compute(slot):
        x = jnp.concatenate([xbuf[slot, s] for s in range(xbuf.shape[1])], axis=-1)
        for c in range(tf // tc):
            cols = slice(c * tc, (c + 1) * tc)
            g = jnp.dot(x, w_bf[0, :, cols], preferred_element_type=F32) + bg_ref[:, cols]
            u = jnp.dot(x, w_bf[1, :, cols], preferred_element_type=F32) + bu_ref[:, cols]
            g = jnp.minimum(g, SWIGLU_LIMIT)
            u = jnp.clip(u, -SWIGLU_LIMIT, SWIGLU_LIMIT)
            obuf[slot, :, cols] = (g * jax.nn.sigmoid(SWIGLU_ALPHA * g) * (u + 1.0)).astype(BF16)

    _tile_loop(nt_ref[e], x_copies, o_copy, compute)

    @pl.when(e == n_e - 1)
    def _():
        obuf[0] = jnp.zeros(obuf.shape[1:], obuf.dtype)
        _zero_tail_tiles(nu_ref[0], o_hbm.shape[1], obuf.at[0], lambda t: o_hbm.at[j, t], osem.at[0])


def _gemm1(x_rows, tile_start, tile_count, n_used, w_gu, b_gu, layer, tf, tc):
    n_tiles, slabs, tm, _ = x_rows.shape
    d = slabs * LANES
    f = D_EXPERT
    nf = f // tf
    bias = b_gu.reshape(b_gu.shape[0], N_EXPERTS, 1, 2 * f)
    bspec = lambda off: pl.BlockSpec((None, None, 1, tf), lambda j, e, ts, nt, nu: (layer, e, 0, j + off))
    any_spec = pl.BlockSpec(memory_space=pl.ANY)
    est = 2 * d * tf * (4 + 2) + 2 * tm * d * 2 + 2 * tm * tf * 2 + 6 * tm * tc * 4
    return pl.pallas_call(
        functools.partial(_gemm1_kernel, layer=layer, tf=tf, tc=tc),
        grid_spec=pltpu.PrefetchScalarGridSpec(
            num_scalar_prefetch=3,
            grid=(nf, N_EXPERTS),
            in_specs=[bspec(0), bspec(nf), any_spec, any_spec],
            out_specs=any_spec,
            scratch_shapes=[pltpu.VMEM((2, d, tf), F32), pltpu.VMEM((2, d, tf), BF16),
                            pltpu.VMEM((2, slabs, tm, LANES), BF16), pltpu.VMEM((2, tm, tf), BF16),
                            pltpu.SemaphoreType.DMA((2,)), pltpu.SemaphoreType.DMA((2,)),
                            pltpu.SemaphoreType.DMA((2,))]),
        out_shape=jax.ShapeDtypeStruct((nf, n_tiles, tm, tf), BF16),
        compiler_params=pltpu.CompilerParams(
            dimension_semantics=("arbitrary", "arbitrary"), vmem_limit_bytes=_vmem_limit(est)),
        name="moe_gate_up_swiglu",
    )(tile_start, tile_count, n_used, bias, bias, w_gu, x_rows)


def _gemm2_kernel(ts_ref, nt_ref, nu_ref, b_ref, w_hbm, a_hbm, o_hbm,
                  stage, w_bf, abuf, obuf, wsem, asem, osem, *, layer, tn):
    e = pl.program_id(0)
    nf, tm = abuf.shape[1], abuf.shape[2]
    tile_rows = tm * SLABS

    def weight_copies(step):
        return [pltpu.make_async_copy(w_hbm.at[layer, step], stage, wsem.at[0])]

    _expert_weights(e, pl.num_programs(0), weight_copies, stage, w_bf)

    t0 = ts_ref[e]

    def a_copies(k, slot):
        return [pltpu.make_async_copy(a_hbm.at[c, t0 + k], abuf.at[slot, c], asem.at[slot, c])
                for c in range(nf)]

    def o_tile(t):
        return o_hbm.at[pl.ds(pl.multiple_of(t * tile_rows, tile_rows), tile_rows), :]

    def o_copy(k, slot):
        return pltpu.make_async_copy(
            obuf.at[pl.ds(pl.multiple_of(slot * tile_rows, tile_rows), tile_rows), :],
            o_tile(t0 + k), osem.at[slot])

    def compute(slot):
        a = jnp.concatenate([abuf[slot, c] for c in range(nf)], axis=-1)
        for c in range(w_bf.shape[1] // tn):
            cols = slice(c * tn, (c + 1) * tn)
            res = jnp.dot(a, w_bf[:, cols], preferred_element_type=F32) + b_ref[:, cols]
            _store_slabs(obuf, res, first_slab=c * tn // LANES, first_row=slot * tm)

    _tile_loop(nt_ref[e], a_copies, o_copy, compute)

    @pl.when(e == pl.num_programs(0) - 1)
    def _():
        obuf[0:tile_rows, :] = jnp.zeros((tile_rows, LANES), obuf.dtype)
        _zero_tail_tiles(nu_ref[0], o_hbm.shape[0] // tile_rows, obuf.at[0:tile_rows, :], o_tile, osem.at[0])


def _gemm2(act, tile_start, tile_count, n_used, w_down, b_down, layer, tn):
    nf, n_tiles, tm, tf = act.shape
    f = nf * tf
    d = D_MODEL
    bias = b_down.reshape(b_down.shape[0], N_EXPERTS, 1, d)
    any_spec = pl.BlockSpec(memory_space=pl.ANY)
    est = f * d * (4 + 2) + 2 * tm * f * 2 + 2 * tm * d * 4 + 3 * tm * tn * 4
    return pl.pallas_call(
        functools.partial(_gemm2_kernel, layer=layer, tn=tn),
        grid_spec=pltpu.PrefetchScalarGridSpec(
            num_scalar_prefetch=3,
            grid=(N_EXPERTS,),
            in_specs=[pl.BlockSpec((None, None, 1, d), lambda e, ts, nt, nu: (layer, e, 0, 0)),
                      any_spec, any_spec],
            out_specs=any_spec,
            scratch_shapes=[pltpu.VMEM((f, d), F32), pltpu.VMEM((f, d), BF16),
                            pltpu.VMEM((2, nf, tm, tf), BF16), pltpu.VMEM((2 * tm * SLABS, LANES), F32),
                            pltpu.SemaphoreType.DMA((1,)), pltpu.SemaphoreType.DMA((2, nf)),
                            pltpu.SemaphoreType.DMA((2,))]),
        out_shape=jax.ShapeDtypeStruct((n_tiles * tm * SLABS, LANES), F32),
        compiler_params=pltpu.CompilerParams(
            dimension_semantics=("arbitrary",), vmem_limit_bytes=_vmem_limit(est)),
        name="moe_down",
    )(tile_start, tile_count, n_used, bias, w_down, act)


def _combine_kernel(idx_ref, nxt_ref, rows_hbm, g_ref, x_ref, gate_ref, lng_ref, lnb_ref, o_ref,
                    buf, sem, *, tc):
    i = pl.program_id(0)
    n = pl.num_programs(0)
    slot = lax.rem(i, 2)
    n_rows = TOP_K * tc

    @pl.when(i == 0)
    def _():
        _gather_issue(idx_ref, rows_hbm, buf, sem, 0, n_rows)

    @pl.when(i + 1 < n)
    def _():
        _gather_issue(nxt_ref, rows_hbm, buf, sem, 1 - slot, n_rows)

    _gather_wait(buf, sem, slot)
    g = g_ref[...]
    gk = [jnp.broadcast_to(g[:, k:k + 1], (tc, LANES)) for k in range(TOP_K)]
    parts = []
    for s in range(buf.shape[1]):
        acc = gk[0] * buf[slot, s, pl.ds(0, tc), :]
        for k in range(1, TOP_K):
            acc = acc + gk[k] * buf[slot, s, pl.ds(k * tc, tc), :]
        parts.append(acc)
    y = jnp.concatenate(parts, axis=-1)
    z = DEEPNORM_ALPHA * x_ref[...] + gate_ref[...] * y
    o_ref[...] = _layer_norm(z, lng_ref[...], lnb_ref[...])


def _combine(out_rows, dest, topg, x1, mod, k_ffn, ln_g, ln_b, seq, tc):
    t, d = x1.shape
    n = t // tc
    tiles_per_batch = seq // tc
    idx = dest.reshape(n, tc, TOP_K).transpose(0, 2, 1).reshape(n, 1, TOP_K * tc)
    smem = lambda f: pl.BlockSpec((1, 1, TOP_K * tc), f, memory_space=pltpu.SMEM)
    tok = lambda w: pl.BlockSpec((tc, w), lambda i: (i, 0))
    return pl.pallas_call(
        functools.partial(_combine_kernel, tc=tc),
        grid=(n,),
        in_specs=[smem(lambda i: (i, 0, 0)),
                  smem(lambda i: (jnp.minimum(i + 1, n - 1), 0, 0)),
                  pl.BlockSpec(memory_space=pl.ANY),
                  tok(LANES), tok(d),
                  _mod_spec(k_ffn, 2, tiles_per_batch),
                  _resident((1, d)), _resident((1, d))],
        out_specs=tok(d),
        out_shape=jax.ShapeDtypeStruct((t, d), F32),
        scratch_shapes=[pltpu.VMEM((2, d // LANES, TOP_K * tc, LANES), F32), pltpu.SemaphoreType.DMA((2,))],
        compiler_params=pltpu.CompilerParams(
            dimension_semantics=("arbitrary",),
            vmem_limit_bytes=_vmem_limit(2 * TOP_K * tc * d * 4 + 6 * tc * d * 4)),
        name="moe_combine_ln",
    )(idx, idx, out_rows, topg, x1, mod, ln_g.reshape(1, d), ln_b.reshape(1, d))


def _moe(h2, topi, topg, x1, mod, k_ffn, ln_g, ln_b, w_gu, b_gu, w_down, b_down, layer, seq, cfg):
    dest, row_tok, tile_start, tile_count, n_used = _route(topi, cfg["tm_moe"])
    x_rows = _dispatch(h2.reshape(-1, SLABS, LANES), row_tok, n_used, cfg["tm_moe"])
    act = _gemm1(x_rows, tile_start, tile_count, n_used, w_gu, b_gu, layer, cfg["tf"], cfg["tc_gemm"])
    out_rows = _gemm2(act, tile_start, tile_count, n_used, w_down, b_down, layer, cfg["tn"])
    out_rows = out_rows.reshape(-1, SLABS, LANES)
    return _combine(out_rows, dest, topg, x1, mod, k_ffn, ln_g, ln_b, seq, cfg["tc"])


def _softcap(g):
    return GATE_SOFTCAP * jnp.tanh(g / GATE_SOFTCAP)


def _log_sigmoid(x):
    return jnp.minimum(x, 0.0) - jnp.log(1.0 + jnp.exp(-jnp.abs(x)))


def _mlstm_proj_kernel(x_ref, shift_ref, scale_ref, wq_ref, wk_ref, wv_ref, wo_ref, wg_ref, wgt_ref,
                       bg_ref, bgt_ref, q_ref, k_ref, v_ref, og_ref, gl_ref, glt_ref):
    h = (x_ref[...] * (1.0 + scale_ref[...]) + shift_ref[...]).astype(BF16)
    q_ref[...] = (jnp.dot(h, wq_ref[...], preferred_element_type=F32) * MLSTM_QK_DIM ** -0.5).astype(BF16)
    k_ref[...] = jnp.dot(h, wk_ref[...], preferred_element_type=F32).astype(BF16)
    v_ref[...] = jnp.dot(h, wv_ref[...], preferred_element_type=F32).astype(BF16)
    og_ref[...] = jax.nn.sigmoid(jnp.dot(h, wo_ref[...], preferred_element_type=F32)).astype(BF16)
    g = _softcap(jnp.dot(h, wg_ref[...], preferred_element_type=F32) + bg_ref[...])
    lane = lax.broadcasted_iota(jnp.int32, g.shape, 1)
    gl_ref[...] = jnp.where(lane < MLSTM_HEADS, g,
                            jnp.where(lane < 2 * MLSTM_HEADS, _log_sigmoid(g), 0.0))
    gt = _softcap(lax.dot_general(wgt_ref[...], h, (((1,), (1,)), ((), ())),
                                  preferred_element_type=F32) + bgt_ref[...])
    sub = lax.broadcasted_iota(jnp.int32, gt.shape, 0)
    glt_ref[...] = jnp.where(sub < MLSTM_HEADS, gt, _log_sigmoid(gt))


def _mlstm_proj(x2d, mod, k_mod, w_in, b_gates, seq, tm):
    t, d = x2d.shape
    tiles_per_batch = seq // tm
    hk = MLSTM_HEADS * MLSTM_QK_DIM
    hv = MLSTM_HEADS * MLSTM_V_DIM
    ng = 2 * MLSTM_HEADS
    o1, o2, o3, o4 = hk, 2 * hk, 2 * hk + hv, 2 * hk + 2 * hv
    wq, wk, wv, wo = (w_in[:, a:b].astype(BF16) for a, b in ((0, o1), (o1, o2), (o2, o3), (o3, o4)))
    wg = jnp.concatenate([w_in[:, o4:], jnp.zeros((d, LANES - ng), F32)], axis=1).astype(BF16)
    wgt = w_in[:, o4:].T.astype(BF16)
    bg = jnp.concatenate([b_gates, jnp.zeros((LANES - ng,), F32)]).reshape(1, LANES)
    bgt = b_gates.reshape(ng, 1)
    tok = lambda n: pl.BlockSpec((tm, n), lambda i: (i, 0))
    est = 2 * d * (2 * hk + 2 * hv + LANES + ng) + 2 * tm * d * 4 + 2 * tm * (2 * hk + 2 * hv) * 2 \
        + 6 * tm * hv * 4
    return pl.pallas_call(
        _mlstm_proj_kernel,
        grid=(t // tm,),
        in_specs=[tok(d), _mod_spec(k_mod, 0, tiles_per_batch), _mod_spec(k_mod, 1, tiles_per_batch),
                  _resident(wq.shape), _resident(wk.shape), _resident(wv.shape), _resident(wo.shape),
                  _resident(wg.shape), _resident(wgt.shape), _resident((1, LANES)), _resident((ng, 1))],
        out_specs=[tok(hk), tok(hk), tok(hv), tok(hv), tok(LANES),
                   pl.BlockSpec((ng, tm), lambda i: (0, i))],
        out_shape=[jax.ShapeDtypeStruct((t, hk), BF16), jax.ShapeDtypeStruct((t, hk), BF16),
                   jax.ShapeDtypeStruct((t, hv), BF16), jax.ShapeDtypeStruct((t, hv), BF16),
                   jax.ShapeDtypeStruct((t, LANES), F32), jax.ShapeDtypeStruct((ng, t), F32)],
        compiler_params=pltpu.CompilerParams(
            dimension_semantics=("arbitrary",), vmem_limit_bytes=_vmem_limit(est)),
        name="mlstm_proj",
    )(x2d, mod, mod, wq, wk, wv, wo, wg, wgt, bg, bgt)


def _mlstm_scan_kernel(q_ref, k_ref, v_ref, og_ref, gl_ref, glt_ref, hn_ref, o_ref,
                       c_ref, n_ref, m_ref, *, chunk, heads_per_step):
    seq = q_ref.shape[0]
    L = chunk
    dk, dv = MLSTM_QK_DIM, MLSTM_V_DIM
    ng = 2 * MLSTM_HEADS
    head0 = pl.program_id(1) * heads_per_step
    hi = lax.Precision.HIGHEST

    c_ref[...] = jnp.zeros(c_ref.shape, F32)
    n_ref[...] = jnp.zeros(n_ref.shape, F32)
    m_ref[...] = jnp.zeros(m_ref.shape, F32)

    row = lax.broadcasted_iota(jnp.int32, (L, L), 0)
    col = lax.broadcasted_iota(jnp.int32, (L, L), 1)
    causal = col <= row
    tri = causal.astype(F32)
    tri_t = (row <= col).astype(F32)
    lane = lax.broadcasted_iota(jnp.int32, (L, LANES), 1)
    sub = lax.broadcasted_iota(jnp.int32, (ng, L), 0)

    def chunk_body(c, carry):
        r0 = pl.multiple_of(c * L, L)
        gl = gl_ref[pl.ds(r0, L), :]
        glt = glt_ref[:, pl.ds(r0, L)]
        for hh in range(heads_per_step):
            head = head0 + hh
            li_col = jnp.sum(jnp.where(lane == head, gl, 0.0), axis=-1, keepdims=True)
            lf_col = jnp.sum(jnp.where(lane == head + MLSTM_HEADS, gl, 0.0), axis=-1, keepdims=True)
            li_row = jnp.sum(jnp.where(sub == head, glt, 0.0), axis=0, keepdims=True)
            lf_row = jnp.sum(jnp.where(sub == head + MLSTM_HEADS, glt, 0.0), axis=0, keepdims=True)
            b_col = jnp.dot(tri, jnp.broadcast_to(lf_col, (L, LANES)), precision=hi,
                            preferred_element_type=F32)[:, 0:1]
            b_row = jnp.dot(jnp.broadcast_to(lf_row, (8, L)), tri_t, precision=hi,
                            preferred_element_type=F32)[0:1, :]
            m_prev = m_ref[hh]
            d = jnp.where(causal, b_col - b_row + li_row, NEG_INF)
            m_t = jnp.maximum(b_col + m_prev, jnp.max(d, axis=-1, keepdims=True))
            w = jnp.exp(d - m_t)
            inter = jnp.exp(b_col + m_prev - m_t)
            q = q_ref[pl.ds(r0, L), hh * dk:(hh + 1) * dk]
            k = k_ref[pl.ds(r0, L), hh * dk:(hh + 1) * dk]
            v = v_ref[pl.ds(r0, L), hh * dv:(hh + 1) * dv]
            qk = lax.dot_general(q, k, (((1,), (1,)), ((), ())), preferred_element_type=F32) * w
            c_state = c_ref[hh]
            n_state = n_ref[hh]
            num = inter * jnp.dot(q, c_state.astype(BF16), preferred_element_type=F32) \
                + jnp.dot(qk.astype(BF16), v, preferred_element_type=F32)
            den = inter * jnp.sum(q.astype(F32) * n_state, axis=-1, keepdims=True) \
                + jnp.sum(qk, axis=-1, keepdims=True)
            hs = num / jnp.maximum(jnp.abs(den), jnp.exp(-m_t))
            m_new = m_t[L - 1:L, :]
            b_last = b_col[L - 1:L, :]
            decay = jnp.exp(b_last + m_prev - m_new)
            ws = jnp.exp(b_last - b_col + li_col - m_new)
            kw = k.astype(F32) * ws
            c_ref[hh] = decay * c_state + jnp.dot(kw.T.astype(BF16), v, preferred_element_type=F32)
            n_ref[hh] = decay * n_state + jnp.sum(kw, axis=0, keepdims=True)
            m_ref[hh] = m_new
            y = hs * lax.rsqrt(jnp.mean(hs * hs, axis=-1, keepdims=True) + NORM_EPS) \
                * hn_ref[:, hh * dv:(hh + 1) * dv]
            o_ref[pl.ds(r0, L), hh * dv:(hh + 1) * dv] = (
                og_ref[pl.ds(r0, L), hh * dv:(hh + 1) * dv].astype(F32) * y).astype(BF16)
        return carry

    lax.fori_loop(0, seq // L, chunk_body, 0)


def _mlstm_scan(q, k, v, og, gl, glt, head_norm, batch, seq, chunk, heads_per_step):
    t = q.shape[0]
    hps = heads_per_step
    dk, dv = MLSTM_QK_DIM, MLSTM_V_DIM
    ng = 2 * MLSTM_HEADS
    est = 2 * seq * hps * (2 * dk + 3 * dv) * 2 + 2 * seq * LANES * 4 + 2 * ng * seq * 4 \
        + hps * dk * dv * 4 + 16 * chunk * chunk * 4
    return pl.pallas_call(
        functools.partial(_mlstm_scan_kernel, chunk=chunk, heads_per_step=hps),
        grid=(batch, MLSTM_HEADS // hps),
        in_specs=[pl.BlockSpec((seq, hps * dk), lambda b, g: (b, g)),
                  pl.BlockSpec((seq, hps * dk), lambda b, g: (b, g)),
                  pl.BlockSpec((seq, hps * dv), lambda b, g: (b, g)),
                  pl.BlockSpec((seq, hps * dv), lambda b, g: (b, g)),
                  pl.BlockSpec((seq, LANES), lambda b, g: (b, 0)),
                  pl.BlockSpec((ng, seq), lambda b, g: (0, b)),
                  pl.BlockSpec((1, hps * dv), lambda b, g: (0, g))],
        out_specs=pl.BlockSpec((seq, hps * dv), lambda b, g: (b, g)),
        out_shape=jax.ShapeDtypeStruct((t, MLSTM_HEADS * dv), BF16),
        scratch_shapes=[pltpu.VMEM((hps, dk, dv), F32), pltpu.VMEM((hps, 1, dk), F32),
                        pltpu.VMEM((hps, 1, 1), F32)],
        compiler_params=pltpu.CompilerParams(
            dimension_semantics=("arbitrary", "arbitrary"), vmem_limit_bytes=_vmem_limit(est)),
        name="mlstm_scan",
    )(q, k, v, og, gl, glt, head_norm.reshape(1, -1))


def kernel(x, c, positions, mla_w_in, mla_q_norm, mla_kv_norm, mla_w_uq, mla_w_ukv, mla_w_o, mlstm_w_in, mlstm_b_gates, mlstm_head_norm, mlstm_w_out, moe_w_router, moe_b_router, moe_w_gu, moe_b_gu, moe_w_down, moe_b_down, ada_w, ada_b, ln_g, ln_b):
    batch, seq, d = x.shape
    assert d == D_MODEL and ada_w.shape[0] == DEPTH
    cfg = _tiles(batch, seq)
    t = batch * seq
    tm = cfg["tm_tok"]
    mod = _adaln(c, ada_w, ada_b, cfg["tn_ada"])
    lng = ln_g.reshape(2 * DEPTH, d)
    lnb = ln_b.reshape(2 * DEPTH, d)
    xt = x.reshape(t, d)
    for i in range(DEPTH):
        j = i // 2
        k_mix, k_ffn = 2 * i, 2 * i + 1
        if i % 2 == 0:
            qn, qr, kn, v, kr = _mla_proj(xt, mod, k_mix, positions, mla_w_in[j], mla_q_norm[j],
                                          mla_kv_norm[j], mla_w_uq[j], mla_w_ukv[j], seq, tm)
            a = _attention(qn, qr, kn, kr, v, batch, seq, cfg["tq"])
            w_o = mla_w_o[j]
        else:
            q, k, v, og, gl, glt = _mlstm_proj(xt, mod, k_mix, mlstm_w_in[j], mlstm_b_gates[j], seq, tm)
            a = _mlstm_scan(q, k, v, og, gl, glt, mlstm_head_norm[j], batch, seq,
                            cfg["chunk"], cfg["heads_per_step"])
            w_o = mlstm_w_out[j]
        x1, h2, topi, topg = _post(a, xt, mod, k_mix, k_ffn, lng[k_mix], lnb[k_mix], w_o,
                                   moe_w_router[i], moe_b_router[i], seq, cfg["tm_post"])
        xt = _moe(h2, topi, topg, x1, mod, k_ffn, lng[k_ffn], lnb[k_ffn],
                  moe_w_gu, moe_b_gu, moe_w_down, moe_b_down, i, seq, cfg)
    return xt.reshape(batch, seq, d)
```

```python
import functools

import jax
import jax.numpy as jnp
from jax import lax
from jax.experimental import pallas as pl
from jax.experimental.pallas import tpu as pltpu

F32 = jnp.float32
BF16 = jnp.bfloat16

D_MODEL = 2048
DEPTH = 2

MLA_HEADS = 16
QK_NOPE_DIM = 128
QK_ROPE_DIM = 64
V_HEAD_DIM = 128
Q_LORA_RANK = 512
KV_LORA_RANK = 512
ROPE_THETA = 10000.0

MLSTM_HEADS = 8
MLSTM_QK_DIM = D_MODEL // (2 * MLSTM_HEADS)
MLSTM_V_DIM = D_MODEL // MLSTM_HEADS
GATE_SOFTCAP = 15.0

N_EXPERTS = 32
TOP_K = 4
D_EXPERT = D_MODEL
SWIGLU_ALPHA = 1.702
SWIGLU_LIMIT = 7.0

DEEPNORM_ALPHA = (2 * DEPTH) ** 0.25
NORM_EPS = 1e-6

LANES = 128
V7X_VMEM_BYTES = 64 * 1024 * 1024
NEG_INF = float("-inf")


def _vmem_limit(estimate_bytes):
    return int(min(estimate_bytes + (12 << 20), V7X_VMEM_BYTES - (8 << 20)))


def _tiles(batch, seq):
    t = batch * seq
    cfg = dict(
        tm_tok=min(256, seq),
        tm_post=min(512, seq),
        tq=min(256, seq),
        chunk=min(128, seq),
        heads_per_step=4,
        tm_moe=min(512, t * TOP_K // 8),
        tf=1024,
        tc_gemm=512,
        tn=1024,
        tc=min(128, seq),
        tn_ada=768,
    )
    return cfg


def _adaln_kernel(c_ref, w_ref, b_ref, o_ref):
    c = c_ref[...]
    cond = c * jax.nn.sigmoid(c)
    o_ref[...] = jnp.dot(cond.astype(BF16), w_ref[...].astype(BF16),
                         preferred_element_type=F32) + b_ref[...]


def _adaln(c, ada_w, ada_b, tn):
    b, d = c.shape
    n_mod = ada_w.shape[0] * ada_w.shape[1]
    w = ada_w.reshape(n_mod, d, 3 * d)
    bias = ada_b.reshape(n_mod, 1, 3 * d)
    out = pl.pallas_call(
        _adaln_kernel,
        grid=(n_mod, 3 * d // tn),
        in_specs=[
            pl.BlockSpec((b, d), lambda i, j: (0, 0)),
            pl.BlockSpec((None, d, tn), lambda i, j: (i, 0, j)),
            pl.BlockSpec((None, 1, tn), lambda i, j: (i, 0, j)),
        ],
        out_specs=pl.BlockSpec((None, b, tn), lambda i, j: (i, 0, j)),
        out_shape=jax.ShapeDtypeStruct((n_mod, b, 3 * d), F32),
        compiler_params=pltpu.CompilerParams(
            dimension_semantics=("arbitrary", "arbitrary"),
            vmem_limit_bytes=_vmem_limit(2 * d * tn * 4 + d * tn * 2)),
        name="adaln",
    )(c, w, bias)
    return out.reshape(n_mod, b, 1, 3 * d)


def _mod_spec(k, part, tiles_per_batch):
    return pl.BlockSpec((None, None, 1, D_MODEL),
                        lambda i, *_: (k, i // tiles_per_batch, 0, part))


def _row_spec(k):
    return pl.BlockSpec((None, 1, D_MODEL), lambda i, *_: (k, 0, 0))


def _resident(shape):
    nd = len(shape)
    return pl.BlockSpec(shape, lambda *_: (0,) * nd, pipeline_mode=pl.Buffered(1))


def _mla_proj_kernel(x_ref, shift_ref, scale_ref, pos_ref, rope_ref, win_ref, qnorm_ref, kvnorm_ref,
                     wqn_ref, wqr_ref, wqs_ref, wkn_ref, wv_ref,
                     qn_ref, qr_ref, kn_ref, v_ref, kr_ref):
    h = x_ref[...] * (1.0 + scale_ref[...]) + shift_ref[...]
    lat = jnp.dot(h.astype(BF16), win_ref[...], preferred_element_type=F32)
    cq = lat[:, :Q_LORA_RANK]
    ckv = lat[:, Q_LORA_RANK:Q_LORA_RANK + KV_LORA_RANK]
    kr = lat[:, Q_LORA_RANK + KV_LORA_RANK:Q_LORA_RANK + KV_LORA_RANK + LANES]
    kr_sw = lat[:, Q_LORA_RANK + KV_LORA_RANK + LANES:]
    cq = cq * lax.rsqrt(jnp.mean(cq * cq, axis=-1, keepdims=True) + NORM_EPS) * qnorm_ref[...]
    ckv = ckv * lax.rsqrt(jnp.mean(ckv * ckv, axis=-1, keepdims=True) + NORM_EPS) * kvnorm_ref[...]
    cq = cq.astype(BF16)
    ckv = ckv.astype(BF16)

    ang = pos_ref[...].astype(F32) * rope_ref[0:1, :]
    cos_t = jnp.cos(ang) * rope_ref[1:2, :]
    sin_t = jnp.sin(ang) * rope_ref[2:3, :]

    qn_ref[...] = jnp.dot(cq, wqn_ref[...], preferred_element_type=F32).astype(BF16)
    q_rope = jnp.dot(cq, wqr_ref[...], preferred_element_type=F32)
    q_swap = jnp.dot(cq, wqs_ref[...], preferred_element_type=F32)
    cos_h = jnp.tile(cos_t, (1, MLA_HEADS))
    sin_h = jnp.tile(sin_t, (1, MLA_HEADS))
    qr_ref[...] = (q_rope * cos_h + q_swap * sin_h).astype(BF16)
    kn_ref[...] = jnp.dot(ckv, wkn_ref[...], preferred_element_type=F32).astype(BF16)
    v_ref[...] = jnp.dot(ckv, wv_ref[...], preferred_element_type=F32).astype(BF16)
    kr_ref[...] = (kr * cos_t + kr_sw * sin_t).astype(BF16)


def _rope_table():
    half = QK_ROPE_DIM // 2
    inv_freq = ROPE_THETA ** (-jnp.arange(0, QK_ROPE_DIM, 2, dtype=F32) / QK_ROPE_DIM)
    zeros = jnp.zeros((LANES - QK_ROPE_DIM,), F32)
    freq_row = jnp.concatenate([inv_freq, inv_freq, zeros])
    cos_mask = jnp.concatenate([jnp.ones((QK_ROPE_DIM,), F32), zeros])
    sin_sign = jnp.concatenate([-jnp.ones((half,), F32), jnp.ones((half,), F32), zeros])
    pad = jnp.zeros((5, LANES), F32)
    return jnp.concatenate([jnp.stack([freq_row, cos_mask, sin_sign]), pad], axis=0)


def _mla_weights(w_in, w_uq, w_ukv):
    d = w_in.shape[0]
    h = MLA_HEADS
    half = QK_ROPE_DIM // 2
    lat = Q_LORA_RANK + KV_LORA_RANK
    kr = w_in[:, lat:]
    zpad = jnp.zeros((d, LANES - QK_ROPE_DIM), w_in.dtype)
    win = jnp.concatenate([w_in[:, :lat], kr, zpad, kr[:, half:], kr[:, :half], zpad], axis=1)
    wq = w_uq.reshape(Q_LORA_RANK, h, QK_NOPE_DIM + QK_ROPE_DIM)
    wqn = wq[:, :, :QK_NOPE_DIM].reshape(Q_LORA_RANK, h * QK_NOPE_DIM)
    rope = wq[:, :, QK_NOPE_DIM:]
    zq = jnp.zeros((Q_LORA_RANK, h, LANES - QK_ROPE_DIM), w_uq.dtype)
    wqr = jnp.concatenate([rope, zq], axis=-1).reshape(Q_LORA_RANK, h * LANES)
    wqs = jnp.concatenate([rope[..., half:], rope[..., :half], zq], axis=-1).reshape(Q_LORA_RANK, h * LANES)
    wkv = w_ukv.reshape(KV_LORA_RANK, h, QK_NOPE_DIM + V_HEAD_DIM)
    wkn = wkv[:, :, :QK_NOPE_DIM].reshape(KV_LORA_RANK, h * QK_NOPE_DIM)
    wv = wkv[:, :, QK_NOPE_DIM:].reshape(KV_LORA_RANK, h * V_HEAD_DIM)
    return tuple(a.astype(BF16) for a in (win, wqn, wqr, wqs, wkn, wv))


def _mla_proj(x2d, mod, k_mod, positions, w_in, q_norm, kv_norm, w_uq, w_ukv, seq, tm):
    t, d = x2d.shape
    tiles_per_batch = seq // tm
    win, wqn, wqr, wqs, wkn, wv = _mla_weights(w_in, w_uq, w_ukv)
    hn = MLA_HEADS * LANES
    tok = lambda n: pl.BlockSpec((tm, n), lambda i: (i, 0))
    weights_bytes = 2 * (win.size + wqn.size + wqr.size + wqs.size + wkn.size + wv.size)
    est = weights_bytes + 2 * tm * d * 4 + 2 * (4 * tm * hn * 2 + tm * LANES * 2) + 8 * tm * hn * 4
    outs = pl.pallas_call(
        _mla_proj_kernel,
        grid=(t // tm,),
        in_specs=[
            tok(d),
            _mod_spec(k_mod, 0, tiles_per_batch),
            _mod_spec(k_mod, 1, tiles_per_batch),
            pl.BlockSpec((tm, 1), lambda i: (i, 0)),
            _resident((8, LANES)),
            _resident(win.shape),
            _resident((1, Q_LORA_RANK)),
            _resident((1, KV_LORA_RANK)),
            _resident(wqn.shape), _resident(wqr.shape), _resident(wqs.shape),
            _resident(wkn.shape), _resident(wv.shape),
        ],
        out_specs=[tok(hn), tok(hn), tok(hn), tok(hn), tok(LANES)],
        out_shape=[jax.ShapeDtypeStruct((t, hn), BF16)] * 4 + [jax.ShapeDtypeStruct((t, LANES), BF16)],
        compiler_params=pltpu.CompilerParams(
            dimension_semantics=("arbitrary",), vmem_limit_bytes=_vmem_limit(est)),
        name="mla_proj",
    )(x2d, mod, mod, positions.reshape(t, 1), _rope_table(), win,
      q_norm.reshape(1, -1), kv_norm.reshape(1, -1), wqn, wqr, wqs, wkn, wv)
    return outs


def _attn_kernel(qn_ref, qr_ref, kn_ref, kr_ref, v_ref, o_ref, s_ref, *, tq, scale):
    seq = qn_ref.shape[0]
    row = lax.broadcasted_iota(jnp.int32, (tq, tq), 0)
    col = lax.broadcasted_iota(jnp.int32, (tq, tq), 1)
    causal = col <= row
    exp2_scale = scale * 1.4426950408889634

    def fold_lanes(a):
        return [a[:, g * LANES:(g + 1) * LANES] for g in range(tq // LANES)]

    for qi in range(seq // tq):
        rows = slice(qi * tq, (qi + 1) * tq)
        q = jnp.concatenate([qn_ref[rows, :], qr_ref[rows, :]], axis=-1)
        m_acc = jnp.full((tq, LANES), NEG_INF, F32)
        for j in range(qi + 1):
            cols = slice(j * tq, (j + 1) * tq)
            k = jnp.concatenate([kn_ref[cols, :], kr_ref[cols, :]], axis=-1)
            s = lax.dot_general(q, k, (((1,), (1,)), ((), ())), preferred_element_type=F32)
            if j == qi:
                s = jnp.where(causal, s, NEG_INF)
            s_ref[:, cols] = s
            for part in fold_lanes(s):
                m_acc = jnp.maximum(m_acc, part)
        m = jnp.max(m_acc, axis=-1, keepdims=True)
        l_acc = jnp.zeros((tq, LANES), F32)
        acc = jnp.zeros((tq, V_HEAD_DIM), F32)
        for j in range(qi + 1):
            cols = slice(j * tq, (j + 1) * tq)
            p = jnp.exp2((s_ref[:, cols] - m) * exp2_scale)
            for part in fold_lanes(p):
                l_acc = l_acc + part
            acc = acc + jnp.dot(p.astype(BF16), v_ref[cols, :], preferred_element_type=F32)
        l = jnp.sum(l_acc, axis=-1, keepdims=True)
        o_ref[rows, :] = (acc / l).astype(BF16)


def _attention(qn, qr, kn, kr, v, batch, seq, tq):
    t = qn.shape[0]
    head = lambda: pl.BlockSpec((seq, LANES), lambda b, h: (b, h))
    scale = (QK_NOPE_DIM + QK_ROPE_DIM) ** -0.5
    est = 2 * 6 * seq * LANES * 2 + tq * seq * 4 + 6 * tq * tq * 4
    return pl.pallas_call(
        functools.partial(_attn_kernel, tq=tq, scale=scale),
        grid=(batch, MLA_HEADS),
        in_specs=[head(), head(), head(), pl.BlockSpec((seq, LANES), lambda b, h: (b, 0)), head()],
        out_specs=head(),
        out_shape=jax.ShapeDtypeStruct((t, MLA_HEADS * V_HEAD_DIM), BF16),
        scratch_shapes=[pltpu.VMEM((tq, seq), F32)],
        compiler_params=pltpu.CompilerParams(
            dimension_semantics=("arbitrary", "arbitrary"), vmem_limit_bytes=_vmem_limit(est)),
        name="mla_attention",
    )(qn, qr, kn, kr, v)


SLABS = D_MODEL // LANES


def _store_slabs(ref, val, first_slab=0, first_row=0):
    rows = val.shape[0]
    for s in range(val.shape[1] // LANES):
        ref[pl.ds(first_row * SLABS + first_slab + s, rows, stride=SLABS), :] = val[:, s * LANES:(s + 1) * LANES]


def _layer_norm(z, g, b):
    mu = jnp.mean(z, axis=-1, keepdims=True)
    zc = z - mu
    var = jnp.mean(zc * zc, axis=-1, keepdims=True)
    return zc * lax.rsqrt(var + NORM_EPS) * g + b


def _post_kernel(a_ref, x_ref, gate_ref, lng_ref, lnb_ref, shift_ref, scale_ref, wo_ref, wr_ref, br_ref,
                 x1_ref, h2_ref, topi_ref, topg_ref, *, sub):
    starts = range(0, a_ref.shape[0], sub)
    ys = [jnp.dot(a_ref[r0:r0 + sub, :], wo_ref[...], preferred_element_type=F32) for r0 in starts]
    for y, r0 in zip(ys, starts):
        _post_rows(slice(r0, r0 + sub), r0, y, x_ref, gate_ref, lng_ref, lnb_ref, shift_ref, scale_ref,
                   wr_ref, br_ref, x1_ref, h2_ref, topi_ref, topg_ref)


def _post_rows(rows, r0, y, x_ref, gate_ref, lng_ref, lnb_ref, shift_ref, scale_ref, wr_ref,
               br_ref, x1_ref, h2_ref, topi_ref, topg_ref):
    z = DEEPNORM_ALPHA * x_ref[rows, :] + gate_ref[...] * y
    x1 = _layer_norm(z, lng_ref[...], lnb_ref[...])
    x1_ref[rows, :] = x1
    h2 = x1 * (1.0 + scale_ref[...]) + shift_ref[...]
    _store_slabs(h2_ref, h2, first_row=r0)
    h_hi = h2.astype(BF16)
    h_lo = (h2 - h_hi.astype(F32)).astype(BF16)
    t_hi = jnp.dot(h_hi, wr_ref[...], preferred_element_type=F32)
    t_lo = jnp.dot(h_lo, wr_ref[:, :LANES], preferred_element_type=F32)
    logits = t_hi[:, :LANES] + (t_hi[:, LANES:] + t_lo) + br_ref[...]
    tm = logits.shape[0]
    lane = lax.broadcasted_iota(jnp.int32, (tm, LANES), 1)
    vals, idxs = [], []
    for _ in range(TOP_K):
        mk = jnp.max(logits, axis=-1, keepdims=True)
        ik = jnp.min(jnp.where(logits == mk, lane, LANES), axis=-1, keepdims=True)
        vals.append(mk)
        idxs.append(ik)
        logits = jnp.where(lane == ik, NEG_INF, logits)
    exps = [jnp.exp(v - vals[0]) for v in vals]
    den = exps[0]
    for e in exps[1:]:
        den = den + e
    topi = jnp.zeros((tm, LANES), jnp.int32)
    topg = jnp.zeros((tm, LANES), F32)
    for k in range(TOP_K):
        topi = jnp.where(lane == k, idxs[k], topi)
        topg = jnp.where(lane == k, exps[k] / den, topg)
    topi_ref[rows, :] = topi
    topg_ref[rows, :] = topg


def _post(a, x2d, mod, k_mix, k_ffn, ln_g, ln_b, w_o, w_router, b_router, seq, tm):
    t, d = x2d.shape
    tiles_per_batch = seq // tm
    wr = jnp.concatenate([w_router, jnp.zeros((d, LANES - N_EXPERTS), F32)], axis=1)
    wr_hi = wr.astype(BF16)
    wr_lo = (wr - wr_hi.astype(F32)).astype(BF16)
    wr = jnp.concatenate([wr_hi, wr_lo], axis=1)
    br = jnp.concatenate([b_router, jnp.full((LANES - N_EXPERTS,), NEG_INF, F32)]).reshape(1, LANES)
    tok = lambda n: pl.BlockSpec((tm, n), lambda i: (i, 0))
    sub = min(256, tm)
    est = d * d * 2 + d * LANES * 4 + 2 * tm * d * (2 + 4 + 4 + 4) + 6 * sub * d * 4
    return pl.pallas_call(
        functools.partial(_post_kernel, sub=sub),
        grid=(t // tm,),
        in_specs=[
            tok(d), tok(d),
            _mod_spec(k_mix, 2, tiles_per_batch),
            _resident((1, d)), _resident((1, d)),
            _mod_spec(k_ffn, 0, tiles_per_batch),
            _mod_spec(k_ffn, 1, tiles_per_batch),
            _resident((d, d)), _resident((d, 2 * LANES)), _resident((1, LANES)),
        ],
        out_specs=[tok(d), pl.BlockSpec((tm * SLABS, LANES), lambda i: (i, 0)), tok(LANES), tok(LANES)],
        out_shape=[jax.ShapeDtypeStruct((t, d), F32), jax.ShapeDtypeStruct((t * SLABS, LANES), F32),
                   jax.ShapeDtypeStruct((t, LANES), jnp.int32), jax.ShapeDtypeStruct((t, LANES), F32)],
        compiler_params=pltpu.CompilerParams(
            dimension_semantics=("arbitrary",), vmem_limit_bytes=_vmem_limit(est)),
        name="mixer_out_ln_router",
    )(a, x2d, mod, ln_g.reshape(1, d), ln_b.reshape(1, d), mod, mod, w_o.astype(BF16), wr, br)


def _route(topi, tm_moe):
    t = topi.shape[0]
    m = t * TOP_K
    e_tk = topi[:, :TOP_K]
    a = jnp.sum((e_tk[:, :, None] == jnp.arange(N_EXPERTS, dtype=jnp.int32)).astype(F32), axis=1)
    blk = min(256, t)
    ab = a.reshape(t // blk, blk, N_EXPERTS)
    tri = jnp.tril(jnp.ones((blk, blk), F32))
    within = jnp.einsum("ts,bse->bte", tri, ab)
    btot = within[:, -1, :]
    boff = jnp.cumsum(btot, axis=0) - btot
    excl = (within + boff[:, None, :] - ab).reshape(t, N_EXPERTS)
    counts = (boff[-1] + btot[-1]).astype(jnp.int32)
    rank = jnp.take_along_axis(excl, e_tk, axis=1).astype(jnp.int32)
    padded = ((counts + tm_moe - 1) // tm_moe) * tm_moe
    pends = jnp.cumsum(padded)
    pstarts = pends - padded
    dest = (pstarts[e_tk] + rank).astype(jnp.int32).reshape(m)
    n_tiles = m // tm_moe + N_EXPERTS
    row_tok = jnp.zeros((n_tiles * tm_moe,), jnp.int32).at[dest].set(
        jnp.arange(m, dtype=jnp.int32) // TOP_K, unique_indices=True)
    tile_start = (pstarts // tm_moe).astype(jnp.int32)
    tile_count = (padded // tm_moe).astype(jnp.int32)
    n_used = (pends[-1] // tm_moe).astype(jnp.int32).reshape(1)
    return dest, row_tok, tile_start, tile_count, n_used


def _gather_issue(idx_ref, src_hbm, buf, sem, slot, n_rows):
    def body(i, carry):
        for u in range(2):
            r = 2 * i + u
            pltpu.make_async_copy(src_hbm.at[idx_ref[0, 0, r]], buf.at[slot, :, r, :],
                                  sem.at[slot]).start(priority=u)
        return carry
    lax.fori_loop(0, n_rows // 2, body, 0, unroll=4)


def _gather_wait(buf, sem, slot):
    pltpu.make_async_copy(buf.at[slot], buf.at[slot], sem.at[slot]).wait()


def _dispatch_kernel(ns_ref, idx_ref, nxt_ref, src_hbm, o_ref, buf, sem, *, n_rows):
    i = pl.program_id(0)
    n_active = ns_ref[0]
    slot = lax.rem(i, 2)

    @pl.when(jnp.logical_and(i == 0, n_active > 0))
    def _():
        _gather_issue(idx_ref, src_hbm, buf, sem, 0, n_rows)

    @pl.when(i + 1 < n_active)
    def _():
        _gather_issue(nxt_ref, src_hbm, buf, sem, 1 - slot, n_rows)

    @pl.when(i < n_active)
    def _():
        _gather_wait(buf, sem, slot)
        o_ref[...] = buf[slot].astype(BF16)

    @pl.when(i >= n_active)
    def _():
        o_ref[...] = jnp.zeros(o_ref.shape, o_ref.dtype)


def _dispatch(h2_slabs, row_tok, n_used, r):
    t, slabs, _ = h2_slabs.shape
    n = row_tok.shape[0] // r
    idx = row_tok.reshape(n, 1, r)
    smem = lambda f: pl.BlockSpec((1, 1, r), f, memory_space=pltpu.SMEM)
    return pl.pallas_call(
        functools.partial(_dispatch_kernel, n_rows=r),
        grid_spec=pltpu.PrefetchScalarGridSpec(
            num_scalar_prefetch=1,
            grid=(n,),
            in_specs=[smem(lambda i, ns: (i, 0, 0)),
                      smem(lambda i, ns: (jnp.minimum(i + 1, n - 1), 0, 0)),
                      pl.BlockSpec(memory_space=pl.ANY)],
            out_specs=pl.BlockSpec((None, slabs, r, LANES), lambda i, ns: (i, 0, 0, 0)),
            scratch_shapes=[pltpu.VMEM((2, slabs, r, LANES), F32), pltpu.SemaphoreType.DMA((2,))]),
        out_shape=jax.ShapeDtypeStruct((n, slabs, r, LANES), BF16),
        compiler_params=pltpu.CompilerParams(
            dimension_semantics=("arbitrary",),
            vmem_limit_bytes=_vmem_limit(2 * r * slabs * LANES * (4 + 2))),
        name="moe_dispatch_gather",
    )(n_used, idx, idx, h2_slabs)


def _expert_weights(step, n_steps, copies_for, stage, w_bf):
    @pl.when(step == 0)
    def _():
        for c in copies_for(step):
            c.start()

    for c in copies_for(step):
        c.wait()
    w_bf[...] = stage[...].astype(BF16)

    @pl.when(step + 1 < n_steps)
    def _():
        for c in copies_for(step + 1):
            c.start()


def _tile_loop(n_tiles, in_copies, out_copy, compute):
    def start_in(k, slot):
        for c in in_copies(k, slot):
            c.start()

    @pl.when(n_tiles > 0)
    def _():
        start_in(0, 0)

    def tile(k, slot):
        for c in in_copies(k, slot):
            c.wait()

        @pl.when(k + 1 < n_tiles)
        def _():
            start_in(k + 1, 1 - slot)

        @pl.when(k >= 2)
        def _():
            out_copy(k - 2, slot).wait()

        compute(slot)
        out_copy(k, slot).start()

    def pair(p, carry):
        tile(2 * p, 0)

        @pl.when(2 * p + 1 < n_tiles)
        def _():
            tile(2 * p + 1, 1)
        return carry

    lax.fori_loop(0, (n_tiles + 1) // 2, pair, 0)

    for slot in (0, 1):
        @pl.when(jnp.logical_and(n_tiles >= 2, lax.rem(n_tiles, 2) == slot))
        def _():
            out_copy(n_tiles - 2, slot).wait()

        @pl.when(jnp.logical_and(n_tiles >= 1, lax.rem(n_tiles + 1, 2) == slot))
        def _():
            out_copy(n_tiles - 1, slot).wait()


def _zero_tail_tiles(first, n_total, zero_src, dst_for, sem):
    def body(t, carry):
        cp = pltpu.make_async_copy(zero_src, dst_for(t), sem)
        cp.start()
        cp.wait()
        return carry
    lax.fori_loop(first, n_total, body, 0)


def _gemm1_kernel(ts_ref, nt_ref, nu_ref, bg_ref, bu_ref, w_hbm, x_hbm, o_hbm,
                  stage, w_bf, xbuf, obuf, wsem, xsem, osem, *, layer, tf, tc):
    j = pl.program_id(0)
    e = pl.program_id(1)
    n_e = pl.num_programs(1)
    f = D_EXPERT

    def weight_copies(step):
        jj = step // n_e
        ee = step - jj * n_e
        return [pltpu.make_async_copy(
            w_hbm.at[layer, ee, :, pl.ds(pl.multiple_of(half * f + jj * tf, tf), tf)],
            stage.at[half], wsem.at[half]) for half in range(2)]

    _expert_weights(j * n_e + e, pl.num_programs(0) * n_e, weight_copies, stage, w_bf)

    t0 = ts_ref[e]

    def x_copies(k, slot):
        return [pltpu.make_async_copy(x_hbm.at[t0 + k], xbuf.at[slot], xsem.at[slot])]

    def o_copy(k, slot):
        return pltpu.make_async_copy(obuf.at[slot], o_hbm.at[j, t0 + k], osem.at[slot])

    def compute(slot):
        x = jnp.concatenate([xbuf[slot, s] for s in range(xbuf.shape[1])], axis=-1)
        for c in range(tf // tc):
            cols = slice(c * tc, (c + 1) * tc)
            g = jnp.dot(x, w_bf[0, :, cols], preferred_element_type=F32) + bg_ref[:, cols]
            u = jnp.dot(x, w_bf[1, :, cols], preferred_element_type=F32) + bu_ref[:, cols]
            g = jnp.minimum(g, SWIGLU_LIMIT)
            u = jnp.clip(u, -SWIGLU_LIMIT, SWIGLU_LIMIT)
            obuf[slot, :, cols] = (g * jax.nn.sigmoid(SWIGLU_ALPHA * g) * (u + 1.0)).astype(BF16)

    _tile_loop(nt_ref[e], x_copies, o_copy, compute)

    @pl.when(e == n_e - 1)
    def _():
        obuf[0] = jnp.zeros(obuf.shape[1:], obuf.dtype)
        _zero_tail_tiles(nu_ref[0], o_hbm.shape[1], obuf.at[0], lambda t: o_hbm.at[j, t], osem.at[0])


def _gemm1(x_rows, tile_start, tile_count, n_used, w_gu, b_gu, layer, tf, tc):
    n_tiles, slabs, tm, _ = x_rows.shape
    d = slabs * LANES
    f = D_EXPERT
    nf = f // tf
    bias = b_gu.reshape(b_gu.shape[0], N_EXPERTS, 1, 2 * f)
    bspec = lambda off: pl.BlockSpec((None, None, 1, tf), lambda j, e, ts, nt, nu: (layer, e, 0, j + off))
    any_spec = pl.BlockSpec(memory_space=pl.ANY)
    est = 2 * d * tf * (4 + 2) + 2 * tm * d * 2 + 2 * tm * tf * 2 + 6 * tm * tc * 4
    return pl.pallas_call(
        functools.partial(_gemm1_kernel, layer=layer, tf=tf, tc=tc),
        grid_spec=pltpu.PrefetchScalarGridSpec(
            num_scalar_prefetch=3,
            grid=(nf, N_EXPERTS),
            in_specs=[bspec(0), bspec(nf), any_spec, any_spec],
            out_specs=any_spec,
            scratch_shapes=[pltpu.VMEM((2, d, tf), F32), pltpu.VMEM((2, d, tf), BF16),
                            pltpu.VMEM((2, slabs, tm, LANES), BF16), pltpu.VMEM((2, tm, tf), BF16),
                            pltpu.SemaphoreType.DMA((2,)), pltpu.SemaphoreType.DMA((2,)),
                            pltpu.SemaphoreType.DMA((2,))]),
        out_shape=jax.ShapeDtypeStruct((nf, n_tiles, tm, tf), BF16),
        compiler_params=pltpu.CompilerParams(
            dimension_semantics=("arbitrary", "arbitrary"), vmem_limit_bytes=_vmem_limit(est)),
        name="moe_gate_up_swiglu",
    )(tile_start, tile_count, n_used, bias, bias, w_gu, x_rows)


def _gemm2_kernel(ts_ref, nt_ref, nu_ref, b_ref, w_hbm, a_hbm, o_hbm,
                  stage, w_bf, abuf, obuf, wsem, asem, osem, *, layer, tn):
    e = pl.program_id(0)
    nf, tm = abuf.shape[1], abuf.shape[2]
    tile_rows = tm * SLABS

    def weight_copies(step):
        return [pltpu.make_async_copy(w_hbm.at[layer, step], stage, wsem.at[0])]

    _expert_weights(e, pl.num_programs(0), weight_copies, stage, w_bf)

    t0 = ts_ref[e]

    def a_copies(k, slot):
        return [pltpu.make_async_copy(a_hbm.at[c, t0 + k], abuf.at[slot, c], asem.at[slot, c])
                for c in range(nf)]

    def o_tile(t):
        return o_hbm.at[pl.ds(pl.multiple_of(t * tile_rows, tile_rows), tile_rows), :]

    def o_copy(k, slot):
        return pltpu.make_async_copy(obuf.at[pl.ds(slot * tile_rows, tile_rows), :],
                                     o_tile(t0 + k), osem.at[slot])

    def compute(slot):
        a = jnp.concatenate([abuf[slot, c] for c in range(nf)], axis=-1)
        for c in range(w_bf.shape[1] // tn):
            cols = slice(c * tn, (c + 1) * tn)
            res = jnp.dot(a, w_bf[:, cols], preferred_element_type=F32) + b_ref[:, cols]
            _store_slabs(obuf, res, first_slab=c * tn // LANES, first_row=slot * tm)

    _tile_loop(nt_ref[e], a_copies, o_copy, compute)

    @pl.when(e == pl.num_programs(0) - 1)
    def _():
        obuf[0:tile_rows, :] = jnp.zeros((tile_rows, LANES), obuf.dtype)
        _zero_tail_tiles(nu_ref[0], o_hbm.shape[0] // tile_rows, obuf.at[0:tile_rows, :], o_tile, osem.at[0])


def _gemm2(act, tile_start, tile_count, n_used, w_down, b_down, layer, tn):
    nf, n_tiles, tm, tf = act.shape
    f = nf * tf
    d = D_MODEL
    bias = b_down.reshape(b_down.shape[0], N_EXPERTS, 1, d)
    any_spec = pl.BlockSpec(memory_space=pl.ANY)
    est = f * d * (4 + 2) + 2 * tm * f * 2 + 2 * tm * d * 4 + 3 * tm * tn * 4
    return pl.pallas_call(
        functools.partial(_gemm2_kernel, layer=layer, tn=tn),
        grid_spec=pltpu.PrefetchScalarGridSpec(
            num_scalar_prefetch=3,
            grid=(N_EXPERTS,),
            in_specs=[pl.BlockSpec((None, None, 1, d), lambda e, ts, nt, nu: (layer, e, 0, 0)),
                      any_spec, any_spec],
            out_specs=any_spec,
            scratch_shapes=[pltpu.VMEM((f, d), F32), pltpu.VMEM((f, d), BF16),
                            pltpu.VMEM((2, nf, tm, tf), BF16), pltpu.VMEM((2 * tm * SLABS, LANES), F32),
                            pltpu.SemaphoreType.DMA((1,)), pltpu.SemaphoreType.DMA((2, nf)),
                            pltpu.SemaphoreType.DMA((2,))]),
        out_shape=jax.ShapeDtypeStruct((n_tiles * tm * SLABS, LANES), F32),
        compiler_params=pltpu.CompilerParams(
            dimension_semantics=("arbitrary",), vmem_limit_bytes=_vmem_limit(est)),
        name="moe_down",
    )(tile_start, tile_count, n_used, bias, w_down, act)


def _combine_kernel(idx_ref, nxt_ref, rows_hbm, g_ref, x_ref, gate_ref, lng_ref, lnb_ref, o_ref,
                    buf, sem, *, tc):
    i = pl.program_id(0)
    n = pl.num_programs(0)
    slot = lax.rem(i, 2)
    n_rows = TOP_K * tc

    @pl.when(i == 0)
    def _():
        _gather_issue(idx_ref, rows_hbm, buf, sem, 0, n_rows)

    @pl.when(i + 1 < n)
    def _():
        _gather_issue(nxt_ref, rows_hbm, buf, sem, 1 - slot, n_rows)

    _gather_wait(buf, sem, slot)
    g = g_ref[...]
    gk = [jnp.broadcast_to(g[:, k:k + 1], (tc, LANES)) for k in range(TOP_K)]
    parts = []
    for s in range(buf.shape[1]):
        acc = gk[0] * buf[slot, s, pl.ds(0, tc), :]
        for k in range(1, TOP_K):
            acc = acc + gk[k] * buf[slot, s, pl.ds(k * tc, tc), :]
        parts.append(acc)
    y = jnp.concatenate(parts, axis=-1)
    z = DEEPNORM_ALPHA * x_ref[...] + gate_ref[...] * y
    o_ref[...] = _layer_norm(z, lng_ref[...], lnb_ref[...])


def _combine(out_rows, dest, topg, x1, mod, k_ffn, ln_g, ln_b, seq, tc):
    t, d = x1.shape
    n = t // tc
    tiles_per_batch = seq // tc
    idx = dest.reshape(n, tc, TOP_K).transpose(0, 2, 1).reshape(n, 1, TOP_K * tc)
    smem = lambda f: pl.BlockSpec((1, 1, TOP_K * tc), f, memory_space=pltpu.SMEM)
    tok = lambda w: pl.BlockSpec((tc, w), lambda i: (i, 0))
    return pl.pallas_call(
        functools.partial(_combine_kernel, tc=tc),
        grid=(n,),
        in_specs=[smem(lambda i: (i, 0, 0)),
                  smem(lambda i: (jnp.minimum(i + 1, n - 1), 0, 0)),
                  pl.BlockSpec(memory_space=pl.ANY),
                  tok(LANES), tok(d),
                  _mod_spec(k_ffn, 2, tiles_per_batch),
                  _resident((1, d)), _resident((1, d))],
        out_specs=tok(d),
        out_shape=jax.ShapeDtypeStruct((t, d), F32),
        scratch_shapes=[pltpu.VMEM((2, d // LANES, TOP_K * tc, LANES), F32), pltpu.SemaphoreType.DMA((2,))],
        compiler_params=pltpu.CompilerParams(
            dimension_semantics=("arbitrary",),
            vmem_limit_bytes=_vmem_limit(2 * TOP_K * tc * d * 4 + 6 * tc * d * 4)),
        name="moe_combine_ln",
    )(idx, idx, out_rows, topg, x1, mod, ln_g.reshape(1, d), ln_b.reshape(1, d))


def _moe(h2, topi, topg, x1, mod, k_ffn, ln_g, ln_b, w_gu, b_gu, w_down, b_down, layer, seq, cfg):
    dest, row_tok, tile_start, tile_count, n_used = _route(topi, cfg["tm_moe"])
    x_rows = _dispatch(h2.reshape(-1, SLABS, LANES), row_tok, n_used, cfg["tm_moe"])
    act = _gemm1(x_rows, tile_start, tile_count, n_used, w_gu, b_gu, layer, cfg["tf"], cfg["tc_gemm"])
    out_rows = _gemm2(act, tile_start, tile_count, n_used, w_down, b_down, layer, cfg["tn"])
    out_rows = out_rows.reshape(-1, SLABS, LANES)
    return _combine(out_rows, dest, topg, x1, mod, k_ffn, ln_g, ln_b, seq, cfg["tc"])


def _softcap(g):
    return GATE_SOFTCAP * jnp.tanh(g / GATE_SOFTCAP)


def _log_sigmoid(x):
    return jnp.minimum(x, 0.0) - jnp.log(1.0 + jnp.exp(-jnp.abs(x)))


def _mlstm_proj_kernel(x_ref, shift_ref, scale_ref, wq_ref, wk_ref, wv_ref, wo_ref, wg_ref, wgt_ref,
                       bg_ref, bgt_ref, q_ref, k_ref, v_ref, og_ref, gl_ref, glt_ref):
    h = (x_ref[...] * (1.0 + scale_ref[...]) + shift_ref[...]).astype(BF16)
    q_ref[...] = (jnp.dot(h, wq_ref[...], preferred_element_type=F32) * MLSTM_QK_DIM ** -0.5).astype(BF16)
    k_ref[...] = jnp.dot(h, wk_ref[...], preferred_element_type=F32).astype(BF16)
    v_ref[...] = jnp.dot(h, wv_ref[...], preferred_element_type=F32).astype(BF16)
    og_ref[...] = jax.nn.sigmoid(jnp.dot(h, wo_ref[...], preferred_element_type=F32)).astype(BF16)
    g = _softcap(jnp.dot(h, wg_ref[...], preferred_element_type=F32) + bg_ref[...])
    lane = lax.broadcasted_iota(jnp.int32, g.shape, 1)
    gl_ref[...] = jnp.where(lane < MLSTM_HEADS, g,
                            jnp.where(lane < 2 * MLSTM_HEADS, _log_sigmoid(g), 0.0))
    gt = _softcap(lax.dot_general(wgt_ref[...], h, (((1,), (1,)), ((), ())),
                                  preferred_element_type=F32) + bgt_ref[...])
    sub = lax.broadcasted_iota(jnp.int32, gt.shape, 0)
    glt_ref[...] = jnp.where(sub < MLSTM_HEADS, gt, _log_sigmoid(gt))


def _mlstm_proj(x2d, mod, k_mod, w_in, b_gates, seq, tm):
    t, d = x2d.shape
    tiles_per_batch = seq // tm
    hk = MLSTM_HEADS * MLSTM_QK_DIM
    hv = MLSTM_HEADS * MLSTM_V_DIM
    ng = 2 * MLSTM_HEADS
    o1, o2, o3, o4 = hk, 2 * hk, 2 * hk + hv, 2 * hk + 2 * hv
    wq, wk, wv, wo = (w_in[:, a:b].astype(BF16) for a, b in ((0, o1), (o1, o2), (o2, o3), (o3, o4)))
    wg = jnp.concatenate([w_in[:, o4:], jnp.zeros((d, LANES - ng), F32)], axis=1).astype(BF16)
    wgt = w_in[:, o4:].T.astype(BF16)
    bg = jnp.concatenate([b_gates, jnp.zeros((LANES - ng,), F32)]).reshape(1, LANES)
    bgt = b_gates.reshape(ng, 1)
    tok = lambda n: pl.BlockSpec((tm, n), lambda i: (i, 0))
    est = 2 * d * (2 * hk + 2 * hv + LANES + ng) + 2 * tm * d * 4 + 2 * tm * (2 * hk + 2 * hv) * 2 \
        + 6 * tm * hv * 4
    return pl.pallas_call(
        _mlstm_proj_kernel,
        grid=(t // tm,),
        in_specs=[tok(d), _mod_spec(k_mod, 0, tiles_per_batch), _mod_spec(k_mod, 1, tiles_per_batch),
                  _resident(wq.shape), _resident(wk.shape), _resident(wv.shape), _resident(wo.shape),
                  _resident(wg.shape), _resident(wgt.shape), _resident((1, LANES)), _resident((ng, 1))],
        out_specs=[tok(hk), tok(hk), tok(hv), tok(hv), tok(LANES),
                   pl.BlockSpec((ng, tm), lambda i: (0, i))],
        out_shape=[jax.ShapeDtypeStruct((t, hk), BF16), jax.ShapeDtypeStruct((t, hk), BF16),
                   jax.ShapeDtypeStruct((t, hv), BF16), jax.ShapeDtypeStruct((t, hv), BF16),
                   jax.ShapeDtypeStruct((t, LANES), F32), jax.ShapeDtypeStruct((ng, t), F32)],
        compiler_params=pltpu.CompilerParams(
            dimension_semantics=("arbitrary",), vmem_limit_bytes=_vmem_limit(est)),
        name="mlstm_proj",
    )(x2d, mod, mod, wq, wk, wv, wo, wg, wgt, bg, bgt)


def _mlstm_scan_kernel(q_ref, k_ref, v_ref, og_ref, gl_ref, glt_ref, hn_ref, o_ref,
                       c_ref, n_ref, m_ref, *, chunk, heads_per_step):
    seq = q_ref.shape[0]
    L = chunk
    dk, dv = MLSTM_QK_DIM, MLSTM_V_DIM
    ng = 2 * MLSTM_HEADS
    head0 = pl.program_id(1) * heads_per_step
    hi = lax.Precision.HIGHEST

    c_ref[...] = jnp.zeros(c_ref.shape, F32)
    n_ref[...] = jnp.zeros(n_ref.shape, F32)
    m_ref[...] = jnp.zeros(m_ref.shape, F32)

    row = lax.broadcasted_iota(jnp.int32, (L, L), 0)
    col = lax.broadcasted_iota(jnp.int32, (L, L), 1)
    causal = col <= row
    tri = causal.astype(F32)
    tri_t = (row <= col).astype(F32)
    lane = lax.broadcasted_iota(jnp.int32, (L, LANES), 1)
    sub = lax.broadcasted_iota(jnp.int32, (ng, L), 0)

    def chunk_body(c, carry):
        r0 = pl.multiple_of(c * L, L)
        gl = gl_ref[pl.ds(r0, L), :]
        glt = glt_ref[:, pl.ds(r0, L)]
        for hh in range(heads_per_step):
            head = head0 + hh
            li_col = jnp.sum(jnp.where(lane == head, gl, 0.0), axis=-1, keepdims=True)
            lf_col = jnp.sum(jnp.where(lane == head + MLSTM_HEADS, gl, 0.0), axis=-1, keepdims=True)
            li_row = jnp.sum(jnp.where(sub == head, glt, 0.0), axis=0, keepdims=True)
            lf_row = jnp.sum(jnp.where(sub == head + MLSTM_HEADS, glt, 0.0), axis=0, keepdims=True)
            b_col = jnp.dot(tri, jnp.broadcast_to(lf_col, (L, LANES)), precision=hi,
                            preferred_element_type=F32)[:, 0:1]
            b_row = jnp.dot(jnp.broadcast_to(lf_row, (8, L)), tri_t, precision=hi,
                            preferred_element_type=F32)[0:1, :]
            m_prev = m_ref[hh]
            d = jnp.where(causal, b_col - b_row + li_row, NEG_INF)
            m_t = jnp.maximum(b_col + m_prev, jnp.max(d, axis=-1, keepdims=True))
            w = jnp.exp(d - m_t)
            inter = jnp.exp(b_col + m_prev - m_t)
            q = q_ref[pl.ds(r0, L), hh * dk:(hh + 1) * dk]
            k = k_ref[pl.ds(r0, L), hh * dk:(hh + 1) * dk]
            v = v_ref[pl.ds(r0, L), hh * dv:(hh + 1) * dv]
            qk = lax.dot_general(q, k, (((1,), (1,)), ((), ())), preferred_element_type=F32) * w
            c_state = c_ref[hh]
            n_state = n_ref[hh]
            num = inter * jnp.dot(q, c_state.astype(BF16), preferred_element_type=F32) \
                + jnp.dot(qk.astype(BF16), v, preferred_element_type=F32)
            den = inter * jnp.sum(q.astype(F32) * n_state, axis=-1, keepdims=True) \
                + jnp.sum(qk, axis=-1, keepdims=True)
            hs = num / jnp.maximum(jnp.abs(den), jnp.exp(-m_t))
            m_new = m_t[L - 1:L, :]
            b_last = b_col[L - 1:L, :]
            decay = jnp.exp(b_last + m_prev - m_new)
            ws = jnp.exp(b_last - b_col + li_col - m_new)
            kw = k.astype(F32) * ws
            c_ref[hh] = decay * c_state + jnp.dot(kw.T.astype(BF16), v, preferred_element_type=F32)
            n_ref[hh] = decay * n_state + jnp.sum(kw, axis=0, keepdims=True)
            m_ref[hh] = m_new
            y = hs * lax.rsqrt(jnp.mean(hs * hs, axis=-1, keepdims=True) + NORM_EPS) \
                * hn_ref[:, hh * dv:(hh + 1) * dv]
            o_ref[pl.ds(r0, L), hh * dv:(hh + 1) * dv] = (
                og_ref[pl.ds(r0, L), hh * dv:(hh + 1) * dv].astype(F32) * y).astype(BF16)
        return carry

    lax.fori_loop(0, seq // L, chunk_body, 0)


def _mlstm_scan(q, k, v, og, gl, glt, head_norm, batch, seq, chunk, heads_per_step):
    t = q.shape[0]
    hps = heads_per_step
    dk, dv = MLSTM_QK_DIM, MLSTM_V_DIM
    ng = 2 * MLSTM_HEADS
    est = 2 * seq * hps * (2 * dk + 3 * dv) * 2 + 2 * seq * LANES * 4 + 2 * ng * seq * 4 \
        + hps * dk * dv * 4 + 16 * chunk * chunk * 4
    return pl.pallas_call(
        functools.partial(_mlstm_scan_kernel, chunk=chunk, heads_per_step=hps),
        grid=(batch, MLSTM_HEADS // hps),
        in_specs=[pl.BlockSpec((seq, hps * dk), lambda b, g: (b, g)),
                  pl.BlockSpec((seq, hps * dk), lambda b, g: (b, g)),
                  pl.BlockSpec((seq, hps * dv), lambda b, g: (b, g)),
                  pl.BlockSpec((seq, hps * dv), lambda b, g: (b, g)),
                  pl.BlockSpec((seq, LANES), lambda b, g: (b, 0)),
                  pl.BlockSpec((ng, seq), lambda b, g: (0, b)),
                  pl.BlockSpec((1, hps * dv), lambda b, g: (0, g))],
        out_specs=pl.BlockSpec((seq, hps * dv), lambda b, g: (b, g)),
        out_shape=jax.ShapeDtypeStruct((t, MLSTM_HEADS * dv), BF16),
        scratch_shapes=[pltpu.VMEM((hps, dk, dv), F32), pltpu.VMEM((hps, 1, dk), F32),
                        pltpu.VMEM((hps, 1, 1), F32)],
        compiler_params=pltpu.CompilerParams(
            dimension_semantics=("arbitrary", "arbitrary"), vmem_limit_bytes=_vmem_limit(est)),
        name="mlstm_scan",
    )(q, k, v, og, gl, glt, head_norm.reshape(1, -1))


def kernel(x, c, positions, mla_w_in, mla_q_norm, mla_kv_norm, mla_w_uq, mla_w_ukv, mla_w_o, mlstm_w_in, mlstm_b_gates, mlstm_head_norm, mlstm_w_out, moe_w_router, moe_b_router, moe_w_gu, moe_b_gu, moe_w_down, moe_b_down, ada_w, ada_b, ln_g, ln_b):
    batch, seq, d = x.shape
    assert d == D_MODEL and ada_w.shape[0] == DEPTH
    cfg = _tiles(batch, seq)
    t = batch * seq
    tm = cfg["tm_tok"]
    mod = _adaln(c, ada_w, ada_b, cfg["tn_ada"])
    lng = ln_g.reshape(2 * DEPTH, d)
    lnb = ln_b.reshape(2 * DEPTH, d)
    xt = x.reshape(t, d)
    for i in range(DEPTH):
        j = i // 2
        k_mix, k_ffn = 2 * i, 2 * i + 1
        if i % 2 == 0:
            qn, qr, kn, v, kr = _mla_proj(xt, mod, k_mix, positions, mla_w_in[j], mla_q_norm[j],
                                          mla_kv_norm[j], mla_w_uq[j], mla_w_ukv[j], seq, tm)
            a = _attention(qn, qr, kn, kr, v, batch, seq, cfg["tq"])
            w_o = mla_w_o[j]
        else:
            q, k, v, og, gl, glt = _mlstm_proj(xt, mod, k_mix, mlstm_w_in[j], mlstm_b_gates[j], seq, tm)
            a = _mlstm_scan(q, k, v, og, gl, glt, mlstm_head_norm[j], batch, seq,
                            cfg["chunk"], cfg["heads_per_step"])
            w_o = mlstm_w_out[j]
        x1, h2, topi, topg = _post(a, xt, mod, k_mix, k_ffn, lng[k_mix], lnb[k_mix], w_o,
                                   moe_w_router[i], moe_b_router[i], seq, cfg["tm_post"])
        xt = _moe(h2, topi, topg, x1, mod, k_ffn, lng[k_ffn], lnb[k_ffn],
                  moe_w_gu, moe_b_gu, moe_w_down, moe_b_down, i, seq, cfg)
    return xt.reshape(batch, seq, d)
```

```python
import functools

import jax
import jax.numpy as jnp
from jax import lax
from jax.experimental import pallas as pl
from jax.experimental.pallas import tpu as pltpu

F32 = jnp.float32
BF16 = jnp.bfloat16

D_MODEL = 2048
DEPTH = 2

MLA_HEADS = 16
QK_NOPE_DIM = 128
QK_ROPE_DIM = 64
V_HEAD_DIM = 128
Q_LORA_RANK = 512
KV_LORA_RANK = 512
ROPE_THETA = 10000.0

MLSTM_HEADS = 8
MLSTM_QK_DIM = D_MODEL // (2 * MLSTM_HEADS)
MLSTM_V_DIM = D_MODEL // MLSTM_HEADS
GATE_SOFTCAP = 15.0

N_EXPERTS = 32
TOP_K = 4
D_EXPERT = D_MODEL
SWIGLU_ALPHA = 1.702
SWIGLU_LIMIT = 7.0

DEEPNORM_ALPHA = (2 * DEPTH) ** 0.25
NORM_EPS = 1e-6

LANES = 128
V7X_VMEM_BYTES = 64 * 1024 * 1024
NEG_INF = float("-inf")


def _vmem_limit(estimate_bytes):
    return int(min(estimate_bytes + (12 << 20), V7X_VMEM_BYTES - (8 << 20)))


def _tiles(batch, seq):
    t = batch * seq
    cfg = dict(
        tm_tok=min(256, seq),
        tm_post=min(512, seq),
        tq=min(256, seq),
        chunk=min(128, seq),
        heads_per_step=4,
        tm_moe=min(512, t * TOP_K // 8),
        tf=1024,
        tc_gemm=512,
        tn=1024,
        tc=min(128, seq),
        tn_ada=768,
    )
    return cfg


def _adaln_kernel(c_ref, w_ref, b_ref, o_ref):
    c = c_ref[...]
    cond = c * jax.nn.sigmoid(c)
    o_ref[...] = jnp.dot(cond.astype(BF16), w_ref[...].astype(BF16),
                         preferred_element_type=F32) + b_ref[...]


def _adaln(c, ada_w, ada_b, tn):
    b, d = c.shape
    n_mod = ada_w.shape[0] * ada_w.shape[1]
    w = ada_w.reshape(n_mod, d, 3 * d)
    bias = ada_b.reshape(n_mod, 1, 3 * d)
    out = pl.pallas_call(
        _adaln_kernel,
        grid=(n_mod, 3 * d // tn),
        in_specs=[
            pl.BlockSpec((b, d), lambda i, j: (0, 0)),
            pl.BlockSpec((None, d, tn), lambda i, j: (i, 0, j)),
            pl.BlockSpec((None, 1, tn), lambda i, j: (i, 0, j)),
        ],
        out_specs=pl.BlockSpec((None, b, tn), lambda i, j: (i, 0, j)),
        out_shape=jax.ShapeDtypeStruct((n_mod, b, 3 * d), F32),
        compiler_params=pltpu.CompilerParams(
            dimension_semantics=("arbitrary", "arbitrary"),
            vmem_limit_bytes=_vmem_limit(2 * d * tn * 4 + d * tn * 2)),
        name="adaln",
    )(c, w, bias)
    return out.reshape(n_mod, b, 1, 3 * d)


def _mod_spec(k, part, tiles_per_batch):
    return pl.BlockSpec((None, None, 1, D_MODEL),
                        lambda i, *_: (k, i // tiles_per_batch, 0, part))


def _row_spec(k):
    return pl.BlockSpec((None, 1, D_MODEL), lambda i, *_: (k, 0, 0))


def _resident(shape):
    nd = len(shape)
    return pl.BlockSpec(shape, lambda *_: (0,) * nd, pipeline_mode=pl.Buffered(1))


def _mla_proj_kernel(x_ref, shift_ref, scale_ref, pos_ref, rope_ref, win_ref, qnorm_ref, kvnorm_ref,
                     wqn_ref, wqr_ref, wqs_ref, wkn_ref, wv_ref,
                     qn_ref, qr_ref, kn_ref, v_ref, kr_ref):
    h = x_ref[...] * (1.0 + scale_ref[...]) + shift_ref[...]
    lat = jnp.dot(h.astype(BF16), win_ref[...], preferred_element_type=F32)
    cq = lat[:, :Q_LORA_RANK]
    ckv = lat[:, Q_LORA_RANK:Q_LORA_RANK + KV_LORA_RANK]
    kr = lat[:, Q_LORA_RANK + KV_LORA_RANK:Q_LORA_RANK + KV_LORA_RANK + LANES]
    kr_sw = lat[:, Q_LORA_RANK + KV_LORA_RANK + LANES:]
    cq = cq * lax.rsqrt(jnp.mean(cq * cq, axis=-1, keepdims=True) + NORM_EPS) * qnorm_ref[...]
    ckv = ckv * lax.rsqrt(jnp.mean(ckv * ckv, axis=-1, keepdims=True) + NORM_EPS) * kvnorm_ref[...]
    cq = cq.astype(BF16)
    ckv = ckv.astype(BF16)

    ang = pos_ref[...].astype(F32) * rope_ref[0:1, :]
    cos_t = jnp.cos(ang) * rope_ref[1:2, :]
    sin_t = jnp.sin(ang) * rope_ref[2:3, :]

    qn_ref[...] = jnp.dot(cq, wqn_ref[...], preferred_element_type=F32).astype(BF16)
    q_rope = jnp.dot(cq, wqr_ref[...], preferred_element_type=F32)
    q_swap = jnp.dot(cq, wqs_ref[...], preferred_element_type=F32)
    cos_h = jnp.tile(cos_t, (1, MLA_HEADS))
    sin_h = jnp.tile(sin_t, (1, MLA_HEADS))
    qr_ref[...] = (q_rope * cos_h + q_swap * sin_h).astype(BF16)
    kn_ref[...] = jnp.dot(ckv, wkn_ref[...], preferred_element_type=F32).astype(BF16)
    v_ref[...] = jnp.dot(ckv, wv_ref[...], preferred_element_type=F32).astype(BF16)
    kr_ref[...] = (kr * cos_t + kr_sw * sin_t).astype(BF16)


def _rope_table():
    half = QK_ROPE_DIM // 2
    inv_freq = ROPE_THETA ** (-jnp.arange(0, QK_ROPE_DIM, 2, dtype=F32) / QK_ROPE_DIM)
    zeros = jnp.zeros((LANES - QK_ROPE_DIM,), F32)
    freq_row = jnp.concatenate([inv_freq, inv_freq, zeros])
    cos_mask = jnp.concatenate([jnp.ones((QK_ROPE_DIM,), F32), zeros])
    sin_sign = jnp.concatenate([-jnp.ones((half,), F32), jnp.ones((half,), F32), zeros])
    pad = jnp.zeros((5, LANES), F32)
    return jnp.concatenate([jnp.stack([freq_row, cos_mask, sin_sign]), pad], axis=0)


def _mla_weights(w_in, w_uq, w_ukv):
    d = w_in.shape[0]
    h = MLA_HEADS
    half = QK_ROPE_DIM // 2
    lat = Q_LORA_RANK + KV_LORA_RANK
    kr = w_in[:, lat:]
    zpad = jnp.zeros((d, LANES - QK_ROPE_DIM), w_in.dtype)
    win = jnp.concatenate([w_in[:, :lat], kr, zpad, kr[:, half:], kr[:, :half], zpad], axis=1)
    wq = w_uq.reshape(Q_LORA_RANK, h, QK_NOPE_DIM + QK_ROPE_DIM)
    wqn = wq[:, :, :QK_NOPE_DIM].reshape(Q_LORA_RANK, h * QK_NOPE_DIM)
    rope = wq[:, :, QK_NOPE_DIM:]
    zq = jnp.zeros((Q_LORA_RANK, h, LANES - QK_ROPE_DIM), w_uq.dtype)
    wqr = jnp.concatenate([rope, zq], axis=-1).reshape(Q_LORA_RANK, h * LANES)
    wqs = jnp.concatenate([rope[..., half:], rope[..., :half], zq], axis=-1).reshape(Q_LORA_RANK, h * LANES)
    wkv = w_ukv.reshape(KV_LORA_RANK, h, QK_NOPE_DIM + V_HEAD_DIM)
    wkn = wkv[:, :, :QK_NOPE_DIM].reshape(KV_LORA_RANK, h * QK_NOPE_DIM)
    wv = wkv[:, :, QK_NOPE_DIM:].reshape(KV_LORA_RANK, h * V_HEAD_DIM)
    return tuple(a.astype(BF16) for a in (win, wqn, wqr, wqs, wkn, wv))


def _mla_proj(x2d, mod, k_mod, positions, w_in, q_norm, kv_norm, w_uq, w_ukv, seq, tm):
    t, d = x2d.shape
    tiles_per_batch = seq // tm
    win, wqn, wqr, wqs, wkn, wv = _mla_weights(w_in, w_uq, w_ukv)
    hn = MLA_HEADS * LANES
    tok = lambda n: pl.BlockSpec((tm, n), lambda i: (i, 0))
    weights_bytes = 2 * (win.size + wqn.size + wqr.size + wqs.size + wkn.size + wv.size)
    est = weights_bytes + 2 * tm * d * 4 + 2 * (4 * tm * hn * 2 + tm * LANES * 2) + 8 * tm * hn * 4
    outs = pl.pallas_call(
        _mla_proj_kernel,
        grid=(t // tm,),
        in_specs=[
            tok(d),
            _mod_spec(k_mod, 0, tiles_per_batch),
            _mod_spec(k_mod, 1, tiles_per_batch),
            pl.BlockSpec((tm, 1), lambda i: (i, 0)),
            _resident((8, LANES)),
            _resident(win.shape),
            _resident((1, Q_LORA_RANK)),
            _resident((1, KV_LORA_RANK)),
            _resident(wqn.shape), _resident(wqr.shape), _resident(wqs.shape),
            _resident(wkn.shape), _resident(wv.shape),
        ],
        out_specs=[tok(hn), tok(hn), tok(hn), tok(hn), tok(LANES)],
        out_shape=[jax.ShapeDtypeStruct((t, hn), BF16)] * 4 + [jax.ShapeDtypeStruct((t, LANES), BF16)],
        compiler_params=pltpu.CompilerParams(
            dimension_semantics=("arbitrary",), vmem_limit_bytes=_vmem_limit(est)),
        name="mla_proj",
    )(x2d, mod, mod, positions.reshape(t, 1), _rope_table(), win,
      q_norm.reshape(1, -1), kv_norm.reshape(1, -1), wqn, wqr, wqs, wkn, wv)
    return outs


def _attn_kernel(qn_ref, qr_ref, kn_ref, kr_ref, v_ref, o_ref, s_ref, *, tq, scale):
    seq = qn_ref.shape[0]
    row = lax.broadcasted_iota(jnp.int32, (tq, tq), 0)
    col = lax.broadcasted_iota(jnp.int32, (tq, tq), 1)
    causal = col <= row
    exp2_scale = scale * 1.4426950408889634

    def fold_lanes(a):
        return [a[:, g * LANES:(g + 1) * LANES] for g in range(tq // LANES)]

    for qi in range(seq // tq):
        rows = slice(qi * tq, (qi + 1) * tq)
        q = jnp.concatenate([qn_ref[rows, :], qr_ref[rows, :]], axis=-1)
        m_acc = jnp.full((tq, LANES), NEG_INF, F32)
        for j in range(qi + 1):
            cols = slice(j * tq, (j + 1) * tq)
            k = jnp.concatenate([kn_ref[cols, :], kr_ref[cols, :]], axis=-1)
            s = lax.dot_general(q, k, (((1,), (1,)), ((), ())), preferred_element_type=F32)
            if j == qi:
                s = jnp.where(causal, s, NEG_INF)
            s_ref[:, cols] = s
            for part in fold_lanes(s):
                m_acc = jnp.maximum(m_acc, part)
        m = jnp.max(m_acc, axis=-1, keepdims=True)
        l_acc = jnp.zeros((tq, LANES), F32)
        acc = jnp.zeros((tq, V_HEAD_DIM), F32)
        for j in range(qi + 1):
            cols = slice(j * tq, (j + 1) * tq)
            p = jnp.exp2((s_ref[:, cols] - m) * exp2_scale)
            for part in fold_lanes(p):
                l_acc = l_acc + part
            acc = acc + jnp.dot(p.astype(BF16), v_ref[cols, :], preferred_element_type=F32)
        l = jnp.sum(l_acc, axis=-1, keepdims=True)
        o_ref[rows, :] = (acc / l).astype(BF16)


def _attention(qn, qr, kn, kr, v, batch, seq, tq):
    t = qn.shape[0]
    head = lambda: pl.BlockSpec((seq, LANES), lambda b, h: (b, h))
    scale = (QK_NOPE_DIM + QK_ROPE_DIM) ** -0.5
    est = 2 * 6 * seq * LANES * 2 + tq * seq * 4 + 6 * tq * tq * 4
    return pl.pallas_call(
        functools.partial(_attn_kernel, tq=tq, scale=scale),
        grid=(batch, MLA_HEADS),
        in_specs=[head(), head(), head(), pl.BlockSpec((seq, LANES), lambda b, h: (b, 0)), head()],
        out_specs=head(),
        out_shape=jax.ShapeDtypeStruct((t, MLA_HEADS * V_HEAD_DIM), BF16),
        scratch_shapes=[pltpu.VMEM((tq, seq), F32)],
        compiler_params=pltpu.CompilerParams(
            dimension_semantics=("arbitrary", "arbitrary"), vmem_limit_bytes=_vmem_limit(est)),
        name="mla_attention",
    )(qn, qr, kn, kr, v)


SLABS = D_MODEL // LANES


PACKED_SLABS = SLABS // 2


def _store_slabs(ref, val, first_slab=0, first_row=0, pitch=SLABS):
    rows = val.shape[0]
    for s in range(val.shape[1] // LANES):
        ref[pl.ds(first_row * pitch + first_slab + s, rows, stride=pitch), :] = val[:, s * LANES:(s + 1) * LANES]


def _pack_bf16_pairs(x):
    n = x.shape[1] // 2
    hi = pltpu.bitcast(x[:, :n].astype(BF16).astype(F32), jnp.uint32)
    lo = pltpu.bitcast(x[:, n:].astype(BF16).astype(F32), jnp.uint32)
    return hi | (lo >> 16)


def _unpack_bf16_pairs(u):
    hi = pltpu.bitcast(u & jnp.uint32(0xFFFF0000), F32).astype(BF16)
    lo = pltpu.bitcast(u << 16, F32).astype(BF16)
    return hi, lo


def _layer_norm(z, g, b):
    mu = jnp.mean(z, axis=-1, keepdims=True)
    zc = z - mu
    var = jnp.mean(zc * zc, axis=-1, keepdims=True)
    return zc * lax.rsqrt(var + NORM_EPS) * g + b


def _post_kernel(a_ref, x_ref, gate_ref, lng_ref, lnb_ref, shift_ref, scale_ref, wo_ref, wr_ref, br_ref,
                 x1_ref, h2_ref, topi_ref, topg_ref, *, sub):
    starts = range(0, a_ref.shape[0], sub)
    ys = [jnp.dot(a_ref[r0:r0 + sub, :], wo_ref[...], preferred_element_type=F32) for r0 in starts]
    for y, r0 in zip(ys, starts):
        _post_rows(slice(r0, r0 + sub), r0, y, x_ref, gate_ref, lng_ref, lnb_ref, shift_ref, scale_ref,
                   wr_ref, br_ref, x1_ref, h2_ref, topi_ref, topg_ref)


def _post_rows(rows, r0, y, x_ref, gate_ref, lng_ref, lnb_ref, shift_ref, scale_ref, wr_ref,
               br_ref, x1_ref, h2_ref, topi_ref, topg_ref):
    z = DEEPNORM_ALPHA * x_ref[rows, :] + gate_ref[...] * y
    x1 = _layer_norm(z, lng_ref[...], lnb_ref[...])
    x1_ref[rows, :] = x1
    h2 = x1 * (1.0 + scale_ref[...]) + shift_ref[...]
    _store_slabs(h2_ref, _pack_bf16_pairs(h2), first_row=r0, pitch=PACKED_SLABS)
    h_hi = h2.astype(BF16)
    h_lo = (h2 - h_hi.astype(F32)).astype(BF16)
    t_hi = jnp.dot(h_hi, wr_ref[...], preferred_element_type=F32)
    t_lo = jnp.dot(h_lo, wr_ref[:, :LANES], preferred_element_type=F32)
    logits = t_hi[:, :LANES] + (t_hi[:, LANES:] + t_lo) + br_ref[...]
    tm = logits.shape[0]
    lane = lax.broadcasted_iota(jnp.int32, (tm, LANES), 1)
    vals, idxs = [], []
    for _ in range(TOP_K):
        mk = jnp.max(logits, axis=-1, keepdims=True)
        ik = jnp.min(jnp.where(logits == mk, lane, LANES), axis=-1, keepdims=True)
        vals.append(mk)
        idxs.append(ik)
        logits = jnp.where(lane == ik, NEG_INF, logits)
    exps = [jnp.exp(v - vals[0]) for v in vals]
    den = exps[0]
    for e in exps[1:]:
        den = den + e
    topi = jnp.zeros((tm, LANES), jnp.int32)
    topg = jnp.zeros((tm, LANES), F32)
    for k in range(TOP_K):
        topi = jnp.where(lane == k, idxs[k], topi)
        topg = jnp.where(lane == k, exps[k] / den, topg)
    topi_ref[rows, :] = topi
    topg_ref[rows, :] = topg


def _post(a, x2d, mod, k_mix, k_ffn, ln_g, ln_b, w_o, w_router, b_router, seq, tm):
    t, d = x2d.shape
    tiles_per_batch = seq // tm
    wr = jnp.concatenate([w_router, jnp.zeros((d, LANES - N_EXPERTS), F32)], axis=1)
    wr_hi = wr.astype(BF16)
    wr_lo = (wr - wr_hi.astype(F32)).astype(BF16)
    wr = jnp.concatenate([wr_hi, wr_lo], axis=1)
    br = jnp.concatenate([b_router, jnp.full((LANES - N_EXPERTS,), NEG_INF, F32)]).reshape(1, LANES)
    tok = lambda n: pl.BlockSpec((tm, n), lambda i: (i, 0))
    sub = min(256, tm)
    est = d * d * 2 + d * LANES * 4 + 2 * tm * d * (2 + 4 + 4 + 4) + 6 * sub * d * 4
    return pl.pallas_call(
        functools.partial(_post_kernel, sub=sub),
        grid=(t // tm,),
        in_specs=[
            tok(d), tok(d),
            _mod_spec(k_mix, 2, tiles_per_batch),
            _resident((1, d)), _resident((1, d)),
            _mod_spec(k_ffn, 0, tiles_per_batch),
            _mod_spec(k_ffn, 1, tiles_per_batch),
            _resident((d, d)), _resident((d, 2 * LANES)), _resident((1, LANES)),
        ],
        out_specs=[tok(d), pl.BlockSpec((tm * PACKED_SLABS, LANES), lambda i: (i, 0)), tok(LANES), tok(LANES)],
        out_shape=[jax.ShapeDtypeStruct((t, d), F32), jax.ShapeDtypeStruct((t * PACKED_SLABS, LANES), jnp.uint32),
                   jax.ShapeDtypeStruct((t, LANES), jnp.int32), jax.ShapeDtypeStruct((t, LANES), F32)],
        compiler_params=pltpu.CompilerParams(
            dimension_semantics=("arbitrary",), vmem_limit_bytes=_vmem_limit(est)),
        name="mixer_out_ln_router",
    )(a, x2d, mod, ln_g.reshape(1, d), ln_b.reshape(1, d), mod, mod, w_o.astype(BF16), wr, br)


def _route(topi, tm_moe):
    t = topi.shape[0]
    m = t * TOP_K
    e_tk = topi[:, :TOP_K]
    a = jnp.sum((e_tk[:, :, None] == jnp.arange(N_EXPERTS, dtype=jnp.int32)).astype(F32), axis=1)
    blk = min(256, t)
    ab = a.reshape(t // blk, blk, N_EXPERTS)
    tri = jnp.tril(jnp.ones((blk, blk), F32))
    within = jnp.einsum("ts,bse->bte", tri, ab)
    btot = within[:, -1, :]
    boff = jnp.cumsum(btot, axis=0) - btot
    excl = (within + boff[:, None, :] - ab).reshape(t, N_EXPERTS)
    counts = (boff[-1] + btot[-1]).astype(jnp.int32)
    rank = jnp.take_along_axis(excl, e_tk, axis=1).astype(jnp.int32)
    padded = ((counts + tm_moe - 1) // tm_moe) * tm_moe
    pends = jnp.cumsum(padded)
    pstarts = pends - padded
    dest = (pstarts[e_tk] + rank).astype(jnp.int32).reshape(m)
    n_tiles = m // tm_moe + N_EXPERTS
    row_tok = jnp.zeros((n_tiles * tm_moe,), jnp.int32).at[dest].set(
        jnp.arange(m, dtype=jnp.int32) // TOP_K, unique_indices=True)
    tile_start = (pstarts // tm_moe).astype(jnp.int32)
    tile_count = (padded // tm_moe).astype(jnp.int32)
    n_used = (pends[-1] // tm_moe).astype(jnp.int32).reshape(1)
    return dest, row_tok, tile_start, tile_count, n_used


def _gather_issue(idx_ref, src_hbm, buf, sem, slot, n_rows):
    def body(i, carry):
        for u in range(2):
            r = 2 * i + u
            pltpu.make_async_copy(src_hbm.at[idx_ref[0, 0, r]], buf.at[slot, :, r, :],
                                  sem.at[slot]).start(priority=u)
        return carry
    lax.fori_loop(0, n_rows // 2, body, 0, unroll=4)


def _gather_wait(buf, sem, slot):
    pltpu.make_async_copy(buf.at[slot], buf.at[slot], sem.at[slot]).wait()


def _dispatch_kernel(ns_ref, idx_ref, nxt_ref, src_hbm, o_ref, buf, sem, *, n_rows):
    i = pl.program_id(0)
    n_active = ns_ref[0]
    slot = lax.rem(i, 2)

    @pl.when(jnp.logical_and(i == 0, n_active > 0))
    def _():
        _gather_issue(idx_ref, src_hbm, buf, sem, 0, n_rows)

    @pl.when(i + 1 < n_active)
    def _():
        _gather_issue(nxt_ref, src_hbm, buf, sem, 1 - slot, n_rows)

    @pl.when(i < n_active)
    def _():
        _gather_wait(buf, sem, slot)
        hi, lo = _unpack_bf16_pairs(buf[slot])
        o_ref[0:PACKED_SLABS] = hi
        o_ref[PACKED_SLABS:SLABS] = lo

    @pl.when(i >= n_active)
    def _():
        o_ref[...] = jnp.zeros(o_ref.shape, o_ref.dtype)


def _dispatch(h2_packed, row_tok, n_used, r):
    t, packed_slabs, _ = h2_packed.shape
    slabs = 2 * packed_slabs
    n = row_tok.shape[0] // r
    idx = row_tok.reshape(n, 1, r)
    smem = lambda f: pl.BlockSpec((1, 1, r), f, memory_space=pltpu.SMEM)
    return pl.pallas_call(
        functools.partial(_dispatch_kernel, n_rows=r),
        grid_spec=pltpu.PrefetchScalarGridSpec(
            num_scalar_prefetch=1,
            grid=(n,),
            in_specs=[smem(lambda i, ns: (i, 0, 0)),
                      smem(lambda i, ns: (jnp.minimum(i + 1, n - 1), 0, 0)),
                      pl.BlockSpec(memory_space=pl.ANY)],
            out_specs=pl.BlockSpec((None, slabs, r, LANES), lambda i, ns: (i, 0, 0, 0)),
            scratch_shapes=[pltpu.VMEM((2, packed_slabs, r, LANES), jnp.uint32),
                            pltpu.SemaphoreType.DMA((2,))]),
        out_shape=jax.ShapeDtypeStruct((n, slabs, r, LANES), BF16),
        compiler_params=pltpu.CompilerParams(
            dimension_semantics=("arbitrary",),
            vmem_limit_bytes=_vmem_limit(4 * r * slabs * LANES * 2)),
        name="moe_dispatch_gather",
    )(n_used, idx, idx, h2_packed)


def _expert_weights(step, n_steps, copies_for, stage, w_bf):
    @pl.when(step == 0)
    def _():
        for c in copies_for(step):
            c.start(priority=1)

    for c in copies_for(step):
        c.wait()
    w_bf[...] = stage[...].astype(BF16)

    @pl.when(step + 1 < n_steps)
    def _():
        for c in copies_for(step + 1):
            c.start(priority=1)


def _tile_loop(n_tiles, in_copies, out_copy, compute):
    def start_in(k, slot):
        for c in in_copies(k, slot):
            c.start()

    @pl.when(n_tiles > 0)
    def _():
        start_in(0, 0)

    def tile(k, slot):
        for c in in_copies(k, slot):
            c.wait()

        @pl.when(k + 1 < n_tiles)
        def _():
            start_in(k + 1, 1 - slot)

        @pl.when(k >= 2)
        def _():
            out_copy(k - 2, slot).wait()

        compute(slot)
        out_copy(k, slot).start()

    def pair(p, carry):
        tile(2 * p, 0)

        @pl.when(2 * p + 1 < n_tiles)
        def _():
            tile(2 * p + 1, 1)
        return carry

    lax.fori_loop(0, (n_tiles + 1) // 2, pair, 0)

    for slot in (0, 1):
        @pl.when(jnp.logical_and(n_tiles >= 2, lax.rem(n_tiles, 2) == slot))
        def _():
            out_copy(n_tiles - 2, slot).wait()

        @pl.when(jnp.logical_and(n_tiles >= 1, lax.rem(n_tiles + 1, 2) == slot))
        def _():
            out_copy(n_tiles - 1, slot).wait()


def _zero_tail_tiles(first, n_total, zero_src, dst_for, sem):
    def body(t, carry):
        cp = pltpu.make_async_copy(zero_src, dst_for(t), sem)
        cp.start()
        cp.wait()
        return carry
    lax.fori_loop(first, n_total, body, 0)


def _gemm1_kernel(ts_ref, nt_ref, nu_ref, bg_ref, bu_ref, w_hbm, x_hbm, o_hbm,
                  stage, w_bf, xbuf, obuf, wsem, xsem, osem, *, layer, tf, tc):
    j = pl.program_id(0)
    e = pl.program_id(1)
    n_e = pl.num_programs(1)
    f = D_EXPERT

    def weight_copies(step):
        jj = step // n_e
        ee = step - jj * n_e
        return [pltpu.make_async_copy(
            w_hbm.at[layer, ee, :, pl.ds(pl.multiple_of(half * f + jj * tf, tf), tf)],
            stage.at[half], wsem.at[half]) for half in range(2)]

    _expert_weights(j * n_e + e, pl.num_programs(0) * n_e, weight_copies, stage, w_bf)

    t0 = ts_ref[e]

    def x_copies(k, slot):
        return [pltpu.make_async_copy(x_hbm.at[t0 + k], xbuf.at[slot], xsem.at[slot])]

    def o_copy(k, slot):
        return pltpu.make_async_copy(obuf.at[slot], o_hbm.at[j, t0 + k], osem.at[slot])

    def compute(slot):
        x = jnp.concatenate([xbuf[slot, s] for s in range(xbuf.shape[1])], axis=-1)
        for c in range(tf // tc):
            cols = slice(c * tc, (c + 1) * tc)
            g = jnp.dot(x, w_bf[0, :, cols], preferred_element_type=F32) + bg_ref[:, cols]
            u = jnp.dot(x, w_bf[1, :, cols], preferred_element_type=F32) + bu_ref[:, cols]
            g = jnp.minimum(g, SWIGLU_LIMIT)
            u = jnp.clip(u, -SWIGLU_LIMIT, SWIGLU_LIMIT)
            obuf[slot, :, cols] = (g * jax.nn.sigmoid(SWIGLU_ALPHA * g) * (u + 1.0)).astype(BF16)

    _tile_loop(nt_ref[e], x_copies, o_copy, compute)

    @pl.when(e == n_e - 1)
    def _():
        obuf[0] = jnp.zeros(obuf.shape[1:], obuf.dtype)
        _zero_tail_tiles(nu_ref[0], o_hbm.shape[1], obuf.at[0], lambda t: o_hbm.at[j, t], osem.at[0])


def _gemm1(x_rows, tile_start, tile_count, n_used, w_gu, b_gu, layer, tf, tc):
    n_tiles, slabs, tm, _ = x_rows.shape
    d = slabs * LANES
    f = D_EXPERT
    nf = f // tf
    bias = b_gu.reshape(b_gu.shape[0], N_EXPERTS, 1, 2 * f)
    bspec = lambda off: pl.BlockSpec((None, None, 1, tf), lambda j, e, ts, nt, nu: (layer, e, 0, j + off))
    any_spec = pl.BlockSpec(memory_space=pl.ANY)
    est = 2 * d * tf * (4 + 2) + 2 * tm * d * 2 + 2 * tm * tf * 2 + 6 * tm * tc * 4
    return pl.pallas_call(
        functools.partial(_gemm1_kernel, layer=layer, tf=tf, tc=tc),
        grid_spec=pltpu.PrefetchScalarGridSpec(
            num_scalar_prefetch=3,
            grid=(nf, N_EXPERTS),
            in_specs=[bspec(0), bspec(nf), any_spec, any_spec],
            out_specs=any_spec,
            scratch_shapes=[pltpu.VMEM((2, d, tf), F32), pltpu.VMEM((2, d, tf), BF16),
                            pltpu.VMEM((2, slabs, tm, LANES), BF16), pltpu.VMEM((2, tm, tf), BF16),
                            pltpu.SemaphoreType.DMA((2,)), pltpu.SemaphoreType.DMA((2,)),
                            pltpu.SemaphoreType.DMA((2,))]),
        out_shape=jax.ShapeDtypeStruct((nf, n_tiles, tm, tf), BF16),
        compiler_params=pltpu.CompilerParams(
            dimension_semantics=("arbitrary", "arbitrary"), vmem_limit_bytes=_vmem_limit(est)),
        name="moe_gate_up_swiglu",
    )(tile_start, tile_count, n_used, bias, bias, w_gu, x_rows)


def _gemm2_kernel(ts_ref, nt_ref, nu_ref, b_ref, w_hbm, a_hbm, o_hbm,
                  stage, w_bf, abuf, obuf, wsem, asem, osem, *, layer, tn):
    e = pl.program_id(0)
    nf, tm = abuf.shape[1], abuf.shape[2]
    tile_rows = tm * SLABS

    def weight_copies(step):
        return [pltpu.make_async_copy(w_hbm.at[layer, step], stage, wsem.at[0])]

    _expert_weights(e, pl.num_programs(0), weight_copies, stage, w_bf)

    t0 = ts_ref[e]

    def a_copies(k, slot):
        return [pltpu.make_async_copy(a_hbm.at[c, t0 + k], abuf.at[slot, c], asem.at[slot, c])
                for c in range(nf)]

    def o_tile(t):
        return o_hbm.at[pl.ds(pl.multiple_of(t * tile_rows, tile_rows), tile_rows), :]

    def o_copy(k, slot):
        return pltpu.make_async_copy(obuf.at[pl.ds(slot * tile_rows, tile_rows), :],
                                     o_tile(t0 + k), osem.at[slot])

    def compute(slot):
        a = jnp.concatenate([abuf[slot, c] for c in range(nf)], axis=-1)
        for c in range(w_bf.shape[1] // tn):
            cols = slice(c * tn, (c + 1) * tn)
            res = jnp.dot(a, w_bf[:, cols], preferred_element_type=F32) + b_ref[:, cols]
            _store_slabs(obuf, res, first_slab=c * tn // LANES, first_row=slot * tm)

    _tile_loop(nt_ref[e], a_copies, o_copy, compute)

    @pl.when(e == pl.num_programs(0) - 1)
    def _():
        obuf[0:tile_rows, :] = jnp.zeros((tile_rows, LANES), obuf.dtype)
        _zero_tail_tiles(nu_ref[0], o_hbm.shape[0] // tile_rows, obuf.at[0:tile_rows, :], o_tile, osem.at[0])


def _gemm2(act, tile_start, tile_count, n_used, w_down, b_down, layer, tn):
    nf, n_tiles, tm, tf = act.shape
    f = nf * tf
    d = D_MODEL
    bias = b_down.reshape(b_down.shape[0], N_EXPERTS, 1, d)
    any_spec = pl.BlockSpec(memory_space=pl.ANY)
    est = f * d * (4 + 2) + 2 * tm * f * 2 + 2 * tm * d * 4 + 3 * tm * tn * 4
    return pl.pallas_call(
        functools.partial(_gemm2_kernel, layer=layer, tn=tn),
        grid_spec=pltpu.PrefetchScalarGridSpec(
            num_scalar_prefetch=3,
            grid=(N_EXPERTS,),
            in_specs=[pl.BlockSpec((None, None, 1, d), lambda e, ts, nt, nu: (layer, e, 0, 0)),
                      any_spec, any_spec],
            out_specs=any_spec,
            scratch_shapes=[pltpu.VMEM((f, d), F32), pltpu.VMEM((f, d), BF16),
                            pltpu.VMEM((2, nf, tm, tf), BF16), pltpu.VMEM((2 * tm * SLABS, LANES), F32),
                            pltpu.SemaphoreType.DMA((1,)), pltpu.SemaphoreType.DMA((2, nf)),
                            pltpu.SemaphoreType.DMA((2,))]),
        out_shape=jax.ShapeDtypeStruct((n_tiles * tm * SLABS, LANES), F32),
        compiler_params=pltpu.CompilerParams(
            dimension_semantics=("arbitrary",), vmem_limit_bytes=_vmem_limit(est)),
        name="moe_down",
    )(tile_start, tile_count, n_used, bias, w_down, act)


def _combine_kernel(idx_ref, nxt_ref, rows_hbm, g_ref, x_ref, gate_ref, lng_ref, lnb_ref, o_ref,
                    buf, sem, *, tc):
    i = pl.program_id(0)
    n = pl.num_programs(0)
    slot = lax.rem(i, 2)
    n_rows = TOP_K * tc

    @pl.when(i == 0)
    def _():
        _gather_issue(idx_ref, rows_hbm, buf, sem, 0, n_rows)

    @pl.when(i + 1 < n)
    def _():
        _gather_issue(nxt_ref, rows_hbm, buf, sem, 1 - slot, n_rows)

    _gather_wait(buf, sem, slot)
    g = g_ref[...]
    gk = [jnp.broadcast_to(g[:, k:k + 1], (tc, LANES)) for k in range(TOP_K)]
    parts = []
    for s in range(buf.shape[1]):
        acc = gk[0] * buf[slot, s, pl.ds(0, tc), :]
        for k in range(1, TOP_K):
            acc = acc + gk[k] * buf[slot, s, pl.ds(k * tc, tc), :]
        parts.append(acc)
    y = jnp.concatenate(parts, axis=-1)
    z = DEEPNORM_ALPHA * x_ref[...] + gate_ref[...] * y
    o_ref[...] = _layer_norm(z, lng_ref[...], lnb_ref[...])


def _combine(out_rows, dest, topg, x1, mod, k_ffn, ln_g, ln_b, seq, tc):
    t, d = x1.shape
    n = t // tc
    tiles_per_batch = seq // tc
    idx = dest.reshape(n, tc, TOP_K).transpose(0, 2, 1).reshape(n, 1, TOP_K * tc)
    smem = lambda f: pl.BlockSpec((1, 1, TOP_K * tc), f, memory_space=pltpu.SMEM)
    tok = lambda w: pl.BlockSpec((tc, w), lambda i: (i, 0))
    return pl.pallas_call(
        functools.partial(_combine_kernel, tc=tc),
        grid=(n,),
        in_specs=[smem(lambda i: (i, 0, 0)),
                  smem(lambda i: (jnp.minimum(i + 1, n - 1), 0, 0)),
                  pl.BlockSpec(memory_space=pl.ANY),
                  tok(LANES), tok(d),
                  _mod_spec(k_ffn, 2, tiles_per_batch),
                  _resident((1, d)), _resident((1, d))],
        out_specs=tok(d),
        out_shape=jax.ShapeDtypeStruct((t, d), F32),
        scratch_shapes=[pltpu.VMEM((2, d // LANES, TOP_K * tc, LANES), F32), pltpu.SemaphoreType.DMA((2,))],
        compiler_params=pltpu.CompilerParams(
            dimension_semantics=("arbitrary",),
            vmem_limit_bytes=_vmem_limit(2 * TOP_K * tc * d * 4 + 6 * tc * d * 4)),
        name="moe_combine_ln",
    )(idx, idx, out_rows, topg, x1, mod, ln_g.reshape(1, d), ln_b.reshape(1, d))


def _moe(h2, topi, topg, x1, mod, k_ffn, ln_g, ln_b, w_gu, b_gu, w_down, b_down, layer, seq, cfg):
    dest, row_tok, tile_start, tile_count, n_used = _route(topi, cfg["tm_moe"])
    x_rows = _dispatch(h2.reshape(-1, PACKED_SLABS, LANES), row_tok, n_used, cfg["tm_moe"])
    act = _gemm1(x_rows, tile_start, tile_count, n_used, w_gu, b_gu, layer, cfg["tf"], cfg["tc_gemm"])
    out_rows = _gemm2(act, tile_start, tile_count, n_used, w_down, b_down, layer, cfg["tn"])
    out_rows = out_rows.reshape(-1, SLABS, LANES)
    return _combine(out_rows, dest, topg, x1, mod, k_ffn, ln_g, ln_b, seq, cfg["tc"])


def _softcap(g):
    return GATE_SOFTCAP * jnp.tanh(g / GATE_SOFTCAP)


def _log_sigmoid(x):
    return jnp.minimum(x, 0.0) - jnp.log(1.0 + jnp.exp(-jnp.abs(x)))


def _mlstm_proj_kernel(x_ref, shift_ref, scale_ref, wq_ref, wk_ref, wv_ref, wo_ref, wg_ref, wgt_ref,
                       bg_ref, bgt_ref, q_ref, k_ref, v_ref, og_ref, gl_ref, glt_ref):
    h = (x_ref[...] * (1.0 + scale_ref[...]) + shift_ref[...]).astype(BF16)
    q_ref[...] = (jnp.dot(h, wq_ref[...], preferred_element_type=F32) * MLSTM_QK_DIM ** -0.5).astype(BF16)
    k_ref[...] = jnp.dot(h, wk_ref[...], preferred_element_type=F32).astype(BF16)
    v_ref[...] = jnp.dot(h, wv_ref[...], preferred_element_type=F32).astype(BF16)
    og_ref[...] = jax.nn.sigmoid(jnp.dot(h, wo_ref[...], preferred_element_type=F32)).astype(BF16)
    g = _softcap(jnp.dot(h, wg_ref[...], preferred_element_type=F32) + bg_ref[...])
    lane = lax.broadcasted_iota(jnp.int32, g.shape, 1)
    gl_ref[...] = jnp.where(lane < MLSTM_HEADS, g,
                            jnp.where(lane < 2 * MLSTM_HEADS, _log_sigmoid(g), 0.0))
    gt = _softcap(lax.dot_general(wgt_ref[...], h, (((1,), (1,)), ((), ())),
                                  preferred_element_type=F32) + bgt_ref[...])
    sub = lax.broadcasted_iota(jnp.int32, gt.shape, 0)
    glt_ref[...] = jnp.where(sub < MLSTM_HEADS, gt, _log_sigmoid(gt))


def _mlstm_proj(x2d, mod, k_mod, w_in, b_gates, seq, tm):
    t, d = x2d.shape
    tiles_per_batch = seq // tm
    hk = MLSTM_HEADS * MLSTM_QK_DIM
    hv = MLSTM_HEADS * MLSTM_V_DIM
    ng = 2 * MLSTM_HEADS
    o1, o2, o3, o4 = hk, 2 * hk, 2 * hk + hv, 2 * hk + 2 * hv
    wq, wk, wv, wo = (w_in[:, a:b].astype(BF16) for a, b in ((0, o1), (o1, o2), (o2, o3), (o3, o4)))
    wg = jnp.concatenate([w_in[:, o4:], jnp.zeros((d, LANES - ng), F32)], axis=1).astype(BF16)
    wgt = w_in[:, o4:].T.astype(BF16)
    bg = jnp.concatenate([b_gates, jnp.zeros((LANES - ng,), F32)]).reshape(1, LANES)
    bgt = b_gates.reshape(ng, 1)
    tok = lambda n: pl.BlockSpec((tm, n), lambda i: (i, 0))
    est = 2 * d * (2 * hk + 2 * hv + LANES + ng) + 2 * tm * d * 4 + 2 * tm * (2 * hk + 2 * hv) * 2 \
        + 6 * tm * hv * 4
    return pl.pallas_call(
        _mlstm_proj_kernel,
        grid=(t // tm,),
        in_specs=[tok(d), _mod_spec(k_mod, 0, tiles_per_batch), _mod_spec(k_mod, 1, tiles_per_batch),
                  _resident(wq.shape), _resident(wk.shape), _resident(wv.shape), _resident(wo.shape),
                  _resident(wg.shape), _resident(wgt.shape), _resident((1, LANES)), _resident((ng, 1))],
        out_specs=[tok(hk), tok(hk), tok(hv), tok(hv), tok(LANES),
                   pl.BlockSpec((ng, tm), lambda i: (0, i))],
        out_shape=[jax.ShapeDtypeStruct((t, hk), BF16), jax.ShapeDtypeStruct((t, hk), BF16),
                   jax.ShapeDtypeStruct((t, hv), BF16), jax.ShapeDtypeStruct((t, hv), BF16),
                   jax.ShapeDtypeStruct((t, LANES), F32), jax.ShapeDtypeStruct((ng, t), F32)],
        compiler_params=pltpu.CompilerParams(
            dimension_semantics=("arbitrary",), vmem_limit_bytes=_vmem_limit(est)),
        name="mlstm_proj",
    )(x2d, mod, mod, wq, wk, wv, wo, wg, wgt, bg, bgt)


def _mlstm_scan_kernel(q_ref, k_ref, v_ref, og_ref, gl_ref, glt_ref, hn_ref, o_ref,
                       c_ref, n_ref, m_ref, *, chunk, heads_per_step):
    seq = q_ref.shape[0]
    L = chunk
    dk, dv = MLSTM_QK_DIM, MLSTM_V_DIM
    ng = 2 * MLSTM_HEADS
    head0 = pl.program_id(1) * heads_per_step
    hi = lax.Precision.HIGHEST

    c_ref[...] = jnp.zeros(c_ref.shape, F32)
    n_ref[...] = jnp.zeros(n_ref.shape, F32)
    m_ref[...] = jnp.zeros(m_ref.shape, F32)

    row = lax.broadcasted_iota(jnp.int32, (L, L), 0)
    col = lax.broadcasted_iota(jnp.int32, (L, L), 1)
    causal = col <= row
    tri = causal.astype(F32)
    tri_t = (row <= col).astype(F32)
    lane = lax.broadcasted_iota(jnp.int32, (L, LANES), 1)
    sub = lax.broadcasted_iota(jnp.int32, (ng, L), 0)

    def chunk_body(c, carry):
        r0 = pl.multiple_of(c * L, L)
        gl = gl_ref[pl.ds(r0, L), :]
        glt = glt_ref[:, pl.ds(r0, L)]
        for hh in range(heads_per_step):
            head = head0 + hh
            li_col = jnp.sum(jnp.where(lane == head, gl, 0.0), axis=-1, keepdims=True)
            lf_col = jnp.sum(jnp.where(lane == head + MLSTM_HEADS, gl, 0.0), axis=-1, keepdims=True)
            li_row = jnp.sum(jnp.where(sub == head, glt, 0.0), axis=0, keepdims=True)
            lf_row = jnp.sum(jnp.where(sub == head + MLSTM_HEADS, glt, 0.0), axis=0, keepdims=True)
            b_col = jnp.dot(tri, jnp.broadcast_to(lf_col, (L, LANES)), precision=hi,
                            preferred_element_type=F32)[:, 0:1]
            b_row = jnp.dot(jnp.broadcast_to(lf_row, (8, L)), tri_t, precision=hi,
                            preferred_element_type=F32)[0:1, :]
            m_prev = m_ref[hh]
            d = jnp.where(causal, b_col - b_row + li_row, NEG_INF)
            m_t = jnp.maximum(b_col + m_prev, jnp.max(d, axis=-1, keepdims=True))
            w = jnp.exp(d - m_t)
            inter = jnp.exp(b_col + m_prev - m_t)
            q = q_ref[pl.ds(r0, L), hh * dk:(hh + 1) * dk]
            k = k_ref[pl.ds(r0, L), hh * dk:(hh + 1) * dk]
            v = v_ref[pl.ds(r0, L), hh * dv:(hh + 1) * dv]
            qk = lax.dot_general(q, k, (((1,), (1,)), ((), ())), preferred_element_type=F32) * w
            c_state = c_ref[hh]
            n_state = n_ref[hh]
            num = inter * jnp.dot(q, c_state.astype(BF16), preferred_element_type=F32) \
                + jnp.dot(qk.astype(BF16), v, preferred_element_type=F32)
            den = inter * jnp.sum(q.astype(F32) * n_state, axis=-1, keepdims=True) \
                + jnp.sum(qk, axis=-1, keepdims=True)
            hs = num / jnp.maximum(jnp.abs(den), jnp.exp(-m_t))
            m_new = m_t[L - 1:L, :]
            b_last = b_col[L - 1:L, :]
            decay = jnp.exp(b_last + m_prev - m_new)
            ws = jnp.exp(b_last - b_col + li_col - m_new)
            kw = k.astype(F32) * ws
            c_ref[hh] = decay * c_state + jnp.dot(kw.T.astype(BF16), v, preferred_element_type=F32)
            n_ref[hh] = decay * n_state + jnp.sum(kw, axis=0, keepdims=True)
            m_ref[hh] = m_new
            y = hs * lax.rsqrt(jnp.mean(hs * hs, axis=-1, keepdims=True) + NORM_EPS) \
                * hn_ref[:, hh * dv:(hh + 1) * dv]
            o_ref[pl.ds(r0, L), hh * dv:(hh + 1) * dv] = (
                og_ref[pl.ds(r0, L), hh * dv:(hh + 1) * dv].astype(F32) * y).astype(BF16)
        return carry

    lax.fori_loop(0, seq // L, chunk_body, 0)


def _mlstm_scan(q, k, v, og, gl, glt, head_norm, batch, seq, chunk, heads_per_step):
    t = q.shape[0]
    hps = heads_per_step
    dk, dv = MLSTM_QK_DIM, MLSTM_V_DIM
    ng = 2 * MLSTM_HEADS
    est = 2 * seq * hps * (2 * dk + 3 * dv) * 2 + 2 * seq * LANES * 4 + 2 * ng * seq * 4 \
        + hps * dk * dv * 4 + 16 * chunk * chunk * 4
    return pl.pallas_call(
        functools.partial(_mlstm_scan_kernel, chunk=chunk, heads_per_step=hps),
        grid=(batch, MLSTM_HEADS // hps),
        in_specs=[pl.BlockSpec((seq, hps * dk), lambda b, g: (b, g)),
                  pl.BlockSpec((seq, hps * dk), lambda b, g: (b, g)),
                  pl.BlockSpec((seq, hps * dv), lambda b, g: (b, g)),
                  pl.BlockSpec((seq, hps * dv), lambda b, g: (b, g)),
                  pl.BlockSpec((seq, LANES), lambda b, g: (b, 0)),
                  pl.BlockSpec((ng, seq), lambda b, g: (0, b)),
                  pl.BlockSpec((1, hps * dv), lambda b, g: (0, g))],
        out_specs=pl.BlockSpec((seq, hps * dv), lambda b, g: (b, g)),
        out_shape=jax.ShapeDtypeStruct((t, MLSTM_HEADS * dv), BF16),
        scratch_shapes=[pltpu.VMEM((hps, dk, dv), F32), pltpu.VMEM((hps, 1, dk), F32),
                        pltpu.VMEM((hps, 1, 1), F32)],
        compiler_params=pltpu.CompilerParams(
            dimension_semantics=("arbitrary", "arbitrary"), vmem_limit_bytes=_vmem_limit(est)),
        name="mlstm_scan",
    )(q, k, v, og, gl, glt, head_norm.reshape(1, -1))


def kernel(x, c, positions, mla_w_in, mla_q_norm, mla_kv_norm, mla_w_uq, mla_w_ukv, mla_w_o, mlstm_w_in, mlstm_b_gates, mlstm_head_norm, mlstm_w_out, moe_w_router, moe_b_router, moe_w_gu, moe_b_gu, moe_w_down, moe_b_down, ada_w, ada_b, ln_g, ln_b):
    batch, seq, d = x.shape
    assert d == D_MODEL and ada_w.shape[0] == DEPTH
    cfg = _tiles(batch, seq)
    t = batch * seq
    tm = cfg["tm_tok"]
    mod = _adaln(c, ada_w, ada_b, cfg["tn_ada"])
    lng = ln_g.reshape(2 * DEPTH, d)
    lnb = ln_b.reshape(2 * DEPTH, d)
    xt = x.reshape(t, d)
    for i in range(DEPTH):
        j = i // 2
        k_mix, k_ffn = 2 * i, 2 * i + 1
        if i % 2 == 0:
            qn, qr, kn, v, kr = _mla_proj(xt, mod, k_mix, positions, mla_w_in[j], mla_q_norm[j],
                                          mla_kv_norm[j], mla_w_uq[j], mla_w_ukv[j], seq, tm)
            a = _attention(qn, qr, kn, kr, v, batch, seq, cfg["tq"])
            w_o = mla_w_o[j]
        else:
            q, k, v, og, gl, glt = _mlstm_proj(xt, mod, k_mix, mlstm_w_in[j], mlstm_b_gates[j], seq, tm)
            a = _mlstm_scan(q, k, v, og, gl, glt, mlstm_head_norm[j], batch, seq,
                            cfg["chunk"], cfg["heads_per_step"])
            w_o = mlstm_w_out[j]
        x1, h2, topi, topg = _post(a, xt, mod, k_mix, k_ffn, lng[k_mix], lnb[k_mix], w_o,
                                   moe_w_router[i], moe_b_router[i], seq, cfg["tm_post"])
        xt = _moe(h2, topi, topg, x1, mod, k_ffn, lng[k_ffn], lnb[k_ffn],
                  moe_w_gu, moe_b_gu, moe_w_down, moe_b_down, i, seq, cfg)
    return xt.reshape(batch, seq, d)
```

```python
import functools

import jax
import jax.numpy as jnp
from jax import lax
from jax.experimental import pallas as pl
from jax.experimental.pallas import tpu as pltpu

F32 = jnp.float32
BF16 = jnp.bfloat16

D_MODEL = 2048
DEPTH = 2

MLA_HEADS = 16
QK_NOPE_DIM = 128
QK_ROPE_DIM = 64
V_HEAD_DIM = 128
Q_LORA_RANK = 512
KV_LORA_RANK = 512
ROPE_THETA = 10000.0

MLSTM_HEADS = 8
MLSTM_QK_DIM = D_MODEL // (2 * MLSTM_HEADS)
MLSTM_V_DIM = D_MODEL // MLSTM_HEADS
GATE_SOFTCAP = 15.0

N_EXPERTS = 32
TOP_K = 4
D_EXPERT = D_MODEL
SWIGLU_ALPHA = 1.702
SWIGLU_LIMIT = 7.0

DEEPNORM_ALPHA = (2 * DEPTH) ** 0.25
NORM_EPS = 1e-6

LANES = 128
V7X_VMEM_BYTES = 64 * 1024 * 1024
NEG_INF = float("-inf")


def _vmem_limit(estimate_bytes):
    return int(min(estimate_bytes + (12 << 20), V7X_VMEM_BYTES - (8 << 20)))


def _tiles(batch, seq):
    t = batch * seq
    cfg = dict(
        tm_tok=min(256, seq),
        tm_post=min(512, seq),
        tq=min(256, seq),
        chunk=min(128, seq),
        heads_per_step=4,
        tm_moe=min(512, t * TOP_K // 8),
        tf=1024,
        tc_gemm=512,
        tn=512,
        tc=min(128, seq),
        tn_ada=768,
    )
    return cfg


def _adaln_kernel(c_ref, w_ref, b_ref, o_ref):
    c = c_ref[...]
    cond = c * jax.nn.sigmoid(c)
    o_ref[...] = jnp.dot(cond.astype(BF16), w_ref[...].astype(BF16),
                         preferred_element_type=F32) + b_ref[...]


def _adaln(c, ada_w, ada_b, tn):
    b, d = c.shape
    n_mod = ada_w.shape[0] * ada_w.shape[1]
    w = ada_w.reshape(n_mod, d, 3 * d)
    bias = ada_b.reshape(n_mod, 1, 3 * d)
    out = pl.pallas_call(
        _adaln_kernel,
        grid=(n_mod, 3 * d // tn),
        in_specs=[
            pl.BlockSpec((b, d), lambda i, j: (0, 0)),
            pl.BlockSpec((None, d, tn), lambda i, j: (i, 0, j)),
            pl.BlockSpec((None, 1, tn), lambda i, j: (i, 0, j)),
        ],
        out_specs=pl.BlockSpec((None, b, tn), lambda i, j: (i, 0, j)),
        out_shape=jax.ShapeDtypeStruct((n_mod, b, 3 * d), F32),
        compiler_params=pltpu.CompilerParams(
            dimension_semantics=("arbitrary", "arbitrary"),
            vmem_limit_bytes=_vmem_limit(2 * d * tn * 4 + d * tn * 2)),
        name="adaln",
    )(c, w, bias)
    return out.reshape(n_mod, b, 1, 3 * d)


def _mod_spec(k, part, tiles_per_batch):
    return pl.BlockSpec((None, None, 1, D_MODEL),
                        lambda i, *_: (k, i // tiles_per_batch, 0, part))


def _row_spec(k):
    return pl.BlockSpec((None, 1, D_MODEL), lambda i, *_: (k, 0, 0))


def _resident(shape):
    nd = len(shape)
    return pl.BlockSpec(shape, lambda *_: (0,) * nd, pipeline_mode=pl.Buffered(1))


def _mla_proj_kernel(x_ref, shift_ref, scale_ref, pos_ref, rope_ref, win_ref, qnorm_ref, kvnorm_ref,
                     wqn_ref, wqr_ref, wqs_ref, wkn_ref, wv_ref,
                     qn_ref, qr_ref, kn_ref, v_ref, kr_ref):
    h = x_ref[...] * (1.0 + scale_ref[...]) + shift_ref[...]
    lat = jnp.dot(h.astype(BF16), win_ref[...], preferred_element_type=F32)
    cq = lat[:, :Q_LORA_RANK]
    ckv = lat[:, Q_LORA_RANK:Q_LORA_RANK + KV_LORA_RANK]
    kr = lat[:, Q_LORA_RANK + KV_LORA_RANK:Q_LORA_RANK + KV_LORA_RANK + LANES]
    kr_sw = lat[:, Q_LORA_RANK + KV_LORA_RANK + LANES:]
    cq = cq * lax.rsqrt(jnp.mean(cq * cq, axis=-1, keepdims=True) + NORM_EPS) * qnorm_ref[...]
    ckv = ckv * lax.rsqrt(jnp.mean(ckv * ckv, axis=-1, keepdims=True) + NORM_EPS) * kvnorm_ref[...]
    cq = cq.astype(BF16)
    ckv = ckv.astype(BF16)

    ang = pos_ref[...].astype(F32) * rope_ref[0:1, :]
    cos_t = jnp.cos(ang) * rope_ref[1:2, :]
    sin_t = jnp.sin(ang) * rope_ref[2:3, :]

    qn_ref[...] = jnp.dot(cq, wqn_ref[...], preferred_element_type=F32).astype(BF16)
    q_rope = jnp.dot(cq, wqr_ref[...], preferred_element_type=F32)
    q_swap = jnp.dot(cq, wqs_ref[...], preferred_element_type=F32)
    cos_h = jnp.tile(cos_t, (1, MLA_HEADS))
    sin_h = jnp.tile(sin_t, (1, MLA_HEADS))
    qr_ref[...] = (q_rope * cos_h + q_swap * sin_h).astype(BF16)
    kn_ref[...] = jnp.dot(ckv, wkn_ref[...], preferred_element_type=F32).astype(BF16)
    v_ref[...] = jnp.dot(ckv, wv_ref[...], preferred_element_type=F32).astype(BF16)
    kr_ref[...] = (kr * cos_t + kr_sw * sin_t).astype(BF16)


def _rope_table():
    half = QK_ROPE_DIM // 2
    inv_freq = ROPE_THETA ** (-jnp.arange(0, QK_ROPE_DIM, 2, dtype=F32) / QK_ROPE_DIM)
    zeros = jnp.zeros((LANES - QK_ROPE_DIM,), F32)
    freq_row = jnp.concatenate([inv_freq, inv_freq, zeros])
    cos_mask = jnp.concatenate([jnp.ones((QK_ROPE_DIM,), F32), zeros])
    sin_sign = jnp.concatenate([-jnp.ones((half,), F32), jnp.ones((half,), F32), zeros])
    pad = jnp.zeros((5, LANES), F32)
    return jnp.concatenate([jnp.stack([freq_row, cos_mask, sin_sign]), pad], axis=0)


def _mla_weights(w_in, w_uq, w_ukv):
    d = w_in.shape[0]
    h = MLA_HEADS
    half = QK_ROPE_DIM // 2
    lat = Q_LORA_RANK + KV_LORA_RANK
    kr = w_in[:, lat:]
    zpad = jnp.zeros((d, LANES - QK_ROPE_DIM), w_in.dtype)
    win = jnp.concatenate([w_in[:, :lat], kr, zpad, kr[:, half:], kr[:, :half], zpad], axis=1)
    wq = w_uq.reshape(Q_LORA_RANK, h, QK_NOPE_DIM + QK_ROPE_DIM)
    wqn = wq[:, :, :QK_NOPE_DIM].reshape(Q_LORA_RANK, h * QK_NOPE_DIM)
    rope = wq[:, :, QK_NOPE_DIM:]
    zq = jnp.zeros((Q_LORA_RANK, h, LANES - QK_ROPE_DIM), w_uq.dtype)
    wqr = jnp.concatenate([rope, zq], axis=-1).reshape(Q_LORA_RANK, h * LANES)
    wqs = jnp.concatenate([rope[..., half:], rope[..., :half], zq], axis=-1).reshape(Q_LORA_RANK, h * LANES)
    wkv = w_ukv.reshape(KV_LORA_RANK, h, QK_NOPE_DIM + V_HEAD_DIM)
    wkn = wkv[:, :, :QK_NOPE_DIM].reshape(KV_LORA_RANK, h * QK_NOPE_DIM)
    wv = wkv[:, :, QK_NOPE_DIM:].reshape(KV_LORA_RANK, h * V_HEAD_DIM)
    return tuple(a.astype(BF16) for a in (win, wqn, wqr, wqs, wkn, wv))


def _mla_proj(x2d, mod, k_mod, positions, w_in, q_norm, kv_norm, w_uq, w_ukv, seq, tm):
    t, d = x2d.shape
    tiles_per_batch = seq // tm
    win, wqn, wqr, wqs, wkn, wv = _mla_weights(w_in, w_uq, w_ukv)
    hn = MLA_HEADS * LANES
    tok = lambda n: pl.BlockSpec((tm, n), lambda i: (i, 0))
    weights_bytes = 2 * (win.size + wqn.size + wqr.size + wqs.size + wkn.size + wv.size)
    est = weights_bytes + 2 * tm * d * 4 + 2 * (4 * tm * hn * 2 + tm * LANES * 2) + 8 * tm * hn * 4
    outs = pl.pallas_call(
        _mla_proj_kernel,
        grid=(t // tm,),
        in_specs=[
            tok(d),
            _mod_spec(k_mod, 0, tiles_per_batch),
            _mod_spec(k_mod, 1, tiles_per_batch),
            pl.BlockSpec((tm, 1), lambda i: (i, 0)),
            _resident((8, LANES)),
            _resident(win.shape),
            _resident((1, Q_LORA_RANK)),
            _resident((1, KV_LORA_RANK)),
            _resident(wqn.shape), _resident(wqr.shape), _resident(wqs.shape),
            _resident(wkn.shape), _resident(wv.shape),
        ],
        out_specs=[tok(hn), tok(hn), tok(hn), tok(hn), tok(LANES)],
        out_shape=[jax.ShapeDtypeStruct((t, hn), BF16)] * 4 + [jax.ShapeDtypeStruct((t, LANES), BF16)],
        compiler_params=pltpu.CompilerParams(
            dimension_semantics=("arbitrary",), vmem_limit_bytes=_vmem_limit(est)),
        name="mla_proj",
    )(x2d, mod, mod, positions.reshape(t, 1), _rope_table(), win,
      q_norm.reshape(1, -1), kv_norm.reshape(1, -1), wqn, wqr, wqs, wkn, wv)
    return outs


def _attn_kernel(qn_ref, qr_ref, kn_ref, kr_ref, v_ref, o_ref, s_ref, *, tq, scale):
    seq = qn_ref.shape[0]
    row = lax.broadcasted_iota(jnp.int32, (tq, tq), 0)
    col = lax.broadcasted_iota(jnp.int32, (tq, tq), 1)
    causal = col <= row
    exp2_scale = scale * 1.4426950408889634

    def fold_lanes(a):
        return [a[:, g * LANES:(g + 1) * LANES] for g in range(tq // LANES)]

    for qi in range(seq // tq):
        rows = slice(qi * tq, (qi + 1) * tq)
        q = jnp.concatenate([qn_ref[rows, :], qr_ref[rows, :]], axis=-1)
        m_acc = jnp.full((tq, LANES), NEG_INF, F32)
        for j in range(qi + 1):
            cols = slice(j * tq, (j + 1) * tq)
            k = jnp.concatenate([kn_ref[cols, :], kr_ref[cols, :]], axis=-1)
            s = lax.dot_general(q, k, (((1,), (1,)), ((), ())), preferred_element_type=F32)
            if j == qi:
                s = jnp.where(causal, s, NEG_INF)
            s_ref[:, cols] = s
            for part in fold_lanes(s):
                m_acc = jnp.maximum(m_acc, part)
        m = jnp.max(m_acc, axis=-1, keepdims=True)
        l_acc = jnp.zeros((tq, LANES), F32)
        acc = jnp.zeros((tq, V_HEAD_DIM), F32)
        for j in range(qi + 1):
            cols = slice(j * tq, (j + 1) * tq)
            p = jnp.exp2((s_ref[:, cols] - m) * exp2_scale)
            for part in fold_lanes(p):
                l_acc = l_acc + part
            acc = acc + jnp.dot(p.astype(BF16), v_ref[cols, :], preferred_element_type=F32)
        l = jnp.sum(l_acc, axis=-1, keepdims=True)
        o_ref[rows, :] = (acc / l).astype(BF16)


def _attention(qn, qr, kn, kr, v, batch, seq, tq):
    t = qn.shape[0]
    head = lambda: pl.BlockSpec((seq, LANES), lambda b, h: (b, h))
    scale = (QK_NOPE_DIM + QK_ROPE_DIM) ** -0.5
    est = 2 * 6 * seq * LANES * 2 + tq * seq * 4 + 6 * tq * tq * 4
    return pl.pallas_call(
        functools.partial(_attn_kernel, tq=tq, scale=scale),
        grid=(batch, MLA_HEADS),
        in_specs=[head(), head(), head(), pl.BlockSpec((seq, LANES), lambda b, h: (b, 0)), head()],
        out_specs=head(),
        out_shape=jax.ShapeDtypeStruct((t, MLA_HEADS * V_HEAD_DIM), BF16),
        scratch_shapes=[pltpu.VMEM((tq, seq), F32)],
        compiler_params=pltpu.CompilerParams(
            dimension_semantics=("arbitrary", "arbitrary"), vmem_limit_bytes=_vmem_limit(est)),
        name="mla_attention",
    )(qn, qr, kn, kr, v)


SLABS = D_MODEL // LANES


PACKED_SLABS = SLABS // 2


def _store_slabs(ref, val, first_slab=0, first_row=0, pitch=SLABS):
    rows = val.shape[0]
    for s in range(val.shape[1] // LANES):
        ref[pl.ds(first_row * pitch + first_slab + s, rows, stride=pitch), :] = val[:, s * LANES:(s + 1) * LANES]


def _pack_bf16_pairs(hi, lo):
    hi = pltpu.bitcast(hi.astype(BF16).astype(F32), jnp.uint32)
    lo = pltpu.bitcast(lo.astype(BF16).astype(F32), jnp.uint32)
    return hi | (lo >> 16)


def _unpack_bf16_pairs(u):
    return pltpu.bitcast(u & jnp.uint32(0xFFFF0000), F32), pltpu.bitcast(u << 16, F32)


def _layer_norm(z, g, b):
    mu = jnp.mean(z, axis=-1, keepdims=True)
    zc = z - mu
    var = jnp.mean(zc * zc, axis=-1, keepdims=True)
    return zc * lax.rsqrt(var + NORM_EPS) * g + b


def _post_kernel(a_ref, x_ref, gate_ref, lng_ref, lnb_ref, shift_ref, scale_ref, wo_ref, wr_ref, br_ref,
                 x1_ref, h2_ref, topi_ref, topg_ref, *, sub):
    starts = range(0, a_ref.shape[0], sub)
    ys = [jnp.dot(a_ref[r0:r0 + sub, :], wo_ref[...], preferred_element_type=F32) for r0 in starts]
    for y, r0 in zip(ys, starts):
        _post_rows(slice(r0, r0 + sub), r0, y, x_ref, gate_ref, lng_ref, lnb_ref, shift_ref, scale_ref,
                   wr_ref, br_ref, x1_ref, h2_ref, topi_ref, topg_ref)


def _post_rows(rows, r0, y, x_ref, gate_ref, lng_ref, lnb_ref, shift_ref, scale_ref, wr_ref,
               br_ref, x1_ref, h2_ref, topi_ref, topg_ref):
    z = DEEPNORM_ALPHA * x_ref[rows, :] + gate_ref[...] * y
    x1 = _layer_norm(z, lng_ref[...], lnb_ref[...])
    x1_ref[rows, :] = x1
    h2 = x1 * (1.0 + scale_ref[...]) + shift_ref[...]
    half = h2.shape[1] // 2
    _store_slabs(h2_ref, _pack_bf16_pairs(h2[:, :half], h2[:, half:]), first_row=r0, pitch=PACKED_SLABS)
    h_hi = h2.astype(BF16)
    h_lo = (h2 - h_hi.astype(F32)).astype(BF16)
    t_hi = jnp.dot(h_hi, wr_ref[...], preferred_element_type=F32)
    t_lo = jnp.dot(h_lo, wr_ref[:, :LANES], preferred_element_type=F32)
    logits = t_hi[:, :LANES] + (t_hi[:, LANES:] + t_lo) + br_ref[...]
    tm = logits.shape[0]
    lane = lax.broadcasted_iota(jnp.int32, (tm, LANES), 1)
    vals, idxs = [], []
    for _ in range(TOP_K):
        mk = jnp.max(logits, axis=-1, keepdims=True)
        ik = jnp.min(jnp.where(logits == mk, lane, LANES), axis=-1, keepdims=True)
        vals.append(mk)
        idxs.append(ik)
        logits = jnp.where(lane == ik, NEG_INF, logits)
    exps = [jnp.exp(v - vals[0]) for v in vals]
    den = exps[0]
    for e in exps[1:]:
        den = den + e
    topi = jnp.zeros((tm, LANES), jnp.int32)
    topg = jnp.zeros((tm, LANES), F32)
    for k in range(TOP_K):
        topi = jnp.where(lane == k, idxs[k], topi)
        topg = jnp.where(lane == k, exps[k] / den, topg)
    topi_ref[rows, :] = topi
    topg_ref[rows, :] = topg


def _post(a, x2d, mod, k_mix, k_ffn, ln_g, ln_b, w_o, w_router, b_router, seq, tm):
    t, d = x2d.shape
    tiles_per_batch = seq // tm
    wr = jnp.concatenate([w_router, jnp.zeros((d, LANES - N_EXPERTS), F32)], axis=1)
    wr_hi = wr.astype(BF16)
    wr_lo = (wr - wr_hi.astype(F32)).astype(BF16)
    wr = jnp.concatenate([wr_hi, wr_lo], axis=1)
    br = jnp.concatenate([b_router, jnp.full((LANES - N_EXPERTS,), NEG_INF, F32)]).reshape(1, LANES)
    tok = lambda n: pl.BlockSpec((tm, n), lambda i: (i, 0))
    sub = min(256, tm)
    est = d * d * 2 + d * LANES * 4 + 2 * tm * d * (2 + 4 + 4 + 4) + 6 * sub * d * 4
    return pl.pallas_call(
        functools.partial(_post_kernel, sub=sub),
        grid=(t // tm,),
        in_specs=[
            tok(d), tok(d),
            _mod_spec(k_mix, 2, tiles_per_batch),
            _resident((1, d)), _resident((1, d)),
            _mod_spec(k_ffn, 0, tiles_per_batch),
            _mod_spec(k_ffn, 1, tiles_per_batch),
            _resident((d, d)), _resident((d, 2 * LANES)), _resident((1, LANES)),
        ],
        out_specs=[tok(d), pl.BlockSpec((tm * PACKED_SLABS, LANES), lambda i: (i, 0)), tok(LANES), tok(LANES)],
        out_shape=[jax.ShapeDtypeStruct((t, d), F32), jax.ShapeDtypeStruct((t * PACKED_SLABS, LANES), jnp.uint32),
                   jax.ShapeDtypeStruct((t, LANES), jnp.int32), jax.ShapeDtypeStruct((t, LANES), F32)],
        compiler_params=pltpu.CompilerParams(
            dimension_semantics=("arbitrary",), vmem_limit_bytes=_vmem_limit(est)),
        name="mixer_out_ln_router",
    )(a, x2d, mod, ln_g.reshape(1, d), ln_b.reshape(1, d), mod, mod, w_o.astype(BF16), wr, br)


def _route(topi, tm_moe):
    t = topi.shape[0]
    m = t * TOP_K
    e_tk = topi[:, :TOP_K]
    a = jnp.sum((e_tk[:, :, None] == jnp.arange(N_EXPERTS, dtype=jnp.int32)).astype(F32), axis=1)
    blk = min(256, t)
    ab = a.reshape(t // blk, blk, N_EXPERTS)
    tri = jnp.tril(jnp.ones((blk, blk), F32))
    within = jnp.einsum("ts,bse->bte", tri, ab)
    btot = within[:, -1, :]
    boff = jnp.cumsum(btot, axis=0) - btot
    excl = (within + boff[:, None, :] - ab).reshape(t, N_EXPERTS)
    counts = (boff[-1] + btot[-1]).astype(jnp.int32)
    rank = jnp.take_along_axis(excl, e_tk, axis=1).astype(jnp.int32)
    padded = ((counts + tm_moe - 1) // tm_moe) * tm_moe
    pends = jnp.cumsum(padded)
    pstarts = pends - padded
    dest = (pstarts[e_tk] + rank).astype(jnp.int32).reshape(m)
    n_tiles = m // tm_moe + N_EXPERTS
    row_tok = jnp.zeros((n_tiles * tm_moe,), jnp.int32).at[dest].set(
        jnp.arange(m, dtype=jnp.int32) // TOP_K, unique_indices=True)
    tile_start = (pstarts // tm_moe).astype(jnp.int32)
    tile_count = (padded // tm_moe).astype(jnp.int32)
    n_used = (pends[-1] // tm_moe).astype(jnp.int32).reshape(1)
    return dest, row_tok, tile_start, tile_count, n_used


def _gather_issue(idx_ref, src_hbm, buf, sem, slot, n_rows):
    def body(i, carry):
        for u in range(2):
            r = 2 * i + u
            pltpu.make_async_copy(src_hbm.at[idx_ref[0, 0, r]], buf.at[slot, :, r, :],
                                  sem.at[slot]).start(priority=u)
        return carry
    lax.fori_loop(0, n_rows // 2, body, 0, unroll=4)


def _gather_wait(buf, sem, slot):
    pltpu.make_async_copy(buf.at[slot], buf.at[slot], sem.at[slot]).wait()


def _dispatch_kernel(ns_ref, idx_ref, nxt_ref, src_hbm, o_ref, buf, sem, *, n_rows):
    i = pl.program_id(0)
    n_active = ns_ref[0]
    slot = lax.rem(i, 2)

    @pl.when(jnp.logical_and(i == 0, n_active > 0))
    def _():
        _gather_issue(idx_ref, src_hbm, buf, sem, 0, n_rows)

    @pl.when(i + 1 < n_active)
    def _():
        _gather_issue(nxt_ref, src_hbm, buf, sem, 1 - slot, n_rows)

    @pl.when(i < n_active)
    def _():
        _gather_wait(buf, sem, slot)
        hi, lo = _unpack_bf16_pairs(buf[slot])
        o_ref[0:PACKED_SLABS] = hi.astype(BF16)
        o_ref[PACKED_SLABS:SLABS] = lo.astype(BF16)

    @pl.when(i >= n_active)
    def _():
        o_ref[...] = jnp.zeros(o_ref.shape, o_ref.dtype)


def _dispatch(h2_packed, row_tok, n_used, r):
    t, packed_slabs, _ = h2_packed.shape
    slabs = 2 * packed_slabs
    n = row_tok.shape[0] // r
    idx = row_tok.reshape(n, 1, r)
    smem = lambda f: pl.BlockSpec((1, 1, r), f, memory_space=pltpu.SMEM)
    return pl.pallas_call(
        functools.partial(_dispatch_kernel, n_rows=r),
        grid_spec=pltpu.PrefetchScalarGridSpec(
            num_scalar_prefetch=1,
            grid=(n,),
            in_specs=[smem(lambda i, ns: (i, 0, 0)),
                      smem(lambda i, ns: (jnp.minimum(i + 1, n - 1), 0, 0)),
                      pl.BlockSpec(memory_space=pl.ANY)],
            out_specs=pl.BlockSpec((None, slabs, r, LANES), lambda i, ns: (i, 0, 0, 0)),
            scratch_shapes=[pltpu.VMEM((2, packed_slabs, r, LANES), jnp.uint32),
                            pltpu.SemaphoreType.DMA((2,))]),
        out_shape=jax.ShapeDtypeStruct((n, slabs, r, LANES), BF16),
        compiler_params=pltpu.CompilerParams(
            dimension_semantics=("arbitrary",),
            vmem_limit_bytes=_vmem_limit(4 * r * slabs * LANES * 2)),
        name="moe_dispatch_gather",
    )(n_used, idx, idx, h2_packed)


def _expert_weights(step, n_steps, copies_for, stage, w_bf):
    @pl.when(step == 0)
    def _():
        for c in copies_for(step):
            c.start(priority=1)

    for c in copies_for(step):
        c.wait()
    w_bf[...] = stage[...].astype(BF16)

    @pl.when(step + 1 < n_steps)
    def _():
        for c in copies_for(step + 1):
            c.start(priority=1)


def _start_first_tile(n_tiles, in_copies):
    @pl.when(n_tiles > 0)
    def _():
        for c in in_copies(0, 0):
            c.start()


def _tile_loop(n_tiles, in_copies, out_copy, compute):
    def start_in(k, slot):
        for c in in_copies(k, slot):
            c.start()

    def tile(k, slot):
        for c in in_copies(k, slot):
            c.wait()

        @pl.when(k + 1 < n_tiles)
        def _():
            start_in(k + 1, 1 - slot)

        @pl.when(k >= 2)
        def _():
            out_copy(k - 2, slot).wait()

        compute(slot)
        out_copy(k, slot).start()

    def pair(p, carry):
        tile(2 * p, 0)

        @pl.when(2 * p + 1 < n_tiles)
        def _():
            tile(2 * p + 1, 1)
        return carry

    lax.fori_loop(0, (n_tiles + 1) // 2, pair, 0)

    for slot in (0, 1):
        @pl.when(jnp.logical_and(n_tiles >= 2, lax.rem(n_tiles, 2) == slot))
        def _():
            out_copy(n_tiles - 2, slot).wait()

        @pl.when(jnp.logical_and(n_tiles >= 1, lax.rem(n_tiles + 1, 2) == slot))
        def _():
            out_copy(n_tiles - 1, slot).wait()


def _zero_tail_tiles(first, n_total, zero_src, dst_for, sem):
    def body(t, carry):
        cp = pltpu.make_async_copy(zero_src, dst_for(t), sem)
        cp.start()
        cp.wait()
        return carry
    lax.fori_loop(first, n_total, body, 0)


def _gemm1_kernel(ts_ref, nt_ref, nu_ref, bg_ref, bu_ref, w_hbm, x_hbm, o_hbm,
                  stage, w_bf, xbuf, obuf, wsem, xsem, osem, *, layer, tf, tc):
    j = pl.program_id(0)
    e = pl.program_id(1)
    n_e = pl.num_programs(1)
    f = D_EXPERT

    def weight_copies(step):
        jj = step // n_e
        ee = step - jj * n_e
        return [pltpu.make_async_copy(
            w_hbm.at[layer, ee, :, pl.ds(pl.multiple_of(half * f + jj * tf, tf), tf)],
            stage.at[half], wsem.at[half]) for half in range(2)]

    t0 = ts_ref[e]

    def x_copies(k, slot):
        return [pltpu.make_async_copy(x_hbm.at[t0 + k], xbuf.at[slot], xsem.at[slot])]

    _start_first_tile(nt_ref[e], x_copies)
    _expert_weights(j * n_e + e, pl.num_programs(0) * n_e, weight_copies, stage, w_bf)

    def o_copy(k, slot):
        return pltpu.make_async_copy(obuf.at[slot], o_hbm.at[j, t0 + k], osem.at[slot])

    def compute(slot):
        x = jnp.concatenate([xbuf[slot, s] for s in range(xbuf.shape[1])], axis=-1)
        for c in range(tf // tc):
            cols = slice(c * tc, (c + 1) * tc)
            g = jnp.dot(x, w_bf[0, :, cols], preferred_element_type=F32) + bg_ref[:, cols]
            u = jnp.dot(x, w_bf[1, :, cols], preferred_element_type=F32) + bu_ref[:, cols]
            g = jnp.minimum(g, SWIGLU_LIMIT)
            u = jnp.clip(u, -SWIGLU_LIMIT, SWIGLU_LIMIT)
            obuf[slot, :, cols] = (g * jax.nn.sigmoid(SWIGLU_ALPHA * g) * (u + 1.0)).astype(BF16)

    _tile_loop(nt_ref[e], x_copies, o_copy, compute)

    @pl.when(e == n_e - 1)
    def _():
        obuf[0] = jnp.zeros(obuf.shape[1:], obuf.dtype)
        _zero_tail_tiles(nu_ref[0], o_hbm.shape[1], obuf.at[0], lambda t: o_hbm.at[j, t], osem.at[0])


def _gemm1(x_rows, tile_start, tile_count, n_used, w_gu, b_gu, layer, tf, tc):
    n_tiles, slabs, tm, _ = x_rows.shape
    d = slabs * LANES
    f = D_EXPERT
    nf = f // tf
    bias = b_gu.reshape(b_gu.shape[0], N_EXPERTS, 1, 2 * f)
    bspec = lambda off: pl.BlockSpec((None, None, 1, tf), lambda j, e, ts, nt, nu: (layer, e, 0, j + off))
    any_spec = pl.BlockSpec(memory_space=pl.ANY)
    est = 2 * d * tf * (4 + 2) + 2 * tm * d * 2 + 2 * tm * tf * 2 + 6 * tm * tc * 4
    return pl.pallas_call(
        functools.partial(_gemm1_kernel, layer=layer, tf=tf, tc=tc),
        grid_spec=pltpu.PrefetchScalarGridSpec(
            num_scalar_prefetch=3,
            grid=(nf, N_EXPERTS),
            in_specs=[bspec(0), bspec(nf), any_spec, any_spec],
            out_specs=any_spec,
            scratch_shapes=[pltpu.VMEM((2, d, tf), F32), pltpu.VMEM((2, d, tf), BF16),
                            pltpu.VMEM((2, slabs, tm, LANES), BF16), pltpu.VMEM((2, tm, tf), BF16),
                            pltpu.SemaphoreType.DMA((2,)), pltpu.SemaphoreType.DMA((2,)),
                            pltpu.SemaphoreType.DMA((2,))]),
        out_shape=jax.ShapeDtypeStruct((nf, n_tiles, tm, tf), BF16),
        compiler_params=pltpu.CompilerParams(
            dimension_semantics=("arbitrary", "arbitrary"), vmem_limit_bytes=_vmem_limit(est)),
        name="moe_gate_up_swiglu",
    )(tile_start, tile_count, n_used, bias, bias, w_gu, x_rows)


def _gemm2_kernel(ts_ref, nt_ref, nu_ref, b_ref, w_hbm, a_hbm, o_hbm,
                  stage, w_bf, abuf, obuf, wsem, asem, osem, *, layer, tn):
    e = pl.program_id(0)
    nf, tm = abuf.shape[1], abuf.shape[2]
    tile_rows = tm * PACKED_SLABS

    def weight_copies(step):
        return [pltpu.make_async_copy(w_hbm.at[layer, step], stage, wsem.at[0])]

    t0 = ts_ref[e]

    def a_copies(k, slot):
        return [pltpu.make_async_copy(a_hbm.at[c, t0 + k], abuf.at[slot, c], asem.at[slot, c])
                for c in range(nf)]

    _start_first_tile(nt_ref[e], a_copies)
    _expert_weights(e, pl.num_programs(0), weight_copies, stage, w_bf)

    def o_tile(t):
        return o_hbm.at[pl.ds(pl.multiple_of(t * tile_rows, tile_rows), tile_rows), :]

    def o_copy(k, slot):
        return pltpu.make_async_copy(obuf.at[pl.ds(slot * tile_rows, tile_rows), :],
                                     o_tile(t0 + k), osem.at[slot])

    def compute(slot):
        a = jnp.concatenate([abuf[slot, c] for c in range(nf)], axis=-1)
        half = w_bf.shape[1] // 2
        for c in range(half // tn):
            res = [jnp.dot(a, w_bf[:, cols], preferred_element_type=F32) + b_ref[:, cols]
                   for cols in (slice(c * tn, (c + 1) * tn), slice(half + c * tn, half + (c + 1) * tn))]
            _store_slabs(obuf, _pack_bf16_pairs(*res), first_slab=c * tn // LANES, first_row=slot * tm,
                         pitch=PACKED_SLABS)

    _tile_loop(nt_ref[e], a_copies, o_copy, compute)

    @pl.when(e == pl.num_programs(0) - 1)
    def _():
        obuf[0:tile_rows, :] = jnp.zeros((tile_rows, LANES), obuf.dtype)
        _zero_tail_tiles(nu_ref[0], o_hbm.shape[0] // tile_rows, obuf.at[0:tile_rows, :], o_tile, osem.at[0])


def _gemm2(act, tile_start, tile_count, n_used, w_down, b_down, layer, tn):
    nf, n_tiles, tm, tf = act.shape
    f = nf * tf
    d = D_MODEL
    bias = b_down.reshape(b_down.shape[0], N_EXPERTS, 1, d)
    any_spec = pl.BlockSpec(memory_space=pl.ANY)
    est = f * d * (4 + 2) + 2 * tm * f * 2 + 2 * tm * d * 2 + 6 * tm * tn * 4
    return pl.pallas_call(
        functools.partial(_gemm2_kernel, layer=layer, tn=tn),
        grid_spec=pltpu.PrefetchScalarGridSpec(
            num_scalar_prefetch=3,
            grid=(N_EXPERTS,),
            in_specs=[pl.BlockSpec((None, None, 1, d), lambda e, ts, nt, nu: (layer, e, 0, 0)),
                      any_spec, any_spec],
            out_specs=any_spec,
            scratch_shapes=[pltpu.VMEM((f, d), F32), pltpu.VMEM((f, d), BF16),
                            pltpu.VMEM((2, nf, tm, tf), BF16),
                            pltpu.VMEM((2 * tm * PACKED_SLABS, LANES), jnp.uint32),
                            pltpu.SemaphoreType.DMA((1,)), pltpu.SemaphoreType.DMA((2, nf)),
                            pltpu.SemaphoreType.DMA((2,))]),
        out_shape=jax.ShapeDtypeStruct((n_tiles * tm * PACKED_SLABS, LANES), jnp.uint32),
        compiler_params=pltpu.CompilerParams(
            dimension_semantics=("arbitrary",), vmem_limit_bytes=_vmem_limit(est)),
        name="moe_down",
    )(tile_start, tile_count, n_used, bias, w_down, act)


def _combine_kernel(idx_ref, nxt_ref, rows_hbm, g_ref, x_ref, gate_ref, lng_ref, lnb_ref, o_ref,
                    buf, sem, *, tc):
    i = pl.program_id(0)
    n = pl.num_programs(0)
    slot = lax.rem(i, 2)
    n_rows = TOP_K * tc

    @pl.when(i == 0)
    def _():
        _gather_issue(idx_ref, rows_hbm, buf, sem, 0, n_rows)

    @pl.when(i + 1 < n)
    def _():
        _gather_issue(nxt_ref, rows_hbm, buf, sem, 1 - slot, n_rows)

    _gather_wait(buf, sem, slot)
    g = g_ref[...]
    gk = [jnp.broadcast_to(g[:, k:k + 1], (tc, LANES)) for k in range(TOP_K)]
    hi_parts, lo_parts = [], []
    for s in range(buf.shape[1]):
        acc_hi = acc_lo = None
        for k in range(TOP_K):
            hi, lo = _unpack_bf16_pairs(buf[slot, s, pl.ds(k * tc, tc), :])
            acc_hi = gk[k] * hi if acc_hi is None else acc_hi + gk[k] * hi
            acc_lo = gk[k] * lo if acc_lo is None else acc_lo + gk[k] * lo
        hi_parts.append(acc_hi)
        lo_parts.append(acc_lo)
    y = jnp.concatenate(hi_parts + lo_parts, axis=-1)
    z = DEEPNORM_ALPHA * x_ref[...] + gate_ref[...] * y
    o_ref[...] = _layer_norm(z, lng_ref[...], lnb_ref[...])


def _combine(out_rows, dest, topg, x1, mod, k_ffn, ln_g, ln_b, seq, tc):
    t, d = x1.shape
    n = t // tc
    tiles_per_batch = seq // tc
    idx = dest.reshape(n, tc, TOP_K).transpose(0, 2, 1).reshape(n, 1, TOP_K * tc)
    smem = lambda f: pl.BlockSpec((1, 1, TOP_K * tc), f, memory_space=pltpu.SMEM)
    tok = lambda w: pl.BlockSpec((tc, w), lambda i: (i, 0))
    return pl.pallas_call(
        functools.partial(_combine_kernel, tc=tc),
        grid=(n,),
        in_specs=[smem(lambda i: (i, 0, 0)),
                  smem(lambda i: (jnp.minimum(i + 1, n - 1), 0, 0)),
                  pl.BlockSpec(memory_space=pl.ANY),
                  tok(LANES), tok(d),
                  _mod_spec(k_ffn, 2, tiles_per_batch),
                  _resident((1, d)), _resident((1, d))],
        out_specs=tok(d),
        out_shape=jax.ShapeDtypeStruct((t, d), F32),
        scratch_shapes=[pltpu.VMEM((2, PACKED_SLABS, TOP_K * tc, LANES), jnp.uint32),
                        pltpu.SemaphoreType.DMA((2,))],
        compiler_params=pltpu.CompilerParams(
            dimension_semantics=("arbitrary",),
            vmem_limit_bytes=_vmem_limit(2 * TOP_K * tc * d * 4 + 6 * tc * d * 4)),
        name="moe_combine_ln",
    )(idx, idx, out_rows, topg, x1, mod, ln_g.reshape(1, d), ln_b.reshape(1, d))


def _moe(h2, topi, topg, x1, mod, k_ffn, ln_g, ln_b, w_gu, b_gu, w_down, b_down, layer, seq, cfg):
    dest, row_tok, tile_start, tile_count, n_used = _route(topi, cfg["tm_moe"])
    x_rows = _dispatch(h2.reshape(-1, PACKED_SLABS, LANES), row_tok, n_used, cfg["tm_moe"])
    act = _gemm1(x_rows, tile_start, tile_count, n_used, w_gu, b_gu, layer, cfg["tf"], cfg["tc_gemm"])
    out_rows = _gemm2(act, tile_start, tile_count, n_used, w_down, b_down, layer, cfg["tn"])
    out_rows = out_rows.reshape(-1, PACKED_SLABS, LANES)
    return _combine(out_rows, dest, topg, x1, mod, k_ffn, ln_g, ln_b, seq, cfg["tc"])


def _softcap(g):
    return GATE_SOFTCAP * jnp.tanh(g / GATE_SOFTCAP)


def _log_sigmoid(x):
    return jnp.minimum(x, 0.0) - jnp.log(1.0 + jnp.exp(-jnp.abs(x)))


def _mlstm_proj_kernel(x_ref, shift_ref, scale_ref, wq_ref, wk_ref, wv_ref, wo_ref, wg_ref, wgt_ref,
                       bg_ref, bgt_ref, q_ref, k_ref, v_ref, og_ref, gl_ref, glt_ref):
    h = (x_ref[...] * (1.0 + scale_ref[...]) + shift_ref[...]).astype(BF16)
    q_ref[...] = (jnp.dot(h, wq_ref[...], preferred_element_type=F32) * MLSTM_QK_DIM ** -0.5).astype(BF16)
    k_ref[...] = jnp.dot(h, wk_ref[...], preferred_element_type=F32).astype(BF16)
    v_ref[...] = jnp.dot(h, wv_ref[...], preferred_element_type=F32).astype(BF16)
    og_ref[...] = jax.nn.sigmoid(jnp.dot(h, wo_ref[...], preferred_element_type=F32)).astype(BF16)
    g = _softcap(jnp.dot(h, wg_ref[...], preferred_element_type=F32) + bg_ref[...])
    lane = lax.broadcasted_iota(jnp.int32, g.shape, 1)
    gl_ref[...] = jnp.where(lane < MLSTM_HEADS, g,
                            jnp.where(lane < 2 * MLSTM_HEADS, _log_sigmoid(g), 0.0))
    gt = _softcap(lax.dot_general(wgt_ref[...], h, (((1,), (1,)), ((), ())),
                                  preferred_element_type=F32) + bgt_ref[...])
    sub = lax.broadcasted_iota(jnp.int32, gt.shape, 0)
    glt_ref[...] = jnp.where(sub < MLSTM_HEADS, gt, _log_sigmoid(gt))


def _mlstm_proj(x2d, mod, k_mod, w_in, b_gates, seq, tm):
    t, d = x2d.shape
    tiles_per_batch = seq // tm
    hk = MLSTM_HEADS * MLSTM_QK_DIM
    hv = MLSTM_HEADS * MLSTM_V_DIM
    ng = 2 * MLSTM_HEADS
    o1, o2, o3, o4 = hk, 2 * hk, 2 * hk + hv, 2 * hk + 2 * hv
    wq, wk, wv, wo = (w_in[:, a:b].astype(BF16) for a, b in ((0, o1), (o1, o2), (o2, o3), (o3, o4)))
    wg = jnp.concatenate([w_in[:, o4:], jnp.zeros((d, LANES - ng), F32)], axis=1).astype(BF16)
    wgt = w_in[:, o4:].T.astype(BF16)
    bg = jnp.concatenate([b_gates, jnp.zeros((LANES - ng,), F32)]).reshape(1, LANES)
    bgt = b_gates.reshape(ng, 1)
    tok = lambda n: pl.BlockSpec((tm, n), lambda i: (i, 0))
    est = 2 * d * (2 * hk + 2 * hv + LANES + ng) + 2 * tm * d * 4 + 2 * tm * (2 * hk + 2 * hv) * 2 \
        + 6 * tm * hv * 4
    return pl.pallas_call(
        _mlstm_proj_kernel,
        grid=(t // tm,),
        in_specs=[tok(d), _mod_spec(k_mod, 0, tiles_per_batch), _mod_spec(k_mod, 1, tiles_per_batch),
                  _resident(wq.shape), _resident(wk.shape), _resident(wv.shape), _resident(wo.shape),
                  _resident(wg.shape), _resident(wgt.shape), _resident((1, LANES)), _resident((ng, 1))],
        out_specs=[tok(hk), tok(hk), tok(hv), tok(hv), tok(LANES),
                   pl.BlockSpec((ng, tm), lambda i: (0, i))],
        out_shape=[jax.ShapeDtypeStruct((t, hk), BF16), jax.ShapeDtypeStruct((t, hk), BF16),
                   jax.ShapeDtypeStruct((t, hv), BF16), jax.ShapeDtypeStruct((t, hv), BF16),
                   jax.ShapeDtypeStruct((t, LANES), F32), jax.ShapeDtypeStruct((ng, t), F32)],
        compiler_params=pltpu.CompilerParams(
            dimension_semantics=("arbitrary",), vmem_limit_bytes=_vmem_limit(est)),
        name="mlstm_proj",
    )(x2d, mod, mod, wq, wk, wv, wo, wg, wgt, bg, bgt)


def _mlstm_scan_kernel(q_ref, k_ref, v_ref, og_ref, gl_ref, glt_ref, hn_ref, o_ref,
                       c_ref, n_ref, m_ref, *, chunk, heads_per_step):
    seq = q_ref.shape[0]
    L = chunk
    dk, dv = MLSTM_QK_DIM, MLSTM_V_DIM
    ng = 2 * MLSTM_HEADS
    head0 = pl.program_id(1) * heads_per_step
    hi = lax.Precision.HIGHEST

    c_ref[...] = jnp.zeros(c_ref.shape, F32)
    n_ref[...] = jnp.zeros(n_ref.shape, F32)
    m_ref[...] = jnp.zeros(m_ref.shape, F32)

    row = lax.broadcasted_iota(jnp.int32, (L, L), 0)
    col = lax.broadcasted_iota(jnp.int32, (L, L), 1)
    causal = col <= row
    tri = causal.astype(F32)
    tri_t = (row <= col).astype(F32)
    lane = lax.broadcasted_iota(jnp.int32, (L, LANES), 1)
    sub = lax.broadcasted_iota(jnp.int32, (ng, L), 0)

    def chunk_body(c, carry):
        r0 = pl.multiple_of(c * L, L)
        gl = gl_ref[pl.ds(r0, L), :]
        glt = glt_ref[:, pl.ds(r0, L)]
        for hh in range(heads_per_step):
            head = head0 + hh
            li_col = jnp.sum(jnp.where(lane == head, gl, 0.0), axis=-1, keepdims=True)
            lf_col = jnp.sum(jnp.where(lane == head + MLSTM_HEADS, gl, 0.0), axis=-1, keepdims=True)
            li_row = jnp.sum(jnp.where(sub == head, glt, 0.0), axis=0, keepdims=True)
            lf_row = jnp.sum(jnp.where(sub == head + MLSTM_HEADS, glt, 0.0), axis=0, keepdims=True)
            b_col = jnp.dot(tri, jnp.broadcast_to(lf_col, (L, LANES)), precision=hi,
                            preferred_element_type=F32)[:, 0:1]
            b_row = jnp.dot(jnp.broadcast_to(lf_row, (8, L)), tri_t, precision=hi,
                            preferred_element_type=F32)[0:1, :]
            m_prev = m_ref[hh]
            d = jnp.where(causal, b_col - b_row + li_row, NEG_INF)
            m_t = jnp.maximum(b_col + m_prev, jnp.max(d, axis=-1, keepdims=True))
            w = jnp.exp(d - m_t)
            inter = jnp.exp(b_col + m_prev - m_t)
            q = q_ref[pl.ds(r0, L), hh * dk:(hh + 1) * dk]
            k = k_ref[pl.ds(r0, L), hh * dk:(hh + 1) * dk]
            v = v_ref[pl.ds(r0, L), hh * dv:(hh + 1) * dv]
            qk = lax.dot_general(q, k, (((1,), (1,)), ((), ())), preferred_element_type=F32) * w
            c_state = c_ref[hh]
            n_state = n_ref[hh]
            num = inter * jnp.dot(q, c_state.astype(BF16), preferred_element_type=F32) \
                + jnp.dot(qk.astype(BF16), v, preferred_element_type=F32)
            den = inter * jnp.sum(q.astype(F32) * n_state, axis=-1, keepdims=True) \
                + jnp.sum(qk, axis=-1, keepdims=True)
            hs = num / jnp.maximum(jnp.abs(den), jnp.exp(-m_t))
            m_new = m_t[L - 1:L, :]
            b_last = b_col[L - 1:L, :]
            decay = jnp.exp(b_last + m_prev - m_new)
            ws = jnp.exp(b_last - b_col + li_col - m_new)
            kw = k.astype(F32) * ws
            c_ref[hh] = decay * c_state + jnp.dot(kw.T.astype(BF16), v, preferred_element_type=F32)
            n_ref[hh] = decay * n_state + jnp.sum(kw, axis=0, keepdims=True)
            m_ref[hh] = m_new
            y = hs * lax.rsqrt(jnp.mean(hs * hs, axis=-1, keepdims=True) + NORM_EPS) \
                * hn_ref[:, hh * dv:(hh + 1) * dv]
            o_ref[pl.ds(r0, L), hh * dv:(hh + 1) * dv] = (
                og_ref[pl.ds(r0, L), hh * dv:(hh + 1) * dv].astype(F32) * y).astype(BF16)
        return carry

    lax.fori_loop(0, seq // L, chunk_body, 0)


def _mlstm_scan(q, k, v, og, gl, glt, head_norm, batch, seq, chunk, heads_per_step):
    t = q.shape[0]
    hps = heads_per_step
    dk, dv = MLSTM_QK_DIM, MLSTM_V_DIM
    ng = 2 * MLSTM_HEADS
    est = 2 * seq * hps * (2 * dk + 3 * dv) * 2 + 2 * seq * LANES * 4 + 2 * ng * seq * 4 \
        + hps * dk * dv * 4 + 16 * chunk * chunk * 4
    return pl.pallas_call(
        functools.partial(_mlstm_scan_kernel, chunk=chunk, heads_per_step=hps),
        grid=(batch, MLSTM_HEADS // hps),
        in_specs=[pl.BlockSpec((seq, hps * dk), lambda b, g: (b, g)),
                  pl.BlockSpec((seq, hps * dk), lambda b, g: (b, g)),
                  pl.BlockSpec((seq, hps * dv), lambda b, g: (b, g)),
                  pl.BlockSpec((seq, hps * dv), lambda b, g: (b, g)),
                  pl.BlockSpec((seq, LANES), lambda b, g: (b, 0)),
                  pl.BlockSpec((ng, seq), lambda b, g: (0, b)),
                  pl.BlockSpec((1, hps * dv), lambda b, g: (0, g))],
        out_specs=pl.BlockSpec((seq, hps * dv), lambda b, g: (b, g)),
        out_shape=jax.ShapeDtypeStruct((t, MLSTM_HEADS * dv), BF16),
        scratch_shapes=[pltpu.VMEM((hps, dk, dv), F32), pltpu.VMEM((hps, 1, dk), F32),
                        pltpu.VMEM((hps, 1, 1), F32)],
        compiler_params=pltpu.CompilerParams(
            dimension_semantics=("arbitrary", "arbitrary"), vmem_limit_bytes=_vmem_limit(est)),
        name="mlstm_scan",
    )(q, k, v, og, gl, glt, head_norm.reshape(1, -1))


def kernel(x, c, positions, mla_w_in, mla_q_norm, mla_kv_norm, mla_w_uq, mla_w_ukv, mla_w_o, mlstm_w_in, mlstm_b_gates, mlstm_head_norm, mlstm_w_out, moe_w_router, moe_b_router, moe_w_gu, moe_b_gu, moe_w_down, moe_b_down, ada_w, ada_b, ln_g, ln_b):
    batch, seq, d = x.shape
    assert d == D_MODEL and ada_w.shape[0] == DEPTH
    cfg = _tiles(batch, seq)
    t = batch * seq
    tm = cfg["tm_tok"]
    mod = _adaln(c, ada_w, ada_b, cfg["tn_ada"])
    lng = ln_g.reshape(2 * DEPTH, d)
    lnb = ln_b.reshape(2 * DEPTH, d)
    xt = x.reshape(t, d)
    for i in range(DEPTH):
        j = i // 2
        k_mix, k_ffn = 2 * i, 2 * i + 1
        if i % 2 == 0:
            qn, qr, kn, v, kr = _mla_proj(xt, mod, k_mix, positions, mla_w_in[j], mla_q_norm[j],
                                          mla_kv_norm[j], mla_w_uq[j], mla_w_ukv[j], seq, tm)
            a = _attention(qn, qr, kn, kr, v, batch, seq, cfg["tq"])
            w_o = mla_w_o[j]
        else:
            q, k, v, og, gl, glt = _mlstm_proj(xt, mod, k_mix, mlstm_w_in[j], mlstm_b_gates[j], seq, tm)
            a = _mlstm_scan(q, k, v, og, gl, glt, mlstm_head_norm[j], batch, seq,
                            cfg["chunk"], cfg["heads_per_step"])
            w_o = mlstm_w_out[j]
        x1, h2, topi, topg = _post(a, xt, mod, k_mix, k_ffn, lng[k_mix], lnb[k_mix], w_o,
                                   moe_w_router[i], moe_b_router[i], seq, cfg["tm_post"])
        xt = _moe(h2, topi, topg, x1, mod, k_ffn, lng[k_ffn], lnb[k_ffn],
                  moe_w_gu, moe_b_gu, moe_w_down, moe_b_down, i, seq, cfg)
    return xt.reshape(batch, seq, d)
```

```python
import functools

import jax
import jax.numpy as jnp
from jax import lax
from jax.experimental import pallas as pl
from jax.experimental.pallas import tpu as pltpu

F32 = jnp.float32
BF16 = jnp.bfloat16

D_MODEL = 2048
DEPTH = 2

MLA_HEADS = 16
QK_NOPE_DIM = 128
QK_ROPE_DIM = 64
V_HEAD_DIM = 128
Q_LORA_RANK = 512
KV_LORA_RANK = 512
ROPE_THETA = 10000.0

MLSTM_HEADS = 8
MLSTM_QK_DIM = D_MODEL // (2 * MLSTM_HEADS)
MLSTM_V_DIM = D_MODEL // MLSTM_HEADS
GATE_SOFTCAP = 15.0

N_EXPERTS = 32
TOP_K = 4
D_EXPERT = D_MODEL
SWIGLU_ALPHA = 1.702
SWIGLU_LIMIT = 7.0

DEEPNORM_ALPHA = (2 * DEPTH) ** 0.25
NORM_EPS = 1e-6

LANES = 128
V7X_VMEM_BYTES = 64 * 1024 * 1024
NEG_INF = float("-inf")


def _vmem_limit(estimate_bytes):
    return int(min(estimate_bytes + (12 << 20), V7X_VMEM_BYTES - (8 << 20)))


def _tiles(batch, seq):
    t = batch * seq
    cfg = dict(
        tm_tok=min(256, seq),
        tm_post=min(512, seq),
        tq=min(256, seq),
        chunk=min(128, seq),
        heads_per_step=4,
        tm_moe=min(512, t * TOP_K // 8),
        tf=1024,
        tc_gemm=512,
        tn=512,
        tc=min(128, seq),
        tn_ada=768,
    )
    return cfg


def _adaln_kernel(c_ref, w_ref, b_ref, o_ref):
    c = c_ref[...]
    cond = c * jax.nn.sigmoid(c)
    o_ref[...] = jnp.dot(cond.astype(BF16), w_ref[...].astype(BF16),
                         preferred_element_type=F32) + b_ref[...]


def _adaln(c, ada_w, ada_b, tn):
    b, d = c.shape
    n_mod = ada_w.shape[0] * ada_w.shape[1]
    w = ada_w.reshape(n_mod, d, 3 * d)
    bias = ada_b.reshape(n_mod, 1, 3 * d)
    out = pl.pallas_call(
        _adaln_kernel,
        grid=(n_mod, 3 * d // tn),
        in_specs=[
            pl.BlockSpec((b, d), lambda i, j: (0, 0)),
            pl.BlockSpec((None, d, tn), lambda i, j: (i, 0, j)),
            pl.BlockSpec((None, 1, tn), lambda i, j: (i, 0, j)),
        ],
        out_specs=pl.BlockSpec((None, b, tn), lambda i, j: (i, 0, j)),
        out_shape=jax.ShapeDtypeStruct((n_mod, b, 3 * d), F32),
        compiler_params=pltpu.CompilerParams(
            dimension_semantics=("arbitrary", "arbitrary"),
            vmem_limit_bytes=_vmem_limit(2 * d * tn * 4 + d * tn * 2)),
        name="adaln",
    )(c, w, bias)
    return out.reshape(n_mod, b, 1, 3 * d)


def _mod_spec(k, part, tiles_per_batch):
    return pl.BlockSpec((None, None, 1, D_MODEL),
                        lambda i, *_: (k, i // tiles_per_batch, 0, part))


def _row_spec(k):
    return pl.BlockSpec((None, 1, D_MODEL), lambda i, *_: (k, 0, 0))


def _resident(shape):
    nd = len(shape)
    return pl.BlockSpec(shape, lambda *_: (0,) * nd, pipeline_mode=pl.Buffered(1))


def _mla_proj_kernel(x_ref, shift_ref, scale_ref, pos_ref, rope_ref, win_ref, qnorm_ref, kvnorm_ref,
                     wqn_ref, wqr_ref, wqs_ref, wkn_ref, wv_ref,
                     qn_ref, qr_ref, kn_ref, v_ref, kr_ref):
    h = x_ref[...] * (1.0 + scale_ref[...]) + shift_ref[...]
    lat = jnp.dot(h.astype(BF16), win_ref[...], preferred_element_type=F32)
    cq = lat[:, :Q_LORA_RANK]
    ckv = lat[:, Q_LORA_RANK:Q_LORA_RANK + KV_LORA_RANK]
    kr = lat[:, Q_LORA_RANK + KV_LORA_RANK:Q_LORA_RANK + KV_LORA_RANK + LANES]
    kr_sw = lat[:, Q_LORA_RANK + KV_LORA_RANK + LANES:]
    cq = cq * lax.rsqrt(jnp.mean(cq * cq, axis=-1, keepdims=True) + NORM_EPS) * qnorm_ref[...]
    ckv = ckv * lax.rsqrt(jnp.mean(ckv * ckv, axis=-1, keepdims=True) + NORM_EPS) * kvnorm_ref[...]
    cq = cq.astype(BF16)
    ckv = ckv.astype(BF16)

    ang = pos_ref[...].astype(F32) * rope_ref[0:1, :]
    cos_t = jnp.cos(ang) * rope_ref[1:2, :]
    sin_t = jnp.sin(ang) * rope_ref[2:3, :]

    qn_ref[...] = jnp.dot(cq, wqn_ref[...], preferred_element_type=F32).astype(BF16)
    q_rope = jnp.dot(cq, wqr_ref[...], preferred_element_type=F32)
    q_swap = jnp.dot(cq, wqs_ref[...], preferred_element_type=F32)
    cos_h = jnp.tile(cos_t, (1, MLA_HEADS))
    sin_h = jnp.tile(sin_t, (1, MLA_HEADS))
    qr_ref[...] = (q_rope * cos_h + q_swap * sin_h).astype(BF16)
    kn_ref[...] = jnp.dot(ckv, wkn_ref[...], preferred_element_type=F32).astype(BF16)
    v_ref[...] = jnp.dot(ckv, wv_ref[...], preferred_element_type=F32).astype(BF16)
    kr_ref[...] = (kr * cos_t + kr_sw * sin_t).astype(BF16)


def _rope_table():
    half = QK_ROPE_DIM // 2
    inv_freq = ROPE_THETA ** (-jnp.arange(0, QK_ROPE_DIM, 2, dtype=F32) / QK_ROPE_DIM)
    zeros = jnp.zeros((LANES - QK_ROPE_DIM,), F32)
    freq_row = jnp.concatenate([inv_freq, inv_freq, zeros])
    cos_mask = jnp.concatenate([jnp.ones((QK_ROPE_DIM,), F32), zeros])
    sin_sign = jnp.concatenate([-jnp.ones((half,), F32), jnp.ones((half,), F32), zeros])
    pad = jnp.zeros((5, LANES), F32)
    return jnp.concatenate([jnp.stack([freq_row, cos_mask, sin_sign]), pad], axis=0)


def _mla_weights(w_in, w_uq, w_ukv):
    d = w_in.shape[0]
    h = MLA_HEADS
    half = QK_ROPE_DIM // 2
    lat = Q_LORA_RANK + KV_LORA_RANK
    kr = w_in[:, lat:]
    zpad = jnp.zeros((d, LANES - QK_ROPE_DIM), w_in.dtype)
    win = jnp.concatenate([w_in[:, :lat], kr, zpad, kr[:, half:], kr[:, :half], zpad], axis=1)
    wq = w_uq.reshape(Q_LORA_RANK, h, QK_NOPE_DIM + QK_ROPE_DIM)
    wqn = wq[:, :, :QK_NOPE_DIM].reshape(Q_LORA_RANK, h * QK_NOPE_DIM)
    rope = wq[:, :, QK_NOPE_DIM:]
    zq = jnp.zeros((Q_LORA_RANK, h, LANES - QK_ROPE_DIM), w_uq.dtype)
    wqr = jnp.concatenate([rope, zq], axis=-1).reshape(Q_LORA_RANK, h * LANES)
    wqs = jnp.concatenate([rope[..., half:], rope[..., :half], zq], axis=-1).reshape(Q_LORA_RANK, h * LANES)
    wkv = w_ukv.reshape(KV_LORA_RANK, h, QK_NOPE_DIM + V_HEAD_DIM)
    wkn = wkv[:, :, :QK_NOPE_DIM].reshape(KV_LORA_RANK, h * QK_NOPE_DIM)
    wv = wkv[:, :, QK_NOPE_DIM:].reshape(KV_LORA_RANK, h * V_HEAD_DIM)
    return tuple(a.astype(BF16) for a in (win, wqn, wqr, wqs, wkn, wv))


def _mla_proj(x2d, mod, k_mod, positions, w_in, q_norm, kv_norm, w_uq, w_ukv, seq, tm):
    t, d = x2d.shape
    tiles_per_batch = seq // tm
    win, wqn, wqr, wqs, wkn, wv = _mla_weights(w_in, w_uq, w_ukv)
    hn = MLA_HEADS * LANES
    tok = lambda n: pl.BlockSpec((tm, n), lambda i: (i, 0))
    weights_bytes = 2 * (win.size + wqn.size + wqr.size + wqs.size + wkn.size + wv.size)
    est = weights_bytes + 2 * tm * d * 4 + 2 * (4 * tm * hn * 2 + tm * LANES * 2) + 8 * tm * hn * 4
    outs = pl.pallas_call(
        _mla_proj_kernel,
        grid=(t // tm,),
        in_specs=[
            tok(d),
            _mod_spec(k_mod, 0, tiles_per_batch),
            _mod_spec(k_mod, 1, tiles_per_batch),
            pl.BlockSpec((tm, 1), lambda i: (i, 0)),
            _resident((8, LANES)),
            _resident(win.shape),
            _resident((1, Q_LORA_RANK)),
            _resident((1, KV_LORA_RANK)),
            _resident(wqn.shape), _resident(wqr.shape), _resident(wqs.shape),
            _resident(wkn.shape), _resident(wv.shape),
        ],
        out_specs=[tok(hn), tok(hn), tok(hn), tok(hn), tok(LANES)],
        out_shape=[jax.ShapeDtypeStruct((t, hn), BF16)] * 4 + [jax.ShapeDtypeStruct((t, LANES), BF16)],
        compiler_params=pltpu.CompilerParams(
            dimension_semantics=("arbitrary",), vmem_limit_bytes=_vmem_limit(est)),
        name="mla_proj",
    )(x2d, mod, mod, positions.reshape(t, 1), _rope_table(), win,
      q_norm.reshape(1, -1), kv_norm.reshape(1, -1), wqn, wqr, wqs, wkn, wv)
    return outs


def _attn_kernel(qn_ref, qr_ref, kn_ref, kr_ref, v_ref, o_ref, s_ref, *, tq, scale):
    seq = qn_ref.shape[0]
    row = lax.broadcasted_iota(jnp.int32, (tq, tq), 0)
    col = lax.broadcasted_iota(jnp.int32, (tq, tq), 1)
    causal = col <= row
    exp2_scale = scale * 1.4426950408889634

    def fold_lanes(a):
        return [a[:, g * LANES:(g + 1) * LANES] for g in range(tq // LANES)]

    for qi in range(seq // tq):
        rows = slice(qi * tq, (qi + 1) * tq)
        q = jnp.concatenate([qn_ref[rows, :], qr_ref[rows, :]], axis=-1)
        m_acc = jnp.full((tq, LANES), NEG_INF, F32)
        for j in range(qi + 1):
            cols = slice(j * tq, (j + 1) * tq)
            k = jnp.concatenate([kn_ref[cols, :], kr_ref[cols, :]], axis=-1)
            s = lax.dot_general(q, k, (((1,), (1,)), ((), ())), preferred_element_type=F32)
            if j == qi:
                s = jnp.where(causal, s, NEG_INF)
            s_ref[:, cols] = s
            for part in fold_lanes(s):
                m_acc = jnp.maximum(m_acc, part)
        m = jnp.max(m_acc, axis=-1, keepdims=True)
        l_acc = jnp.zeros((tq, LANES), F32)
        acc = jnp.zeros((tq, V_HEAD_DIM), F32)
        for j in range(qi + 1):
            cols = slice(j * tq, (j + 1) * tq)
            p = jnp.exp2((s_ref[:, cols] - m) * exp2_scale)
            for part in fold_lanes(p):
                l_acc = l_acc + part
            acc = acc + jnp.dot(p.astype(BF16), v_ref[cols, :], preferred_element_type=F32)
        l = jnp.sum(l_acc, axis=-1, keepdims=True)
        o_ref[rows, :] = (acc / l).astype(BF16)


def _attention(qn, qr, kn, kr, v, batch, seq, tq):
    t = qn.shape[0]
    head = lambda: pl.BlockSpec((seq, LANES), lambda b, h: (b, h))
    scale = (QK_NOPE_DIM + QK_ROPE_DIM) ** -0.5
    est = 2 * 6 * seq * LANES * 2 + tq * seq * 4 + 6 * tq * tq * 4
    return pl.pallas_call(
        functools.partial(_attn_kernel, tq=tq, scale=scale),
        grid=(batch, MLA_HEADS),
        in_specs=[head(), head(), head(), pl.BlockSpec((seq, LANES), lambda b, h: (b, 0)), head()],
        out_specs=head(),
        out_shape=jax.ShapeDtypeStruct((t, MLA_HEADS * V_HEAD_DIM), BF16),
        scratch_shapes=[pltpu.VMEM((tq, seq), F32)],
        compiler_params=pltpu.CompilerParams(
            dimension_semantics=("arbitrary", "arbitrary"), vmem_limit_bytes=_vmem_limit(est)),
        name="mla_attention",
    )(qn, qr, kn, kr, v)


SLABS = D_MODEL // LANES


PACKED_SLABS = SLABS // 2


def _store_slabs(ref, val, first_slab=0, first_row=0, pitch=SLABS):
    rows = val.shape[0]
    for s in range(val.shape[1] // LANES):
        ref[pl.ds(first_row * pitch + first_slab + s, rows, stride=pitch), :] = val[:, s * LANES:(s + 1) * LANES]


def _pack_bf16_pairs(hi, lo):
    hi = pltpu.bitcast(hi.astype(BF16).astype(F32), jnp.uint32)
    lo = pltpu.bitcast(lo.astype(BF16).astype(F32), jnp.uint32)
    return hi | (lo >> 16)


def _unpack_bf16_pairs(u):
    return pltpu.bitcast(u & jnp.uint32(0xFFFF0000), F32), pltpu.bitcast(u << 16, F32)


def _layer_norm(z, g, b):
    mu = jnp.mean(z, axis=-1, keepdims=True)
    zc = z - mu
    var = jnp.mean(zc * zc, axis=-1, keepdims=True)
    return zc * lax.rsqrt(var + NORM_EPS) * g + b


def _post_kernel(a_ref, x_ref, gate_ref, lng_ref, lnb_ref, shift_ref, scale_ref, wo_ref, wr_ref, br_ref,
                 x1_ref, h2_ref, topi_ref, topg_ref, *, sub):
    starts = range(0, a_ref.shape[0], sub)
    ys = [jnp.dot(a_ref[r0:r0 + sub, :], wo_ref[...], preferred_element_type=F32) for r0 in starts]
    for y, r0 in zip(ys, starts):
        _post_rows(slice(r0, r0 + sub), r0, y, x_ref, gate_ref, lng_ref, lnb_ref, shift_ref, scale_ref,
                   wr_ref, br_ref, x1_ref, h2_ref, topi_ref, topg_ref)


def _post_rows(rows, r0, y, x_ref, gate_ref, lng_ref, lnb_ref, shift_ref, scale_ref, wr_ref,
               br_ref, x1_ref, h2_ref, topi_ref, topg_ref):
    z = DEEPNORM_ALPHA * x_ref[rows, :] + gate_ref[...] * y
    x1 = _layer_norm(z, lng_ref[...], lnb_ref[...])
    x1_ref[rows, :] = x1
    h2 = x1 * (1.0 + scale_ref[...]) + shift_ref[...]
    half = h2.shape[1] // 2
    _store_slabs(h2_ref, _pack_bf16_pairs(h2[:, :half], h2[:, half:]), first_row=r0, pitch=PACKED_SLABS)
    h_hi = h2.astype(BF16)
    h_lo = (h2 - h_hi.astype(F32)).astype(BF16)
    t_hi = jnp.dot(h_hi, wr_ref[...], preferred_element_type=F32)
    t_lo = jnp.dot(h_lo, wr_ref[:, :LANES], preferred_element_type=F32)
    logits = t_hi[:, :LANES] + (t_hi[:, LANES:] + t_lo) + br_ref[...]
    tm = logits.shape[0]
    lane = lax.broadcasted_iota(jnp.int32, (tm, LANES), 1)
    vals, idxs = [], []
    for _ in range(TOP_K):
        mk = jnp.max(logits, axis=-1, keepdims=True)
        ik = jnp.min(jnp.where(logits == mk, lane, LANES), axis=-1, keepdims=True)
        vals.append(mk)
        idxs.append(ik)
        logits = jnp.where(lane == ik, NEG_INF, logits)
    exps = [jnp.exp(v - vals[0]) for v in vals]
    den = exps[0]
    for e in exps[1:]:
        den = den + e
    topi = jnp.zeros((tm, LANES), jnp.int32)
    topg = jnp.zeros((tm, LANES), F32)
    for k in range(TOP_K):
        topi = jnp.where(lane == k, idxs[k], topi)
        topg = jnp.where(lane == k, exps[k] / den, topg)
    topi_ref[rows, :] = topi
    topg_ref[rows, :] = topg


def _post(a, x2d, mod, k_mix, k_ffn, ln_g, ln_b, w_o, w_router, b_router, seq, tm):
    t, d = x2d.shape
    tiles_per_batch = seq // tm
    wr = jnp.concatenate([w_router, jnp.zeros((d, LANES - N_EXPERTS), F32)], axis=1)
    wr_hi = wr.astype(BF16)
    wr_lo = (wr - wr_hi.astype(F32)).astype(BF16)
    wr = jnp.concatenate([wr_hi, wr_lo], axis=1)
    br = jnp.concatenate([b_router, jnp.full((LANES - N_EXPERTS,), NEG_INF, F32)]).reshape(1, LANES)
    tok = lambda n: pl.BlockSpec((tm, n), lambda i: (i, 0))
    sub = min(256, tm)
    est = d * d * 2 + d * LANES * 4 + 2 * tm * d * (2 + 4 + 4 + 4) + 6 * sub * d * 4
    return pl.pallas_call(
        functools.partial(_post_kernel, sub=sub),
        grid=(t // tm,),
        in_specs=[
            tok(d), tok(d),
            _mod_spec(k_mix, 2, tiles_per_batch),
            _resident((1, d)), _resident((1, d)),
            _mod_spec(k_ffn, 0, tiles_per_batch),
            _mod_spec(k_ffn, 1, tiles_per_batch),
            _resident((d, d)), _resident((d, 2 * LANES)), _resident((1, LANES)),
        ],
        out_specs=[tok(d), pl.BlockSpec((tm * PACKED_SLABS, LANES), lambda i: (i, 0)), tok(LANES), tok(LANES)],
        out_shape=[jax.ShapeDtypeStruct((t, d), F32), jax.ShapeDtypeStruct((t * PACKED_SLABS, LANES), jnp.uint32),
                   jax.ShapeDtypeStruct((t, LANES), jnp.int32), jax.ShapeDtypeStruct((t, LANES), F32)],
        compiler_params=pltpu.CompilerParams(
            dimension_semantics=("arbitrary",), vmem_limit_bytes=_vmem_limit(est)),
        name="mixer_out_ln_router",
    )(a, x2d, mod, ln_g.reshape(1, d), ln_b.reshape(1, d), mod, mod, w_o.astype(BF16), wr, br)


def _route(topi, tm_moe):
    t = topi.shape[0]
    m = t * TOP_K
    e_tk = topi[:, :TOP_K]
    a = jnp.sum((e_tk[:, :, None] == jnp.arange(N_EXPERTS, dtype=jnp.int32)).astype(F32), axis=1)
    blk = min(256, t)
    ab = a.reshape(t // blk, blk, N_EXPERTS)
    tri = jnp.tril(jnp.ones((blk, blk), F32))
    within = jnp.einsum("ts,bse->bte", tri, ab)
    btot = within[:, -1, :]
    boff = jnp.cumsum(btot, axis=0) - btot
    excl = (within + boff[:, None, :] - ab).reshape(t, N_EXPERTS)
    counts = (boff[-1] + btot[-1]).astype(jnp.int32)
    rank = jnp.take_along_axis(excl, e_tk, axis=1).astype(jnp.int32)
    padded = ((counts + tm_moe - 1) // tm_moe) * tm_moe
    pends = jnp.cumsum(padded)
    pstarts = pends - padded
    dest = (pstarts[e_tk] + rank).astype(jnp.int32).reshape(m)
    n_tiles = m // tm_moe + N_EXPERTS
    order = jnp.argsort(e_tk.reshape(m), stable=True).astype(jnp.int32)
    starts = jnp.cumsum(counts) - counts
    tile_first = jnp.arange(n_tiles, dtype=jnp.int32) * tm_moe
    e_tile = jnp.minimum(jnp.sum((tile_first[:, None] >= pends[None, :]).astype(jnp.int32), axis=1),
                         N_EXPERTS - 1)
    off = (tile_first - pstarts[e_tile])[:, None] + jnp.arange(tm_moe, dtype=jnp.int32)[None, :]
    valid = off < counts[e_tile][:, None]
    src = jnp.where(valid, starts[e_tile][:, None] + off, 0).reshape(n_tiles * tm_moe)
    row_tok = jnp.where(valid.reshape(n_tiles * tm_moe), order[src] // TOP_K, 0)
    tile_start = (pstarts // tm_moe).astype(jnp.int32)
    tile_count = (padded // tm_moe).astype(jnp.int32)
    n_used = (pends[-1] // tm_moe).astype(jnp.int32).reshape(1)
    return dest, row_tok, tile_start, tile_count, n_used


def _gather_issue(idx_ref, src_hbm, buf, sem, slot, n_rows):
    def body(i, carry):
        for u in range(2):
            r = 2 * i + u
            pltpu.make_async_copy(src_hbm.at[idx_ref[0, 0, r]], buf.at[slot, :, r, :],
                                  sem.at[slot]).start(priority=u)
        return carry
    lax.fori_loop(0, n_rows // 2, body, 0, unroll=4)


def _gather_wait(buf, sem, slot):
    pltpu.make_async_copy(buf.at[slot], buf.at[slot], sem.at[slot]).wait()


def _dispatch_kernel(ns_ref, idx_ref, nxt_ref, src_hbm, o_ref, buf, sem, *, n_rows):
    i = pl.program_id(0)
    n_active = ns_ref[0]
    slot = lax.rem(i, 2)

    @pl.when(jnp.logical_and(i == 0, n_active > 0))
    def _():
        _gather_issue(idx_ref, src_hbm, buf, sem, 0, n_rows)

    @pl.when(i + 1 < n_active)
    def _():
        _gather_issue(nxt_ref, src_hbm, buf, sem, 1 - slot, n_rows)

    @pl.when(i < n_active)
    def _():
        _gather_wait(buf, sem, slot)
        hi, lo = _unpack_bf16_pairs(buf[slot])
        o_ref[0:PACKED_SLABS] = hi.astype(BF16)
        o_ref[PACKED_SLABS:SLABS] = lo.astype(BF16)

    @pl.when(i >= n_active)
    def _():
        o_ref[...] = jnp.zeros(o_ref.shape, o_ref.dtype)


def _dispatch(h2_packed, row_tok, n_used, r):
    t, packed_slabs, _ = h2_packed.shape
    slabs = 2 * packed_slabs
    n = row_tok.shape[0] // r
    idx = row_tok.reshape(n, 1, r)
    smem = lambda f: pl.BlockSpec((1, 1, r), f, memory_space=pltpu.SMEM)
    return pl.pallas_call(
        functools.partial(_dispatch_kernel, n_rows=r),
        grid_spec=pltpu.PrefetchScalarGridSpec(
            num_scalar_prefetch=1,
            grid=(n,),
            in_specs=[smem(lambda i, ns: (i, 0, 0)),
                      smem(lambda i, ns: (jnp.minimum(i + 1, n - 1), 0, 0)),
                      pl.BlockSpec(memory_space=pl.ANY)],
            out_specs=pl.BlockSpec((None, slabs, r, LANES), lambda i, ns: (i, 0, 0, 0)),
            scratch_shapes=[pltpu.VMEM((2, packed_slabs, r, LANES), jnp.uint32),
                            pltpu.SemaphoreType.DMA((2,))]),
        out_shape=jax.ShapeDtypeStruct((n, slabs, r, LANES), BF16),
        compiler_params=pltpu.CompilerParams(
            dimension_semantics=("arbitrary",),
            vmem_limit_bytes=_vmem_limit(4 * r * slabs * LANES * 2)),
        name="moe_dispatch_gather",
    )(n_used, idx, idx, h2_packed)


def _expert_weights(step, n_steps, copies_for, stage, w_bf):
    @pl.when(step == 0)
    def _():
        for c in copies_for(step):
            c.start(priority=1)

    for c in copies_for(step):
        c.wait()
    w_bf[...] = stage[...].astype(BF16)

    @pl.when(step + 1 < n_steps)
    def _():
        for c in copies_for(step + 1):
            c.start(priority=1)


def _start_first_tile(n_tiles, in_copies):
    @pl.when(n_tiles > 0)
    def _():
        for c in in_copies(0, 0):
            c.start()


def _tile_loop(n_tiles, in_copies, out_copy, compute):
    def start_in(k, slot):
        for c in in_copies(k, slot):
            c.start()

    def tile(k, slot):
        for c in in_copies(k, slot):
            c.wait()

        @pl.when(k + 1 < n_tiles)
        def _():
            start_in(k + 1, 1 - slot)

        @pl.when(k >= 2)
        def _():
            out_copy(k - 2, slot).wait()

        compute(slot)
        out_copy(k, slot).start()

    def pair(p, carry):
        tile(2 * p, 0)

        @pl.when(2 * p + 1 < n_tiles)
        def _():
            tile(2 * p + 1, 1)
        return carry

    lax.fori_loop(0, (n_tiles + 1) // 2, pair, 0)

    for slot in (0, 1):
        @pl.when(jnp.logical_and(n_tiles >= 2, lax.rem(n_tiles, 2) == slot))
        def _():
            out_copy(n_tiles - 2, slot).wait()

        @pl.when(jnp.logical_and(n_tiles >= 1, lax.rem(n_tiles + 1, 2) == slot))
        def _():
            out_copy(n_tiles - 1, slot).wait()


def _zero_tail_tiles(first, n_total, zero_src, dst_for, sem):
    def body(t, carry):
        cp = pltpu.make_async_copy(zero_src, dst_for(t), sem)
        cp.start()
        cp.wait()
        return carry
    lax.fori_loop(first, n_total, body, 0)


def _gemm1_kernel(ts_ref, nt_ref, nu_ref, bg_ref, bu_ref, w_hbm, x_hbm, o_hbm,
                  stage, w_bf, xbuf, obuf, wsem, xsem, osem, *, layer, tf, tc):
    j = pl.program_id(0)
    e = pl.program_id(1)
    n_e = pl.num_programs(1)
    f = D_EXPERT

    def weight_copies(step):
        jj = step // n_e
        ee = step - jj * n_e
        return [pltpu.make_async_copy(
            w_hbm.at[layer, ee, :, pl.ds(pl.multiple_of(half * f + jj * tf, tf), tf)],
            stage.at[half], wsem.at[half]) for half in range(2)]

    t0 = ts_ref[e]

    def x_copies(k, slot):
        return [pltpu.make_async_copy(x_hbm.at[t0 + k], xbuf.at[slot], xsem.at[slot])]

    _start_first_tile(nt_ref[e], x_copies)
    _expert_weights(j * n_e + e, pl.num_programs(0) * n_e, weight_copies, stage, w_bf)

    def o_copy(k, slot):
        return pltpu.make_async_copy(obuf.at[slot], o_hbm.at[j, t0 + k], osem.at[slot])

    def compute(slot):
        x = jnp.concatenate([xbuf[slot, s] for s in range(xbuf.shape[1])], axis=-1)
        for c in range(tf // tc):
            cols = slice(c * tc, (c + 1) * tc)
            g = jnp.dot(x, w_bf[0, :, cols], preferred_element_type=F32) + bg_ref[:, cols]
            u = jnp.dot(x, w_bf[1, :, cols], preferred_element_type=F32) + bu_ref[:, cols]
            g = jnp.minimum(g, SWIGLU_LIMIT)
            u = jnp.clip(u, -SWIGLU_LIMIT, SWIGLU_LIMIT)
            obuf[slot, :, cols] = (g * jax.nn.sigmoid(SWIGLU_ALPHA * g) * (u + 1.0)).astype(BF16)

    _tile_loop(nt_ref[e], x_copies, o_copy, compute)

    @pl.when(e == n_e - 1)
    def _():
        obuf[0] = jnp.zeros(obuf.shape[1:], obuf.dtype)
        _zero_tail_tiles(nu_ref[0], o_hbm.shape[1], obuf.at[0], lambda t: o_hbm.at[j, t], osem.at[0])


def _gemm1(x_rows, tile_start, tile_count, n_used, w_gu, b_gu, layer, tf, tc):
    n_tiles, slabs, tm, _ = x_rows.shape
    d = slabs * LANES
    f = D_EXPERT
    nf = f // tf
    bias = b_gu.reshape(b_gu.shape[0], N_EXPERTS, 1, 2 * f)
    bspec = lambda off: pl.BlockSpec((None, None, 1, tf), lambda j, e, ts, nt, nu: (layer, e, 0, j + off))
    any_spec = pl.BlockSpec(memory_space=pl.ANY)
    est = 2 * d * tf * (4 + 2) + 2 * tm * d * 2 + 2 * tm * tf * 2 + 6 * tm * tc * 4
    return pl.pallas_call(
        functools.partial(_gemm1_kernel, layer=layer, tf=tf, tc=tc),
        grid_spec=pltpu.PrefetchScalarGridSpec(
            num_scalar_prefetch=3,
            grid=(nf, N_EXPERTS),
            in_specs=[bspec(0), bspec(nf), any_spec, any_spec],
            out_specs=any_spec,
            scratch_shapes=[pltpu.VMEM((2, d, tf), F32), pltpu.VMEM((2, d, tf), BF16),
                            pltpu.VMEM((2, slabs, tm, LANES), BF16), pltpu.VMEM((2, tm, tf), BF16),
                            pltpu.SemaphoreType.DMA((2,)), pltpu.SemaphoreType.DMA((2,)),
                            pltpu.SemaphoreType.DMA((2,))]),
        out_shape=jax.ShapeDtypeStruct((nf, n_tiles, tm, tf), BF16),
        compiler_params=pltpu.CompilerParams(
            dimension_semantics=("arbitrary", "arbitrary"), vmem_limit_bytes=_vmem_limit(est)),
        name="moe_gate_up_swiglu",
    )(tile_start, tile_count, n_used, bias, bias, w_gu, x_rows)


def _gemm2_kernel(ts_ref, nt_ref, nu_ref, b_ref, w_hbm, a_hbm, o_hbm,
                  stage, w_bf, abuf, obuf, wsem, asem, osem, *, layer, tn):
    e = pl.program_id(0)
    nf, tm = abuf.shape[1], abuf.shape[2]
    tile_rows = tm * PACKED_SLABS

    def weight_copies(step):
        return [pltpu.make_async_copy(w_hbm.at[layer, step], stage, wsem.at[0])]

    t0 = ts_ref[e]

    def a_copies(k, slot):
        return [pltpu.make_async_copy(a_hbm.at[c, t0 + k], abuf.at[slot, c], asem.at[slot, c])
                for c in range(nf)]

    _start_first_tile(nt_ref[e], a_copies)
    _expert_weights(e, pl.num_programs(0), weight_copies, stage, w_bf)

    def o_tile(t):
        return o_hbm.at[pl.ds(pl.multiple_of(t * tile_rows, tile_rows), tile_rows), :]

    def o_copy(k, slot):
        return pltpu.make_async_copy(obuf.at[pl.ds(slot * tile_rows, tile_rows), :],
                                     o_tile(t0 + k), osem.at[slot])

    def compute(slot):
        a = jnp.concatenate([abuf[slot, c] for c in range(nf)], axis=-1)
        half = w_bf.shape[1] // 2
        for c in range(half // tn):
            res = [jnp.dot(a, w_bf[:, cols], preferred_element_type=F32) + b_ref[:, cols]
                   for cols in (slice(c * tn, (c + 1) * tn), slice(half + c * tn, half + (c + 1) * tn))]
            _store_slabs(obuf, _pack_bf16_pairs(*res), first_slab=c * tn // LANES, first_row=slot * tm,
                         pitch=PACKED_SLABS)

    _tile_loop(nt_ref[e], a_copies, o_copy, compute)

    @pl.when(e == pl.num_programs(0) - 1)
    def _():
        obuf[0:tile_rows, :] = jnp.zeros((tile_rows, LANES), obuf.dtype)
        _zero_tail_tiles(nu_ref[0], o_hbm.shape[0] // tile_rows, obuf.at[0:tile_rows, :], o_tile, osem.at[0])


def _gemm2(act, tile_start, tile_count, n_used, w_down, b_down, layer, tn):
    nf, n_tiles, tm, tf = act.shape
    f = nf * tf
    d = D_MODEL
    bias = b_down.reshape(b_down.shape[0], N_EXPERTS, 1, d)
    any_spec = pl.BlockSpec(memory_space=pl.ANY)
    est = f * d * (4 + 2) + 2 * tm * f * 2 + 2 * tm * d * 2 + 6 * tm * tn * 4
    return pl.pallas_call(
        functools.partial(_gemm2_kernel, layer=layer, tn=tn),
        grid_spec=pltpu.PrefetchScalarGridSpec(
            num_scalar_prefetch=3,
            grid=(N_EXPERTS,),
            in_specs=[pl.BlockSpec((None, None, 1, d), lambda e, ts, nt, nu: (layer, e, 0, 0)),
                      any_spec, any_spec],
            out_specs=any_spec,
            scratch_shapes=[pltpu.VMEM((f, d), F32), pltpu.VMEM((f, d), BF16),
                            pltpu.VMEM((2, nf, tm, tf), BF16),
                            pltpu.VMEM((2 * tm * PACKED_SLABS, LANES), jnp.uint32),
                            pltpu.SemaphoreType.DMA((1,)), pltpu.SemaphoreType.DMA((2, nf)),
                            pltpu.SemaphoreType.DMA((2,))]),
        out_shape=jax.ShapeDtypeStruct((n_tiles * tm * PACKED_SLABS, LANES), jnp.uint32),
        compiler_params=pltpu.CompilerParams(
            dimension_semantics=("arbitrary",), vmem_limit_bytes=_vmem_limit(est)),
        name="moe_down",
    )(tile_start, tile_count, n_used, bias, w_down, act)


def _combine_kernel(idx_ref, nxt_ref, rows_hbm, g_ref, x_ref, gate_ref, lng_ref, lnb_ref, o_ref,
                    buf, sem, *, tc):
    i = pl.program_id(0)
    n = pl.num_programs(0)
    slot = lax.rem(i, 2)
    n_rows = TOP_K * tc

    @pl.when(i == 0)
    def _():
        _gather_issue(idx_ref, rows_hbm, buf, sem, 0, n_rows)

    @pl.when(i + 1 < n)
    def _():
        _gather_issue(nxt_ref, rows_hbm, buf, sem, 1 - slot, n_rows)

    _gather_wait(buf, sem, slot)
    g = g_ref[...]
    gk = [jnp.broadcast_to(g[:, k:k + 1], (tc, LANES)) for k in range(TOP_K)]
    hi_parts, lo_parts = [], []
    for s in range(buf.shape[1]):
        acc_hi = acc_lo = None
        for k in range(TOP_K):
            hi, lo = _unpack_bf16_pairs(buf[slot, s, pl.ds(k * tc, tc), :])
            acc_hi = gk[k] * hi if acc_hi is None else acc_hi + gk[k] * hi
            acc_lo = gk[k] * lo if acc_lo is None else acc_lo + gk[k] * lo
        hi_parts.append(acc_hi)
        lo_parts.append(acc_lo)
    y = jnp.concatenate(hi_parts + lo_parts, axis=-1)
    z = DEEPNORM_ALPHA * x_ref[...] + gate_ref[...] * y
    o_ref[...] = _layer_norm(z, lng_ref[...], lnb_ref[...])


def _combine(out_rows, dest, topg, x1, mod, k_ffn, ln_g, ln_b, seq, tc):
    t, d = x1.shape
    n = t // tc
    tiles_per_batch = seq // tc
    idx = dest.reshape(n, tc, TOP_K).transpose(0, 2, 1).reshape(n, 1, TOP_K * tc)
    smem = lambda f: pl.BlockSpec((1, 1, TOP_K * tc), f, memory_space=pltpu.SMEM)
    tok = lambda w: pl.BlockSpec((tc, w), lambda i: (i, 0))
    return pl.pallas_call(
        functools.partial(_combine_kernel, tc=tc),
        grid=(n,),
        in_specs=[smem(lambda i: (i, 0, 0)),
                  smem(lambda i: (jnp.minimum(i + 1, n - 1), 0, 0)),
                  pl.BlockSpec(memory_space=pl.ANY),
                  tok(LANES), tok(d),
                  _mod_spec(k_ffn, 2, tiles_per_batch),
                  _resident((1, d)), _resident((1, d))],
        out_specs=tok(d),
        out_shape=jax.ShapeDtypeStruct((t, d), F32),
        scratch_shapes=[pltpu.VMEM((2, PACKED_SLABS, TOP_K * tc, LANES), jnp.uint32),
                        pltpu.SemaphoreType.DMA((2,))],
        compiler_params=pltpu.CompilerParams(
            dimension_semantics=("arbitrary",),
            vmem_limit_bytes=_vmem_limit(2 * TOP_K * tc * d * 4 + 6 * tc * d * 4)),
        name="moe_combine_ln",
    )(idx, idx, out_rows, topg, x1, mod, ln_g.reshape(1, d), ln_b.reshape(1, d))


def _moe(h2, topi, topg, x1, mod, k_ffn, ln_g, ln_b, w_gu, b_gu, w_down, b_down, layer, seq, cfg):
    dest, row_tok, tile_start, tile_count, n_used = _route(topi, cfg["tm_moe"])
    x_rows = _dispatch(h2.reshape(-1, PACKED_SLABS, LANES), row_tok, n_used, cfg["tm_moe"])
    act = _gemm1(x_rows, tile_start, tile_count, n_used, w_gu, b_gu, layer, cfg["tf"], cfg["tc_gemm"])
    out_rows = _gemm2(act, tile_start, tile_count, n_used, w_down, b_down, layer, cfg["tn"])
    out_rows = out_rows.reshape(-1, PACKED_SLABS, LANES)
    return _combine(out_rows, dest, topg, x1, mod, k_ffn, ln_g, ln_b, seq, cfg["tc"])


def _softcap(g):
    return GATE_SOFTCAP * jnp.tanh(g / GATE_SOFTCAP)


def _log_sigmoid(x):
    return jnp.minimum(x, 0.0) - jnp.log(1.0 + jnp.exp(-jnp.abs(x)))


def _mlstm_proj_kernel(x_ref, shift_ref, scale_ref, wq_ref, wk_ref, wv_ref, wo_ref, wg_ref, wgt_ref,
                       bg_ref, bgt_ref, q_ref, k_ref, v_ref, og_ref, gl_ref, glt_ref):
    h = (x_ref[...] * (1.0 + scale_ref[...]) + shift_ref[...]).astype(BF16)
    q_ref[...] = (jnp.dot(h, wq_ref[...], preferred_element_type=F32) * MLSTM_QK_DIM ** -0.5).astype(BF16)
    k_ref[...] = jnp.dot(h, wk_ref[...], preferred_element_type=F32).astype(BF16)
    v_ref[...] = jnp.dot(h, wv_ref[...], preferred_element_type=F32).astype(BF16)
    og_ref[...] = jax.nn.sigmoid(jnp.dot(h, wo_ref[...], preferred_element_type=F32)).astype(BF16)
    g = _softcap(jnp.dot(h, wg_ref[...], preferred_element_type=F32) + bg_ref[...])
    lane = lax.broadcasted_iota(jnp.int32, g.shape, 1)
    gl_ref[...] = jnp.where(lane < MLSTM_HEADS, g,
                            jnp.where(lane < 2 * MLSTM_HEADS, _log_sigmoid(g), 0.0))
    gt = _softcap(lax.dot_general(wgt_ref[...], h, (((1,), (1,)), ((), ())),
                                  preferred_element_type=F32) + bgt_ref[...])
    sub = lax.broadcasted_iota(jnp.int32, gt.shape, 0)
    glt_ref[...] = jnp.where(sub < MLSTM_HEADS, gt, _log_sigmoid(gt))


def _mlstm_proj(x2d, mod, k_mod, w_in, b_gates, seq, tm):
    t, d = x2d.shape
    tiles_per_batch = seq // tm
    hk = MLSTM_HEADS * MLSTM_QK_DIM
    hv = MLSTM_HEADS * MLSTM_V_DIM
    ng = 2 * MLSTM_HEADS
    o1, o2, o3, o4 = hk, 2 * hk, 2 * hk + hv, 2 * hk + 2 * hv
    wq, wk, wv, wo = (w_in[:, a:b].astype(BF16) for a, b in ((0, o1), (o1, o2), (o2, o3), (o3, o4)))
    wg = jnp.concatenate([w_in[:, o4:], jnp.zeros((d, LANES - ng), F32)], axis=1).astype(BF16)
    wgt = w_in[:, o4:].T.astype(BF16)
    bg = jnp.concatenate([b_gates, jnp.zeros((LANES - ng,), F32)]).reshape(1, LANES)
    bgt = b_gates.reshape(ng, 1)
    tok = lambda n: pl.BlockSpec((tm, n), lambda i: (i, 0))
    est = 2 * d * (2 * hk + 2 * hv + LANES + ng) + 2 * tm * d * 4 + 2 * tm * (2 * hk + 2 * hv) * 2 \
        + 6 * tm * hv * 4
    return pl.pallas_call(
        _mlstm_proj_kernel,
        grid=(t // tm,),
        in_specs=[tok(d), _mod_spec(k_mod, 0, tiles_per_batch), _mod_spec(k_mod, 1, tiles_per_batch),
                  _resident(wq.shape), _resident(wk.shape), _resident(wv.shape), _resident(wo.shape),
                  _resident(wg.shape), _resident(wgt.shape), _resident((1, LANES)), _resident((ng, 1))],
        out_specs=[tok(hk), tok(hk), tok(hv), tok(hv), tok(LANES),
                   pl.BlockSpec((ng, tm), lambda i: (0, i))],
        out_shape=[jax.ShapeDtypeStruct((t, hk), BF16), jax.ShapeDtypeStruct((t, hk), BF16),
                   jax.ShapeDtypeStruct((t, hv), BF16), jax.ShapeDtypeStruct((t, hv), BF16),
                   jax.ShapeDtypeStruct((t, LANES), F32), jax.ShapeDtypeStruct((ng, t), F32)],
        compiler_params=pltpu.CompilerParams(
            dimension_semantics=("arbitrary",), vmem_limit_bytes=_vmem_limit(est)),
        name="mlstm_proj",
    )(x2d, mod, mod, wq, wk, wv, wo, wg, wgt, bg, bgt)


def _mlstm_scan_kernel(q_ref, k_ref, v_ref, og_ref, gl_ref, glt_ref, hn_ref, o_ref,
                       c_ref, n_ref, m_ref, *, chunk, heads_per_step):
    seq = q_ref.shape[0]
    L = chunk
    dk, dv = MLSTM_QK_DIM, MLSTM_V_DIM
    ng = 2 * MLSTM_HEADS
    head0 = pl.program_id(1) * heads_per_step
    hi = lax.Precision.HIGHEST

    c_ref[...] = jnp.zeros(c_ref.shape, F32)
    n_ref[...] = jnp.zeros(n_ref.shape, F32)
    m_ref[...] = jnp.zeros(m_ref.shape, F32)

    row = lax.broadcasted_iota(jnp.int32, (L, L), 0)
    col = lax.broadcasted_iota(jnp.int32, (L, L), 1)
    causal = col <= row
    tri = causal.astype(F32)
    tri_t = (row <= col).astype(F32)
    lane = lax.broadcasted_iota(jnp.int32, (L, LANES), 1)
    sub = lax.broadcasted_iota(jnp.int32, (ng, L), 0)

    def chunk_body(c, carry):
        r0 = pl.multiple_of(c * L, L)
        gl = gl_ref[pl.ds(r0, L), :]
        glt = glt_ref[:, pl.ds(r0, L)]
        for hh in range(heads_per_step):
            head = head0 + hh
            li_col = jnp.sum(jnp.where(lane == head, gl, 0.0), axis=-1, keepdims=True)
            lf_col = jnp.sum(jnp.where(lane == head + MLSTM_HEADS, gl, 0.0), axis=-1, keepdims=True)
            li_row = jnp.sum(jnp.where(sub == head, glt, 0.0), axis=0, keepdims=True)
            lf_row = jnp.sum(jnp.where(sub == head + MLSTM_HEADS, glt, 0.0), axis=0, keepdims=True)
            b_col = jnp.dot(tri, jnp.broadcast_to(lf_col, (L, LANES)), precision=hi,
                            preferred_element_type=F32)[:, 0:1]
            b_row = jnp.dot(jnp.broadcast_to(lf_row, (8, L)), tri_t, precision=hi,
                            preferred_element_type=F32)[0:1, :]
            m_prev = m_ref[hh]
            d = jnp.where(causal, b_col - b_row + li_row, NEG_INF)
            m_t = jnp.maximum(b_col + m_prev, jnp.max(d, axis=-1, keepdims=True))
            w = jnp.exp(d - m_t)
            inter = jnp.exp(b_col + m_prev - m_t)
            q = q_ref[pl.ds(r0, L), hh * dk:(hh + 1) * dk]
            k = k_ref[pl.ds(r0, L), hh * dk:(hh + 1) * dk]
            v = v_ref[pl.ds(r0, L), hh * dv:(hh + 1) * dv]
            qk = lax.dot_general(q, k, (((1,), (1,)), ((), ())), preferred_element_type=F32) * w
            c_state = c_ref[hh]
            n_state = n_ref[hh]
            num = inter * jnp.dot(q, c_state.astype(BF16), preferred_element_type=F32) \
                + jnp.dot(qk.astype(BF16), v, preferred_element_type=F32)
            den = inter * jnp.sum(q.astype(F32) * n_state, axis=-1, keepdims=True) \
                + jnp.sum(qk, axis=-1, keepdims=True)
            hs = num / jnp.maximum(jnp.abs(den), jnp.exp(-m_t))
            m_new = m_t[L - 1:L, :]
            b_last = b_col[L - 1:L, :]
            decay = jnp.exp(b_last + m_prev - m_new)
            ws = jnp.exp(b_last - b_col + li_col - m_new)
            kw = k.astype(F32) * ws
            c_ref[hh] = decay * c_state + jnp.dot(kw.T.astype(BF16), v, preferred_element_type=F32)
            n_ref[hh] = decay * n_state + jnp.sum(kw, axis=0, keepdims=True)
            m_ref[hh] = m_new
            y = hs * lax.rsqrt(jnp.mean(hs * hs, axis=-1, keepdims=True) + NORM_EPS) \
                * hn_ref[:, hh * dv:(hh + 1) * dv]
            o_ref[pl.ds(r0, L), hh * dv:(hh + 1) * dv] = (
                og_ref[pl.ds(r0, L), hh * dv:(hh + 1) * dv].astype(F32) * y).astype(BF16)
        return carry

    lax.fori_loop(0, seq // L, chunk_body, 0)


def _mlstm_scan(q, k, v, og, gl, glt, head_norm, batch, seq, chunk, heads_per_step):
    t = q.shape[0]
    hps = heads_per_step
    dk, dv = MLSTM_QK_DIM, MLSTM_V_DIM
    ng = 2 * MLSTM_HEADS
    est = 2 * seq * hps * (2 * dk + 3 * dv) * 2 + 2 * seq * LANES * 4 + 2 * ng * seq * 4 \
        + hps * dk * dv * 4 + 16 * chunk * chunk * 4
    return pl.pallas_call(
        functools.partial(_mlstm_scan_kernel, chunk=chunk, heads_per_step=hps),
        grid=(batch, MLSTM_HEADS // hps),
        in_specs=[pl.BlockSpec((seq, hps * dk), lambda b, g: (b, g)),
                  pl.BlockSpec((seq, hps * dk), lambda b, g: (b, g)),
                  pl.BlockSpec((seq, hps * dv), lambda b, g: (b, g)),
                  pl.BlockSpec((seq, hps * dv), lambda b, g: (b, g)),
                  pl.BlockSpec((seq, LANES), lambda b, g: (b, 0)),
                  pl.BlockSpec((ng, seq), lambda b, g: (0, b)),
                  pl.BlockSpec((1, hps * dv), lambda b, g: (0, g))],
        out_specs=pl.BlockSpec((seq, hps * dv), lambda b, g: (b, g)),
        out_shape=jax.ShapeDtypeStruct((t, MLSTM_HEADS * dv), BF16),
        scratch_shapes=[pltpu.VMEM((hps, dk, dv), F32), pltpu.VMEM((hps, 1, dk), F32),
                        pltpu.VMEM((hps, 1, 1), F32)],
        compiler_params=pltpu.CompilerParams(
            dimension_semantics=("arbitrary", "arbitrary"), vmem_limit_bytes=_vmem_limit(est)),
        name="mlstm_scan",
    )(q, k, v, og, gl, glt, head_norm.reshape(1, -1))


def kernel(x, c, positions, mla_w_in, mla_q_norm, mla_kv_norm, mla_w_uq, mla_w_ukv, mla_w_o, mlstm_w_in, mlstm_b_gates, mlstm_head_norm, mlstm_w_out, moe_w_router, moe_b_router, moe_w_gu, moe_b_gu, moe_w_down, moe_b_down, ada_w, ada_b, ln_g, ln_b):
    batch, seq, d = x.shape
    assert d == D_MODEL and ada_w.shape[0] == DEPTH
    cfg = _tiles(batch, seq)
    t = batch * seq
    tm = cfg["tm_tok"]
    mod = _adaln(c, ada_w, ada_b, cfg["tn_ada"])
    lng = ln_g.reshape(2 * DEPTH, d)
    lnb = ln_b.reshape(2 * DEPTH, d)
    xt = x.reshape(t, d)
    for i in range(DEPTH):
        j = i // 2
        k_mix, k_ffn = 2 * i, 2 * i + 1
        if i % 2 == 0:
            qn, qr, kn, v, kr = _mla_proj(xt, mod, k_mix, positions, mla_w_in[j], mla_q_norm[j],
                                          mla_kv_norm[j], mla_w_uq[j], mla_w_ukv[j], seq, tm)
            a = _attention(qn, qr, kn, kr, v, batch, seq, cfg["tq"])
            w_o = mla_w_o[j]
        else:
            q, k, v, og, gl, glt = _mlstm_proj(xt, mod, k_mix, mlstm_w_in[j], mlstm_b_gates[j], seq, tm)
            a = _mlstm_scan(q, k, v, og, gl, glt, mlstm_head_norm[j], batch, seq,
                            cfg["chunk"], cfg["heads_per_step"])
            w_o = mlstm_w_out[j]
        x1, h2, topi, topg = _post(a, xt, mod, k_mix, k_ffn, lng[k_mix], lnb[k_mix], w_o,
                                   moe_w_router[i], moe_b_router[i], seq, cfg["tm_post"])
        xt = _moe(h2, topi, topg, x1, mod, k_ffn, lng[k_ffn], lnb[k_ffn],
                  moe_w_gu, moe_b_gu, moe_w_down, moe_b_down, i, seq, cfg)
    return xt.reshape(batch, seq, d)
```

```python
import functools

import jax
import jax.numpy as jnp
from jax import lax
from jax.experimental import pallas as pl
from jax.experimental.pallas import tpu as pltpu

F32 = jnp.float32
BF16 = jnp.bfloat16

D_MODEL = 2048
DEPTH = 2

MLA_HEADS = 16
QK_NOPE_DIM = 128
QK_ROPE_DIM = 64
V_HEAD_DIM = 128
Q_LORA_RANK = 512
KV_LORA_RANK = 512
ROPE_THETA = 10000.0

MLSTM_HEADS = 8
MLSTM_QK_DIM = D_MODEL // (2 * MLSTM_HEADS)
MLSTM_V_DIM = D_MODEL // MLSTM_HEADS
GATE_SOFTCAP = 15.0

N_EXPERTS = 32
TOP_K = 4
D_EXPERT = D_MODEL
SWIGLU_ALPHA = 1.702
SWIGLU_LIMIT = 7.0

DEEPNORM_ALPHA = (2 * DEPTH) ** 0.25
NORM_EPS = 1e-6

LANES = 128
V7X_VMEM_BYTES = 64 * 1024 * 1024
NEG_INF = float("-inf")


def _vmem_limit(estimate_bytes):
    return int(min(estimate_bytes + (12 << 20), V7X_VMEM_BYTES - (8 << 20)))


def _tiles(batch, seq):
    t = batch * seq
    cfg = dict(
        tm_tok=min(256, seq),
        tm_post=min(512, seq),
        tq=min(256, seq),
        chunk=min(128, seq),
        heads_per_step=4,
        tm_moe=min(512, t * TOP_K // 8),
        tf=1024,
        tc_gemm=512,
        tn=512,
        tc=min(128, seq),
        tn_ada=768,
    )
    return cfg


def _adaln_kernel(c_ref, w_ref, b_ref, o_ref):
    c = c_ref[...]
    cond = c * jax.nn.sigmoid(c)
    o_ref[...] = jnp.dot(cond.astype(BF16), w_ref[...].astype(BF16),
                         preferred_element_type=F32) + b_ref[...]


def _adaln(c, ada_w, ada_b, tn):
    b, d = c.shape
    n_mod = ada_w.shape[0] * ada_w.shape[1]
    w = ada_w.reshape(n_mod, d, 3 * d)
    bias = ada_b.reshape(n_mod, 1, 3 * d)
    out = pl.pallas_call(
        _adaln_kernel,
        grid=(n_mod, 3 * d // tn),
        in_specs=[
            pl.BlockSpec((b, d), lambda i, j: (0, 0)),
            pl.BlockSpec((None, d, tn), lambda i, j: (i, 0, j)),
            pl.BlockSpec((None, 1, tn), lambda i, j: (i, 0, j)),
        ],
        out_specs=pl.BlockSpec((None, b, tn), lambda i, j: (i, 0, j)),
        out_shape=jax.ShapeDtypeStruct((n_mod, b, 3 * d), F32),
        compiler_params=pltpu.CompilerParams(
            dimension_semantics=("arbitrary", "arbitrary"),
            vmem_limit_bytes=_vmem_limit(2 * d * tn * 4 + d * tn * 2)),
        name="adaln",
    )(c, w, bias)
    return out.reshape(n_mod, b, 1, 3 * d)


def _mod_spec(k, part, tiles_per_batch):
    return pl.BlockSpec((None, None, 1, D_MODEL),
                        lambda i, *_: (k, i // tiles_per_batch, 0, part))


def _row_spec(k):
    return pl.BlockSpec((None, 1, D_MODEL), lambda i, *_: (k, 0, 0))


def _resident(shape):
    nd = len(shape)
    return pl.BlockSpec(shape, lambda *_: (0,) * nd, pipeline_mode=pl.Buffered(1))


def _mla_proj_kernel(x_ref, shift_ref, scale_ref, pos_ref, rope_ref, win_ref, qnorm_ref, kvnorm_ref,
                     wqn_ref, wqr_ref, wqs_ref, wkn_ref, wv_ref,
                     qn_ref, qr_ref, kn_ref, v_ref, kr_ref):
    h = x_ref[...] * (1.0 + scale_ref[...]) + shift_ref[...]
    lat = jnp.dot(h.astype(BF16), win_ref[...], preferred_element_type=F32)
    cq = lat[:, :Q_LORA_RANK]
    ckv = lat[:, Q_LORA_RANK:Q_LORA_RANK + KV_LORA_RANK]
    kr = lat[:, Q_LORA_RANK + KV_LORA_RANK:Q_LORA_RANK + KV_LORA_RANK + LANES]
    kr_sw = lat[:, Q_LORA_RANK + KV_LORA_RANK + LANES:]
    cq = cq * lax.rsqrt(jnp.mean(cq * cq, axis=-1, keepdims=True) + NORM_EPS) * qnorm_ref[...]
    ckv = ckv * lax.rsqrt(jnp.mean(ckv * ckv, axis=-1, keepdims=True) + NORM_EPS) * kvnorm_ref[...]
    cq = cq.astype(BF16)
    ckv = ckv.astype(BF16)

    ang = pos_ref[...].astype(F32) * rope_ref[0:1, :]
    cos_t = jnp.cos(ang) * rope_ref[1:2, :]
    sin_t = jnp.sin(ang) * rope_ref[2:3, :]

    qn_ref[...] = jnp.dot(cq, wqn_ref[...], preferred_element_type=F32).astype(BF16)
    q_rope = jnp.dot(cq, wqr_ref[...], preferred_element_type=F32)
    q_swap = jnp.dot(cq, wqs_ref[...], preferred_element_type=F32)
    cos_h = jnp.tile(cos_t, (1, MLA_HEADS))
    sin_h = jnp.tile(sin_t, (1, MLA_HEADS))
    qr_ref[...] = (q_rope * cos_h + q_swap * sin_h).astype(BF16)
    kn_ref[...] = jnp.dot(ckv, wkn_ref[...], preferred_element_type=F32).astype(BF16)
    v_ref[...] = jnp.dot(ckv, wv_ref[...], preferred_element_type=F32).astype(BF16)
    kr_ref[...] = (kr * cos_t + kr_sw * sin_t).astype(BF16)


def _rope_table():
    half = QK_ROPE_DIM // 2
    inv_freq = ROPE_THETA ** (-jnp.arange(0, QK_ROPE_DIM, 2, dtype=F32) / QK_ROPE_DIM)
    zeros = jnp.zeros((LANES - QK_ROPE_DIM,), F32)
    freq_row = jnp.concatenate([inv_freq, inv_freq, zeros])
    cos_mask = jnp.concatenate([jnp.ones((QK_ROPE_DIM,), F32), zeros])
    sin_sign = jnp.concatenate([-jnp.ones((half,), F32), jnp.ones((half,), F32), zeros])
    pad = jnp.zeros((5, LANES), F32)
    return jnp.concatenate([jnp.stack([freq_row, cos_mask, sin_sign]), pad], axis=0)


def _mla_weights(w_in, w_uq, w_ukv):
    d = w_in.shape[0]
    h = MLA_HEADS
    half = QK_ROPE_DIM // 2
    lat = Q_LORA_RANK + KV_LORA_RANK
    kr = w_in[:, lat:]
    zpad = jnp.zeros((d, LANES - QK_ROPE_DIM), w_in.dtype)
    win = jnp.concatenate([w_in[:, :lat], kr, zpad, kr[:, half:], kr[:, :half], zpad], axis=1)
    wq = w_uq.reshape(Q_LORA_RANK, h, QK_NOPE_DIM + QK_ROPE_DIM)
    wqn = wq[:, :, :QK_NOPE_DIM].reshape(Q_LORA_RANK, h * QK_NOPE_DIM)
    rope = wq[:, :, QK_NOPE_DIM:]
    zq = jnp.zeros((Q_LORA_RANK, h, LANES - QK_ROPE_DIM), w_uq.dtype)
    wqr = jnp.concatenate([rope, zq], axis=-1).reshape(Q_LORA_RANK, h * LANES)
    wqs = jnp.concatenate([rope[..., half:], rope[..., :half], zq], axis=-1).reshape(Q_LORA_RANK, h * LANES)
    wkv = w_ukv.reshape(KV_LORA_RANK, h, QK_NOPE_DIM + V_HEAD_DIM)
    wkn = wkv[:, :, :QK_NOPE_DIM].reshape(KV_LORA_RANK, h * QK_NOPE_DIM)
    wv = wkv[:, :, QK_NOPE_DIM:].reshape(KV_LORA_RANK, h * V_HEAD_DIM)
    return tuple(a.astype(BF16) for a in (win, wqn, wqr, wqs, wkn, wv))


def _mla_proj(x2d, mod, k_mod, positions, w_in, q_norm, kv_norm, w_uq, w_ukv, seq, tm):
    t, d = x2d.shape
    tiles_per_batch = seq // tm
    win, wqn, wqr, wqs, wkn, wv = _mla_weights(w_in, w_uq, w_ukv)
    hn = MLA_HEADS * LANES
    tok = lambda n: pl.BlockSpec((tm, n), lambda i: (i, 0))
    weights_bytes = 2 * (win.size + wqn.size + wqr.size + wqs.size + wkn.size + wv.size)
    est = weights_bytes + 2 * tm * d * 4 + 2 * (4 * tm * hn * 2 + tm * LANES * 2) + 8 * tm * hn * 4
    outs = pl.pallas_call(
        _mla_proj_kernel,
        grid=(t // tm,),
        in_specs=[
            tok(d),
            _mod_spec(k_mod, 0, tiles_per_batch),
            _mod_spec(k_mod, 1, tiles_per_batch),
            pl.BlockSpec((tm, 1), lambda i: (i, 0)),
            _resident((8, LANES)),
            _resident(win.shape),
            _resident((1, Q_LORA_RANK)),
            _resident((1, KV_LORA_RANK)),
            _resident(wqn.shape), _resident(wqr.shape), _resident(wqs.shape),
            _resident(wkn.shape), _resident(wv.shape),
        ],
        out_specs=[tok(hn), tok(hn), tok(hn), tok(hn), tok(LANES)],
        out_shape=[jax.ShapeDtypeStruct((t, hn), BF16)] * 4 + [jax.ShapeDtypeStruct((t, LANES), BF16)],
        compiler_params=pltpu.CompilerParams(
            dimension_semantics=("arbitrary",), vmem_limit_bytes=_vmem_limit(est)),
        name="mla_proj",
    )(x2d, mod, mod, positions.reshape(t, 1), _rope_table(), win,
      q_norm.reshape(1, -1), kv_norm.reshape(1, -1), wqn, wqr, wqs, wkn, wv)
    return outs


def _attn_kernel(qn_ref, qr_ref, kn_ref, kr_ref, v_ref, o_ref, s_ref, *, tq, scale):
    seq = qn_ref.shape[0]
    row = lax.broadcasted_iota(jnp.int32, (tq, tq), 0)
    col = lax.broadcasted_iota(jnp.int32, (tq, tq), 1)
    causal = col <= row
    exp2_scale = scale * 1.4426950408889634

    def fold_lanes(a):
        return [a[:, g * LANES:(g + 1) * LANES] for g in range(tq // LANES)]

    for qi in range(seq // tq):
        rows = slice(qi * tq, (qi + 1) * tq)
        q = jnp.concatenate([qn_ref[rows, :], qr_ref[rows, :]], axis=-1)
        m_acc = jnp.full((tq, LANES), NEG_INF, F32)
        for j in range(qi + 1):
            cols = slice(j * tq, (j + 1) * tq)
            k = jnp.concatenate([kn_ref[cols, :], kr_ref[cols, :]], axis=-1)
            s = lax.dot_general(q, k, (((1,), (1,)), ((), ())), preferred_element_type=F32)
            if j == qi:
                s = jnp.where(causal, s, NEG_INF)
            s_ref[:, cols] = s
            for part in fold_lanes(s):
                m_acc = jnp.maximum(m_acc, part)
        m = jnp.max(m_acc, axis=-1, keepdims=True)
        l_acc = jnp.zeros((tq, LANES), F32)
        acc = jnp.zeros((tq, V_HEAD_DIM), F32)
        for j in range(qi + 1):
            cols = slice(j * tq, (j + 1) * tq)
            p = jnp.exp2((s_ref[:, cols] - m) * exp2_scale)
            for part in fold_lanes(p):
                l_acc = l_acc + part
            acc = acc + jnp.dot(p.astype(BF16), v_ref[cols, :], preferred_element_type=F32)
        l = jnp.sum(l_acc, axis=-1, keepdims=True)
        o_ref[rows, :] = (acc / l).astype(BF16)


def _attention(qn, qr, kn, kr, v, batch, seq, tq):
    t = qn.shape[0]
    head = lambda: pl.BlockSpec((seq, LANES), lambda b, h: (b, h))
    scale = (QK_NOPE_DIM + QK_ROPE_DIM) ** -0.5
    est = 2 * 6 * seq * LANES * 2 + tq * seq * 4 + 6 * tq * tq * 4
    return pl.pallas_call(
        functools.partial(_attn_kernel, tq=tq, scale=scale),
        grid=(batch, MLA_HEADS),
        in_specs=[head(), head(), head(), pl.BlockSpec((seq, LANES), lambda b, h: (b, 0)), head()],
        out_specs=head(),
        out_shape=jax.ShapeDtypeStruct((t, MLA_HEADS * V_HEAD_DIM), BF16),
        scratch_shapes=[pltpu.VMEM((tq, seq), F32)],
        compiler_params=pltpu.CompilerParams(
            dimension_semantics=("arbitrary", "arbitrary"), vmem_limit_bytes=_vmem_limit(est)),
        name="mla_attention",
    )(qn, qr, kn, kr, v)


SLABS = D_MODEL // LANES


PACKED_SLABS = SLABS // 2


def _store_slabs(ref, val, first_slab=0, first_row=0, pitch=SLABS):
    rows = val.shape[0]
    for s in range(val.shape[1] // LANES):
        ref[pl.ds(first_row * pitch + first_slab + s, rows, stride=pitch), :] = val[:, s * LANES:(s + 1) * LANES]


def _pack_bf16_pairs(hi, lo):
    hi = pltpu.bitcast(hi.astype(BF16).astype(F32), jnp.uint32)
    lo = pltpu.bitcast(lo.astype(BF16).astype(F32), jnp.uint32)
    return hi | (lo >> 16)


def _unpack_bf16_pairs(u):
    return pltpu.bitcast(u & jnp.uint32(0xFFFF0000), F32), pltpu.bitcast(u << 16, F32)


def _layer_norm(z, g, b):
    mu = jnp.mean(z, axis=-1, keepdims=True)
    zc = z - mu
    var = jnp.mean(zc * zc, axis=-1, keepdims=True)
    return zc * lax.rsqrt(var + NORM_EPS) * g + b


def _post_kernel(a_ref, x_ref, gate_ref, lng_ref, lnb_ref, shift_ref, scale_ref, wo_ref, wr_ref, br_ref,
                 x1_ref, h2_ref, topi_ref, topg_ref, *, sub):
    starts = range(0, a_ref.shape[0], sub)
    ys = [jnp.dot(a_ref[r0:r0 + sub, :], wo_ref[...], preferred_element_type=F32) for r0 in starts]
    for y, r0 in zip(ys, starts):
        _post_rows(slice(r0, r0 + sub), r0, y, x_ref, gate_ref, lng_ref, lnb_ref, shift_ref, scale_ref,
                   wr_ref, br_ref, x1_ref, h2_ref, topi_ref, topg_ref)


def _post_rows(rows, r0, y, x_ref, gate_ref, lng_ref, lnb_ref, shift_ref, scale_ref, wr_ref,
               br_ref, x1_ref, h2_ref, topi_ref, topg_ref):
    z = DEEPNORM_ALPHA * x_ref[rows, :] + gate_ref[...] * y
    x1 = _layer_norm(z, lng_ref[...], lnb_ref[...])
    x1_ref[rows, :] = x1
    h2 = x1 * (1.0 + scale_ref[...]) + shift_ref[...]
    half = h2.shape[1] // 2
    _store_slabs(h2_ref, _pack_bf16_pairs(h2[:, :half], h2[:, half:]), first_row=r0, pitch=PACKED_SLABS)
    h_hi = h2.astype(BF16)
    h_lo = (h2 - h_hi.astype(F32)).astype(BF16)
    t_hi = jnp.dot(h_hi, wr_ref[...], preferred_element_type=F32)
    t_lo = jnp.dot(h_lo, wr_ref[:, :LANES], preferred_element_type=F32)
    logits = t_hi[:, :LANES] + (t_hi[:, LANES:] + t_lo) + br_ref[...]
    tm = logits.shape[0]
    lane = lax.broadcasted_iota(jnp.int32, (tm, LANES), 1)
    vals, idxs = [], []
    for _ in range(TOP_K):
        mk = jnp.max(logits, axis=-1, keepdims=True)
        ik = jnp.min(jnp.where(logits == mk, lane, LANES), axis=-1, keepdims=True)
        vals.append(mk)
        idxs.append(ik)
        logits = jnp.where(lane == ik, NEG_INF, logits)
    exps = [jnp.exp(v - vals[0]) for v in vals]
    den = exps[0]
    for e in exps[1:]:
        den = den + e
    topi = jnp.zeros((tm, LANES), jnp.int32)
    topg = jnp.zeros((tm, LANES), F32)
    for k in range(TOP_K):
        topi = jnp.where(lane == k, idxs[k], topi)
        topg = jnp.where(lane == k, exps[k] / den, topg)
    topi_ref[rows, :] = topi
    topg_ref[rows, :] = topg


def _post(a, x2d, mod, k_mix, k_ffn, ln_g, ln_b, w_o, w_router, b_router, seq, tm):
    t, d = x2d.shape
    tiles_per_batch = seq // tm
    wr = jnp.concatenate([w_router, jnp.zeros((d, LANES - N_EXPERTS), F32)], axis=1)
    wr_hi = wr.astype(BF16)
    wr_lo = (wr - wr_hi.astype(F32)).astype(BF16)
    wr = jnp.concatenate([wr_hi, wr_lo], axis=1)
    br = jnp.concatenate([b_router, jnp.full((LANES - N_EXPERTS,), NEG_INF, F32)]).reshape(1, LANES)
    tok = lambda n: pl.BlockSpec((tm, n), lambda i: (i, 0))
    sub = min(256, tm)
    est = d * d * 2 + d * LANES * 4 + 2 * tm * d * (2 + 4 + 4 + 4) + 6 * sub * d * 4
    return pl.pallas_call(
        functools.partial(_post_kernel, sub=sub),
        grid=(t // tm,),
        in_specs=[
            tok(d), tok(d),
            _mod_spec(k_mix, 2, tiles_per_batch),
            _resident((1, d)), _resident((1, d)),
            _mod_spec(k_ffn, 0, tiles_per_batch),
            _mod_spec(k_ffn, 1, tiles_per_batch),
            _resident((d, d)), _resident((d, 2 * LANES)), _resident((1, LANES)),
        ],
        out_specs=[tok(d), pl.BlockSpec((tm * PACKED_SLABS, LANES), lambda i: (i, 0)), tok(LANES), tok(LANES)],
        out_shape=[jax.ShapeDtypeStruct((t, d), F32), jax.ShapeDtypeStruct((t * PACKED_SLABS, LANES), jnp.uint32),
                   jax.ShapeDtypeStruct((t, LANES), jnp.int32), jax.ShapeDtypeStruct((t, LANES), F32)],
        compiler_params=pltpu.CompilerParams(
            dimension_semantics=("arbitrary",), vmem_limit_bytes=_vmem_limit(est)),
        name="mixer_out_ln_router",
    )(a, x2d, mod, ln_g.reshape(1, d), ln_b.reshape(1, d), mod, mod, w_o.astype(BF16), wr, br)


def _route(topi, tm_moe):
    t = topi.shape[0]
    m = t * TOP_K
    e_tk = topi[:, :TOP_K]
    a = jnp.sum((e_tk[:, :, None] == jnp.arange(N_EXPERTS, dtype=jnp.int32)).astype(F32), axis=1)
    blk = min(256, t)
    ab = a.reshape(t // blk, blk, N_EXPERTS)
    tri = jnp.tril(jnp.ones((blk, blk), F32))
    within = jnp.einsum("ts,bse->bte", tri, ab)
    btot = within[:, -1, :]
    boff = jnp.cumsum(btot, axis=0) - btot
    excl = (within + boff[:, None, :] - ab).reshape(t, N_EXPERTS)
    counts = (boff[-1] + btot[-1]).astype(jnp.int32)
    rank = jnp.take_along_axis(excl, e_tk, axis=1).astype(jnp.int32)
    padded = ((counts + tm_moe - 1) // tm_moe) * tm_moe
    pends = jnp.cumsum(padded)
    pstarts = pends - padded
    dest = (pstarts[e_tk] + rank).astype(jnp.int32).reshape(m)
    n_tiles = m // tm_moe + N_EXPERTS
    order = jnp.argsort(e_tk.reshape(m), stable=True).astype(jnp.int32)
    starts = jnp.cumsum(counts) - counts
    tile_first = jnp.arange(n_tiles, dtype=jnp.int32) * tm_moe
    e_tile = jnp.minimum(jnp.sum((tile_first[:, None] >= pends[None, :]).astype(jnp.int32), axis=1),
                         N_EXPERTS - 1)
    off = (tile_first - pstarts[e_tile])[:, None] + jnp.arange(tm_moe, dtype=jnp.int32)[None, :]
    valid = off < counts[e_tile][:, None]
    src = jnp.where(valid, starts[e_tile][:, None] + off, 0).reshape(n_tiles * tm_moe)
    row_tok = jnp.where(valid.reshape(n_tiles * tm_moe), order[src] // TOP_K, 0)
    tile_start = (pstarts // tm_moe).astype(jnp.int32)
    tile_count = (padded // tm_moe).astype(jnp.int32)
    n_used = (pends[-1] // tm_moe).astype(jnp.int32).reshape(1)
    return dest, row_tok, tile_start, tile_count, n_used


def _gather_issue(idx_ref, src_hbm, buf, sem, slot, n_rows):
    def body(i, carry):
        for u in range(2):
            r = 2 * i + u
            pltpu.make_async_copy(src_hbm.at[idx_ref[0, 0, r]], buf.at[slot, :, r, :],
                                  sem.at[slot]).start(priority=u)
        return carry
    lax.fori_loop(0, n_rows // 2, body, 0, unroll=4)


def _gather_wait(buf, sem, slot):
    pltpu.make_async_copy(buf.at[slot], buf.at[slot], sem.at[slot]).wait()


def _dispatch_kernel(ns_ref, idx_ref, nxt_ref, src_hbm, o_ref, buf, sem, *, n_rows):
    i = pl.program_id(0)
    n_active = ns_ref[0]
    slot = lax.rem(i, 2)

    @pl.when(jnp.logical_and(i == 0, n_active > 0))
    def _():
        _gather_issue(idx_ref, src_hbm, buf, sem, 0, n_rows)

    @pl.when(i + 1 < n_active)
    def _():
        _gather_issue(nxt_ref, src_hbm, buf, sem, 1 - slot, n_rows)

    @pl.when(i < n_active)
    def _():
        _gather_wait(buf, sem, slot)
        hi, lo = _unpack_bf16_pairs(buf[slot])
        o_ref[0:PACKED_SLABS] = hi.astype(BF16)
        o_ref[PACKED_SLABS:SLABS] = lo.astype(BF16)

    @pl.when(i >= n_active)
    def _():
        o_ref[...] = jnp.zeros(o_ref.shape, o_ref.dtype)


def _dispatch(h2_packed, row_tok, n_used, r):
    t, packed_slabs, _ = h2_packed.shape
    slabs = 2 * packed_slabs
    n = row_tok.shape[0] // r
    idx = row_tok.reshape(n, 1, r)
    smem = lambda f: pl.BlockSpec((1, 1, r), f, memory_space=pltpu.SMEM)
    return pl.pallas_call(
        functools.partial(_dispatch_kernel, n_rows=r),
        grid_spec=pltpu.PrefetchScalarGridSpec(
            num_scalar_prefetch=1,
            grid=(n,),
            in_specs=[smem(lambda i, ns: (i, 0, 0)),
                      smem(lambda i, ns: (jnp.minimum(i + 1, n - 1), 0, 0)),
                      pl.BlockSpec(memory_space=pl.ANY)],
            out_specs=pl.BlockSpec((None, slabs, r, LANES), lambda i, ns: (i, 0, 0, 0)),
            scratch_shapes=[pltpu.VMEM((2, packed_slabs, r, LANES), jnp.uint32),
                            pltpu.SemaphoreType.DMA((2,))]),
        out_shape=jax.ShapeDtypeStruct((n, slabs, r, LANES), BF16),
        compiler_params=pltpu.CompilerParams(
            dimension_semantics=("arbitrary",),
            vmem_limit_bytes=_vmem_limit(4 * r * slabs * LANES * 2)),
        name="moe_dispatch_gather",
    )(n_used, idx, idx, h2_packed)


def _expert_weights(step, n_steps, copies_for, stage, w_bf):
    @pl.when(step == 0)
    def _():
        for c in copies_for(step):
            c.start(priority=1)

    for c in copies_for(step):
        c.wait()
    w_bf[...] = stage[...].astype(BF16)

    @pl.when(step + 1 < n_steps)
    def _():
        for c in copies_for(step + 1):
            c.start(priority=1)


def _start_first_tile(n_tiles, in_copies):
    @pl.when(n_tiles > 0)
    def _():
        for c in in_copies(0, 0):
            c.start()


def _tile_loop(n_tiles, in_copies, out_copy, compute):
    def start_in(k, slot):
        for c in in_copies(k, slot):
            c.start()

    def tile(k, slot):
        for c in in_copies(k, slot):
            c.wait()

        @pl.when(k + 1 < n_tiles)
        def _():
            start_in(k + 1, 1 - slot)

        @pl.when(k >= 2)
        def _():
            out_copy(k - 2, slot).wait()

        compute(slot)
        out_copy(k, slot).start()

    def pair(p, carry):
        tile(2 * p, 0)

        @pl.when(2 * p + 1 < n_tiles)
        def _():
            tile(2 * p + 1, 1)
        return carry

    lax.fori_loop(0, (n_tiles + 1) // 2, pair, 0)

    for slot in (0, 1):
        @pl.when(jnp.logical_and(n_tiles >= 2, lax.rem(n_tiles, 2) == slot))
        def _():
            out_copy(n_tiles - 2, slot).wait()

        @pl.when(jnp.logical_and(n_tiles >= 1, lax.rem(n_tiles + 1, 2) == slot))
        def _():
            out_copy(n_tiles - 1, slot).wait()


def _zero_tail_tiles(first, n_total, zero_src, dst_for, sem):
    def body(t, carry):
        cp = pltpu.make_async_copy(zero_src, dst_for(t), sem)
        cp.start()
        cp.wait()
        return carry
    lax.fori_loop(first, n_total, body, 0)


def _gemm1_kernel(ts_ref, nt_ref, nu_ref, bg_ref, bu_ref, w_hbm, x_hbm, o_hbm,
                  stage, w_bf, xbuf, obuf, wsem, xsem, osem, *, layer, tf, tc):
    j = pl.program_id(0)
    e = pl.program_id(1)
    n_e = pl.num_programs(1)
    f = D_EXPERT

    def weight_copies(step):
        jj = step // n_e
        ee = step - jj * n_e
        return [pltpu.make_async_copy(
            w_hbm.at[layer, ee, :, pl.ds(pl.multiple_of(half * f + jj * tf, tf), tf)],
            stage.at[half], wsem.at[half]) for half in range(2)]

    t0 = ts_ref[e]

    def x_copies(k, slot):
        return [pltpu.make_async_copy(x_hbm.at[t0 + k], xbuf.at[slot], xsem.at[slot])]

    _start_first_tile(nt_ref[e], x_copies)
    _expert_weights(j * n_e + e, pl.num_programs(0) * n_e, weight_copies, stage, w_bf)

    def o_copy(k, slot):
        return pltpu.make_async_copy(obuf.at[slot], o_hbm.at[j, t0 + k], osem.at[slot])

    def compute(slot):
        x = jnp.concatenate([xbuf[slot, s] for s in range(xbuf.shape[1])], axis=-1)
        for c in range(tf // tc):
            cols = slice(c * tc, (c + 1) * tc)
            g = jnp.dot(x, w_bf[0, :, cols], preferred_element_type=F32) + bg_ref[:, cols]
            u = jnp.dot(x, w_bf[1, :, cols], preferred_element_type=F32) + bu_ref[:, cols]
            g = jnp.minimum(g, SWIGLU_LIMIT)
            u = jnp.clip(u, -SWIGLU_LIMIT, SWIGLU_LIMIT)
            obuf[slot, :, cols] = (g * jax.nn.sigmoid(SWIGLU_ALPHA * g) * (u + 1.0)).astype(BF16)

    _tile_loop(nt_ref[e], x_copies, o_copy, compute)

    @pl.when(e == n_e - 1)
    def _():
        obuf[0] = jnp.zeros(obuf.shape[1:], obuf.dtype)
        _zero_tail_tiles(nu_ref[0], o_hbm.shape[1], obuf.at[0], lambda t: o_hbm.at[j, t], osem.at[0])


def _gemm1(x_rows, tile_start, tile_count, n_used, w_gu, b_gu, layer, tf, tc):
    n_tiles, slabs, tm, _ = x_rows.shape
    d = slabs * LANES
    f = D_EXPERT
    nf = f // tf
    bias = b_gu.reshape(b_gu.shape[0], N_EXPERTS, 1, 2 * f)
    bspec = lambda off: pl.BlockSpec((None, None, 1, tf), lambda j, e, ts, nt, nu: (layer, e, 0, j + off))
    any_spec = pl.BlockSpec(memory_space=pl.ANY)
    est = 2 * d * tf * (4 + 2) + 2 * tm * d * 2 + 2 * tm * tf * 2 + 6 * tm * tc * 4
    return pl.pallas_call(
        functools.partial(_gemm1_kernel, layer=layer, tf=tf, tc=tc),
        grid_spec=pltpu.PrefetchScalarGridSpec(
            num_scalar_prefetch=3,
            grid=(nf, N_EXPERTS),
            in_specs=[bspec(0), bspec(nf), any_spec, any_spec],
            out_specs=any_spec,
            scratch_shapes=[pltpu.VMEM((2, d, tf), F32), pltpu.VMEM((2, d, tf), BF16),
                            pltpu.VMEM((2, slabs, tm, LANES), BF16), pltpu.VMEM((2, tm, tf), BF16),
                            pltpu.SemaphoreType.DMA((2,)), pltpu.SemaphoreType.DMA((2,)),
                            pltpu.SemaphoreType.DMA((2,))]),
        out_shape=jax.ShapeDtypeStruct((nf, n_tiles, tm, tf), BF16),
        compiler_params=pltpu.CompilerParams(
            dimension_semantics=("arbitrary", "arbitrary"), vmem_limit_bytes=_vmem_limit(est)),
        name="moe_gate_up_swiglu",
    )(tile_start, tile_count, n_used, bias, bias, w_gu, x_rows)


def _gemm2_kernel(ts_ref, nt_ref, nu_ref, b_ref, w_hbm, a_hbm, o_hbm,
                  stage, w_bf, abuf, obuf, wsem, asem, osem, *, layer, tn):
    e = pl.program_id(0)
    nf, tm = abuf.shape[1], abuf.shape[2]
    tile_rows = tm * PACKED_SLABS

    def weight_copies(step):
        return [pltpu.make_async_copy(w_hbm.at[layer, step], stage, wsem.at[0])]

    t0 = ts_ref[e]

    def a_copies(k, slot):
        return [pltpu.make_async_copy(a_hbm.at[c, t0 + k], abuf.at[slot, c], asem.at[slot, c])
                for c in range(nf)]

    _start_first_tile(nt_ref[e], a_copies)
    _expert_weights(e, pl.num_programs(0), weight_copies, stage, w_bf)

    def o_tile(t):
        return o_hbm.at[pl.ds(pl.multiple_of(t * tile_rows, tile_rows), tile_rows), :]

    def o_copy(k, slot):
        return pltpu.make_async_copy(obuf.at[pl.ds(slot * tile_rows, tile_rows), :],
                                     o_tile(t0 + k), osem.at[slot])

    def compute(slot):
        a = jnp.concatenate([abuf[slot, c] for c in range(nf)], axis=-1)
        half = w_bf.shape[1] // 2
        for c in range(half // tn):
            res = [jnp.dot(a, w_bf[:, cols], preferred_element_type=F32) + b_ref[:, cols]
                   for cols in (slice(c * tn, (c + 1) * tn), slice(half + c * tn, half + (c + 1) * tn))]
            _store_slabs(obuf, _pack_bf16_pairs(*res), first_slab=c * tn // LANES, first_row=slot * tm,
                         pitch=PACKED_SLABS)

    _tile_loop(nt_ref[e], a_copies, o_copy, compute)

    @pl.when(e == pl.num_programs(0) - 1)
    def _():
        obuf[0:tile_rows, :] = jnp.zeros((tile_rows, LANES), obuf.dtype)
        _zero_tail_tiles(nu_ref[0], o_hbm.shape[0] // tile_rows, obuf.at[0:tile_rows, :], o_tile, osem.at[0])


def _gemm2(act, tile_start, tile_count, n_used, w_down, b_down, layer, tn):
    nf, n_tiles, tm, tf = act.shape
    f = nf * tf
    d = D_MODEL
    bias = b_down.reshape(b_down.shape[0], N_EXPERTS, 1, d)
    any_spec = pl.BlockSpec(memory_space=pl.ANY)
    est = f * d * (4 + 2) + 2 * tm * f * 2 + 2 * tm * d * 2 + 6 * tm * tn * 4
    return pl.pallas_call(
        functools.partial(_gemm2_kernel, layer=layer, tn=tn),
        grid_spec=pltpu.PrefetchScalarGridSpec(
            num_scalar_prefetch=3,
            grid=(N_EXPERTS,),
            in_specs=[pl.BlockSpec((None, None, 1, d), lambda e, ts, nt, nu: (layer, e, 0, 0)),
                      any_spec, any_spec],
            out_specs=any_spec,
            scratch_shapes=[pltpu.VMEM((f, d), F32), pltpu.VMEM((f, d), BF16),
                            pltpu.VMEM((2, nf, tm, tf), BF16),
                            pltpu.VMEM((2 * tm * PACKED_SLABS, LANES), jnp.uint32),
                            pltpu.SemaphoreType.DMA((1,)), pltpu.SemaphoreType.DMA((2, nf)),
                            pltpu.SemaphoreType.DMA((2,))]),
        out_shape=jax.ShapeDtypeStruct((n_tiles * tm * PACKED_SLABS, LANES), jnp.uint32),
        compiler_params=pltpu.CompilerParams(
            dimension_semantics=("arbitrary",), vmem_limit_bytes=_vmem_limit(est)),
        name="moe_down",
    )(tile_start, tile_count, n_used, bias, w_down, act)


def _combine_kernel(idx_ref, nxt_ref, rows_hbm, g_ref, x_ref, gate_ref, lng_ref, lnb_ref, o_ref,
                    buf, sem, *, tc):
    i = pl.program_id(0)
    n = pl.num_programs(0)
    slot = lax.rem(i, 2)
    n_rows = TOP_K * tc

    @pl.when(i == 0)
    def _():
        _gather_issue(idx_ref, rows_hbm, buf, sem, 0, n_rows)

    @pl.when(i + 1 < n)
    def _():
        _gather_issue(nxt_ref, rows_hbm, buf, sem, 1 - slot, n_rows)

    _gather_wait(buf, sem, slot)
    g = g_ref[...]
    gk = [jnp.broadcast_to(g[:, k:k + 1], (tc, LANES)) for k in range(TOP_K)]
    hi_parts, lo_parts = [], []
    for s in range(buf.shape[1]):
        acc_hi = acc_lo = None
        for k in range(TOP_K):
            hi, lo = _unpack_bf16_pairs(buf[slot, s, pl.ds(k * tc, tc), :])
            acc_hi = gk[k] * hi if acc_hi is None else acc_hi + gk[k] * hi
            acc_lo = gk[k] * lo if acc_lo is None else acc_lo + gk[k] * lo
        hi_parts.append(acc_hi)
        lo_parts.append(acc_lo)
    y = jnp.concatenate(hi_parts + lo_parts, axis=-1)
    z = DEEPNORM_ALPHA * x_ref[...] + gate_ref[...] * y
    o_ref[...] = _layer_norm(z, lng_ref[...], lnb_ref[...])


def _combine(out_rows, dest, topg, x1, mod, k_ffn, ln_g, ln_b, seq, tc):
    t, d = x1.shape
    n = t // tc
    tiles_per_batch = seq // tc
    idx = dest.reshape(n, tc, TOP_K).transpose(0, 2, 1).reshape(n, 1, TOP_K * tc)
    smem = lambda f: pl.BlockSpec((1, 1, TOP_K * tc), f, memory_space=pltpu.SMEM)
    tok = lambda w: pl.BlockSpec((tc, w), lambda i: (i, 0))
    return pl.pallas_call(
        functools.partial(_combine_kernel, tc=tc),
        grid=(n,),
        in_specs=[smem(lambda i: (i, 0, 0)),
                  smem(lambda i: (jnp.minimum(i + 1, n - 1), 0, 0)),
                  pl.BlockSpec(memory_space=pl.ANY),
                  tok(LANES), tok(d),
                  _mod_spec(k_ffn, 2, tiles_per_batch),
                  _resident((1, d)), _resident((1, d))],
        out_specs=tok(d),
        out_shape=jax.ShapeDtypeStruct((t, d), F32),
        scratch_shapes=[pltpu.VMEM((2, PACKED_SLABS, TOP_K * tc, LANES), jnp.uint32),
                        pltpu.SemaphoreType.DMA((2,))],
        compiler_params=pltpu.CompilerParams(
            dimension_semantics=("arbitrary",),
            vmem_limit_bytes=_vmem_limit(2 * TOP_K * tc * d * 4 + 6 * tc * d * 4)),
        name="moe_combine_ln",
    )(idx, idx, out_rows, topg, x1, mod, ln_g.reshape(1, d), ln_b.reshape(1, d))


def _moe(h2, topi, topg, x1, mod, k_ffn, ln_g, ln_b, w_gu, b_gu, w_down, b_down, layer, seq, cfg):
    dest, row_tok, tile_start, tile_count, n_used = _route(topi, cfg["tm_moe"])
    x_rows = _dispatch(h2.reshape(-1, PACKED_SLABS, LANES), row_tok, n_used, cfg["tm_moe"])
    act = _gemm1(x_rows, tile_start, tile_count, n_used, w_gu, b_gu, layer, cfg["tf"], cfg["tc_gemm"])
    out_rows = _gemm2(act, tile_start, tile_count, n_used, w_down, b_down, layer, cfg["tn"])
    out_rows = out_rows.reshape(-1, PACKED_SLABS, LANES)
    return _combine(out_rows, dest, topg, x1, mod, k_ffn, ln_g, ln_b, seq, cfg["tc"])


def _softcap(g):
    return GATE_SOFTCAP * jnp.tanh(g / GATE_SOFTCAP)


def _log_sigmoid(x):
    return jnp.minimum(x, 0.0) - jnp.log(1.0 + jnp.exp(-jnp.abs(x)))


def _mlstm_proj_kernel(x_ref, shift_ref, scale_ref, wq_ref, wk_ref, wv_ref, wo_ref, wg_ref, wgt_ref,
                       bg_ref, bgt_ref, q_ref, k_ref, v_ref, og_ref, gl_ref, glt_ref):
    h = (x_ref[...] * (1.0 + scale_ref[...]) + shift_ref[...]).astype(BF16)
    q_ref[...] = (jnp.dot(h, wq_ref[...], preferred_element_type=F32) * MLSTM_QK_DIM ** -0.5).astype(BF16)
    k_ref[...] = jnp.dot(h, wk_ref[...], preferred_element_type=F32).astype(BF16)
    v_ref[...] = jnp.dot(h, wv_ref[...], preferred_element_type=F32).astype(BF16)
    og_ref[...] = jax.nn.sigmoid(jnp.dot(h, wo_ref[...], preferred_element_type=F32)).astype(BF16)
    g = _softcap(jnp.dot(h, wg_ref[...], preferred_element_type=F32) + bg_ref[...])
    lane = lax.broadcasted_iota(jnp.int32, g.shape, 1)
    gl_ref[...] = jnp.where(lane < MLSTM_HEADS, g,
                            jnp.where(lane < 2 * MLSTM_HEADS, _log_sigmoid(g), 0.0))
    gt = _softcap(lax.dot_general(wgt_ref[...], h, (((1,), (1,)), ((), ())),
                                  preferred_element_type=F32) + bgt_ref[...])
    sub = lax.broadcasted_iota(jnp.int32, gt.shape, 0)
    glt_ref[...] = jnp.where(sub < MLSTM_HEADS, gt, _log_sigmoid(gt))


def _mlstm_proj(x2d, mod, k_mod, w_in, b_gates, seq, tm):
    t, d = x2d.shape
    tiles_per_batch = seq // tm
    hk = MLSTM_HEADS * MLSTM_QK_DIM
    hv = MLSTM_HEADS * MLSTM_V_DIM
    ng = 2 * MLSTM_HEADS
    o1, o2, o3, o4 = hk, 2 * hk, 2 * hk + hv, 2 * hk + 2 * hv
    wq, wk, wv, wo = (w_in[:, a:b].astype(BF16) for a, b in ((0, o1), (o1, o2), (o2, o3), (o3, o4)))
    wg = jnp.concatenate([w_in[:, o4:], jnp.zeros((d, LANES - ng), F32)], axis=1).astype(BF16)
    wgt = w_in[:, o4:].T.astype(BF16)
    bg = jnp.concatenate([b_gates, jnp.zeros((LANES - ng,), F32)]).reshape(1, LANES)
    bgt = b_gates.reshape(ng, 1)
    tok = lambda n: pl.BlockSpec((tm, n), lambda i: (i, 0))
    est = 2 * d * (2 * hk + 2 * hv + LANES + ng) + 2 * tm * d * 4 + 2 * tm * (2 * hk + 2 * hv) * 2 \
        + 6 * tm * hv * 4
    return pl.pallas_call(
        _mlstm_proj_kernel,
        grid=(t // tm,),
        in_specs=[tok(d), _mod_spec(k_mod, 0, tiles_per_batch), _mod_spec(k_mod, 1, tiles_per_batch),
                  _resident(wq.shape), _resident(wk.shape), _resident(wv.shape), _resident(wo.shape),
                  _resident(wg.shape), _resident(wgt.shape), _resident((1, LANES)), _resident((ng, 1))],
        out_specs=[tok(hk), tok(hk), tok(hv), tok(hv), tok(LANES),
                   pl.BlockSpec((ng, tm), lambda i: (0, i))],
        out_shape=[jax.ShapeDtypeStruct((t, hk), BF16), jax.ShapeDtypeStruct((t, hk), BF16),
                   jax.ShapeDtypeStruct((t, hv), BF16), jax.ShapeDtypeStruct((t, hv), BF16),
                   jax.ShapeDtypeStruct((t, LANES), F32), jax.ShapeDtypeStruct((ng, t), F32)],
        compiler_params=pltpu.CompilerParams(
            dimension_semantics=("arbitrary",), vmem_limit_bytes=_vmem_limit(est)),
        name="mlstm_proj",
    )(x2d, mod, mod, wq, wk, wv, wo, wg, wgt, bg, bgt)


def _mlstm_scan_kernel(q_ref, k_ref, v_ref, og_ref, gl_ref, glt_ref, hn_ref, o_ref,
                       c_ref, n_ref, m_ref, *, chunk, heads_per_step):
    seq = q_ref.shape[0]
    L = chunk
    dk, dv = MLSTM_QK_DIM, MLSTM_V_DIM
    ng = 2 * MLSTM_HEADS
    head0 = pl.program_id(1) * heads_per_step
    hi = lax.Precision.HIGHEST

    c_ref[...] = jnp.zeros(c_ref.shape, F32)
    n_ref[...] = jnp.zeros(n_ref.shape, F32)
    m_ref[...] = jnp.zeros(m_ref.shape, F32)

    row = lax.broadcasted_iota(jnp.int32, (L, L), 0)
    col = lax.broadcasted_iota(jnp.int32, (L, L), 1)
    causal = col <= row
    tri = causal.astype(F32)
    tri_t = (row <= col).astype(F32)
    lane = lax.broadcasted_iota(jnp.int32, (L, LANES), 1)
    sub = lax.broadcasted_iota(jnp.int32, (ng, L), 0)

    def chunk_body(c, carry):
        r0 = pl.multiple_of(c * L, L)
        gl = gl_ref[pl.ds(r0, L), :]
        glt = glt_ref[:, pl.ds(r0, L)]
        for hh in range(heads_per_step):
            head = head0 + hh
            li_col = jnp.sum(jnp.where(lane == head, gl, 0.0), axis=-1, keepdims=True)
            lf_col = jnp.sum(jnp.where(lane == head + MLSTM_HEADS, gl, 0.0), axis=-1, keepdims=True)
            li_row = jnp.sum(jnp.where(sub == head, glt, 0.0), axis=0, keepdims=True)
            lf_row = jnp.sum(jnp.where(sub == head + MLSTM_HEADS, glt, 0.0), axis=0, keepdims=True)
            b_rep = jnp.dot(tri, jnp.broadcast_to(lf_col, (L, LANES)), precision=hi,
                            preferred_element_type=F32)
            b_col = b_rep[:, 0:1]
            b_row = jnp.dot(jnp.broadcast_to(lf_row, (8, L)), tri_t, precision=hi,
                            preferred_element_type=F32)[0:1, :]
            m_prev = m_ref[hh]
            d = jnp.where(causal, (b_rep if L == LANES else b_col) - b_row + li_row, NEG_INF)
            m_t = jnp.maximum(b_col + m_prev, jnp.max(d, axis=-1, keepdims=True))
            w = jnp.exp(d - m_t)
            inter = jnp.exp(b_col + m_prev - m_t)
            q = q_ref[pl.ds(r0, L), hh * dk:(hh + 1) * dk]
            k = k_ref[pl.ds(r0, L), hh * dk:(hh + 1) * dk]
            v = v_ref[pl.ds(r0, L), hh * dv:(hh + 1) * dv]
            qk = lax.dot_general(q, k, (((1,), (1,)), ((), ())), preferred_element_type=F32) * w
            c_state = c_ref[hh]
            n_state = n_ref[hh]
            num = inter * jnp.dot(q, c_state.astype(BF16), preferred_element_type=F32) \
                + jnp.dot(qk.astype(BF16), v, preferred_element_type=F32)
            den = inter * jnp.sum(q.astype(F32) * n_state, axis=-1, keepdims=True) \
                + jnp.sum(qk, axis=-1, keepdims=True)
            hs = num / jnp.maximum(jnp.abs(den), jnp.exp(-m_t))
            m_new = m_t[L - 1:L, :]
            b_last = b_col[L - 1:L, :]
            decay = jnp.exp(b_last + m_prev - m_new)
            ws = jnp.exp(b_last - b_col + li_col - m_new)
            kw = k.astype(F32) * ws
            c_ref[hh] = decay * c_state + jnp.dot(kw.T.astype(BF16), v, preferred_element_type=F32)
            n_ref[hh] = decay * n_state + jnp.sum(kw, axis=0, keepdims=True)
            m_ref[hh] = m_new
            y = hs * lax.rsqrt(jnp.mean(hs * hs, axis=-1, keepdims=True) + NORM_EPS) \
                * hn_ref[:, hh * dv:(hh + 1) * dv]
            o_ref[pl.ds(r0, L), hh * dv:(hh + 1) * dv] = (
                og_ref[pl.ds(r0, L), hh * dv:(hh + 1) * dv].astype(F32) * y).astype(BF16)
        return carry

    lax.fori_loop(0, seq // L, chunk_body, 0)


def _mlstm_scan(q, k, v, og, gl, glt, head_norm, batch, seq, chunk, heads_per_step):
    t = q.shape[0]
    hps = heads_per_step
    dk, dv = MLSTM_QK_DIM, MLSTM_V_DIM
    ng = 2 * MLSTM_HEADS
    est = 2 * seq * hps * (2 * dk + 3 * dv) * 2 + 2 * seq * LANES * 4 + 2 * ng * seq * 4 \
        + hps * dk * dv * 4 + 16 * chunk * chunk * 4
    return pl.pallas_call(
        functools.partial(_mlstm_scan_kernel, chunk=chunk, heads_per_step=hps),
        grid=(batch, MLSTM_HEADS // hps),
        in_specs=[pl.BlockSpec((seq, hps * dk), lambda b, g: (b, g)),
                  pl.BlockSpec((seq, hps * dk), lambda b, g: (b, g)),
                  pl.BlockSpec((seq, hps * dv), lambda b, g: (b, g)),
                  pl.BlockSpec((seq, hps * dv), lambda b, g: (b, g)),
                  pl.BlockSpec((seq, LANES), lambda b, g: (b, 0)),
                  pl.BlockSpec((ng, seq), lambda b, g: (0, b)),
                  pl.BlockSpec((1, hps * dv), lambda b, g: (0, g))],
        out_specs=pl.BlockSpec((seq, hps * dv), lambda b, g: (b, g)),
        out_shape=jax.ShapeDtypeStruct((t, MLSTM_HEADS * dv), BF16),
        scratch_shapes=[pltpu.VMEM((hps, dk, dv), F32), pltpu.VMEM((hps, 1, dk), F32),
                        pltpu.VMEM((hps, 1, 1), F32)],
        compiler_params=pltpu.CompilerParams(
            dimension_semantics=("arbitrary", "arbitrary"), vmem_limit_bytes=_vmem_limit(est)),
        name="mlstm_scan",
    )(q, k, v, og, gl, glt, head_norm.reshape(1, -1))


def kernel(x, c, positions, mla_w_in, mla_q_norm, mla_kv_norm, mla_w_uq, mla_w_ukv, mla_w_o, mlstm_w_in, mlstm_b_gates, mlstm_head_norm, mlstm_w_out, moe_w_router, moe_b_router, moe_w_gu, moe_b_gu, moe_w_down, moe_b_down, ada_w, ada_b, ln_g, ln_b):
    batch, seq, d = x.shape
    assert d == D_MODEL and ada_w.shape[0] == DEPTH
    cfg = _tiles(batch, seq)
    t = batch * seq
    tm = cfg["tm_tok"]
    mod = _adaln(c, ada_w, ada_b, cfg["tn_ada"])
    lng = ln_g.reshape(2 * DEPTH, d)
    lnb = ln_b.reshape(2 * DEPTH, d)
    xt = x.reshape(t, d)
    for i in range(DEPTH):
        j = i // 2
        k_mix, k_ffn = 2 * i, 2 * i + 1
        if i % 2 == 0:
            qn, qr, kn, v, kr = _mla_proj(xt, mod, k_mix, positions, mla_w_in[j], mla_q_norm[j],
                                          mla_kv_norm[j], mla_w_uq[j], mla_w_ukv[j], seq, tm)
            a = _attention(qn, qr, kn, kr, v, batch, seq, cfg["tq"])
            w_o = mla_w_o[j]
        else:
            q, k, v, og, gl, glt = _mlstm_proj(xt, mod, k_mix, mlstm_w_in[j], mlstm_b_gates[j], seq, tm)
            a = _mlstm_scan(q, k, v, og, gl, glt, mlstm_head_norm[j], batch, seq,
                            cfg["chunk"], cfg["heads_per_step"])
            w_o = mlstm_w_out[j]
        x1, h2, topi, topg = _post(a, xt, mod, k_mix, k_ffn, lng[k_mix], lnb[k_mix], w_o,
                                   moe_w_router[i], moe_b_router[i], seq, cfg["tm_post"])
        xt = _moe(h2, topi, topg, x1, mod, k_ffn, lng[k_ffn], lnb[k_ffn],
                  moe_w_gu, moe_b_gu, moe_w_down, moe_b_down, i, seq, cfg)
    return xt.reshape(batch, seq, d)
```

```python
import functools

import jax
import jax.numpy as jnp
from jax import lax
from jax.experimental import pallas as pl
from jax.experimental.pallas import tpu as pltpu

F32 = jnp.float32
BF16 = jnp.bfloat16

D_MODEL = 2048
DEPTH = 2

MLA_HEADS = 16
QK_NOPE_DIM = 128
QK_ROPE_DIM = 64
V_HEAD_DIM = 128
Q_LORA_RANK = 512
KV_LORA_RANK = 512
ROPE_THETA = 10000.0

MLSTM_HEADS = 8
MLSTM_QK_DIM = D_MODEL // (2 * MLSTM_HEADS)
MLSTM_V_DIM = D_MODEL // MLSTM_HEADS
GATE_SOFTCAP = 15.0

N_EXPERTS = 32
TOP_K = 4
D_EXPERT = D_MODEL
SWIGLU_ALPHA = 1.702
SWIGLU_LIMIT = 7.0

DEEPNORM_ALPHA = (2 * DEPTH) ** 0.25
NORM_EPS = 1e-6

LANES = 128
V7X_VMEM_BYTES = 64 * 1024 * 1024
NEG_INF = float("-inf")


def _vmem_limit(estimate_bytes):
    return int(min(estimate_bytes + (12 << 20), V7X_VMEM_BYTES - (8 << 20)))


def _tiles(batch, seq):
    t = batch * seq
    cfg = dict(
        tm_tok=min(256, seq),
        tm_post=min(512, seq),
        tq=min(256, seq),
        chunk=min(128, seq),
        heads_per_step=4,
        tm_moe=min(512, t * TOP_K // 8),
        tf=1024,
        tc_gemm=512,
        tn=512,
        tc=min(256, seq),
        tn_ada=768,
    )
    return cfg


def _adaln_kernel(c_ref, w_ref, b_ref, o_ref):
    c = c_ref[...]
    cond = c * jax.nn.sigmoid(c)
    o_ref[...] = jnp.dot(cond.astype(BF16), w_ref[...].astype(BF16),
                         preferred_element_type=F32) + b_ref[...]


def _adaln(c, ada_w, ada_b, tn):
    b, d = c.shape
    n_mod = ada_w.shape[0] * ada_w.shape[1]
    w = ada_w.reshape(n_mod, d, 3 * d)
    bias = ada_b.reshape(n_mod, 1, 3 * d)
    out = pl.pallas_call(
        _adaln_kernel,
        grid=(n_mod, 3 * d // tn),
        in_specs=[
            pl.BlockSpec((b, d), lambda i, j: (0, 0)),
            pl.BlockSpec((None, d, tn), lambda i, j: (i, 0, j)),
            pl.BlockSpec((None, 1, tn), lambda i, j: (i, 0, j)),
        ],
        out_specs=pl.BlockSpec((None, b, tn), lambda i, j: (i, 0, j)),
        out_shape=jax.ShapeDtypeStruct((n_mod, b, 3 * d), F32),
        compiler_params=pltpu.CompilerParams(
            dimension_semantics=("arbitrary", "arbitrary"),
            vmem_limit_bytes=_vmem_limit(2 * d * tn * 4 + d * tn * 2)),
        name="adaln",
    )(c, w, bias)
    return out.reshape(n_mod, b, 1, 3 * d)


def _mod_spec(k, part, tiles_per_batch):
    return pl.BlockSpec((None, None, 1, D_MODEL),
                        lambda i, *_: (k, i // tiles_per_batch, 0, part))


def _row_spec(k):
    return pl.BlockSpec((None, 1, D_MODEL), lambda i, *_: (k, 0, 0))


def _resident(shape):
    nd = len(shape)
    return pl.BlockSpec(shape, lambda *_: (0,) * nd, pipeline_mode=pl.Buffered(1))


def _mla_proj_kernel(x_ref, shift_ref, scale_ref, pos_ref, rope_ref, win_ref, qnorm_ref, kvnorm_ref,
                     wqn_ref, wqr_ref, wqs_ref, wkn_ref, wv_ref,
                     qn_ref, qr_ref, kn_ref, v_ref, kr_ref):
    h = x_ref[...] * (1.0 + scale_ref[...]) + shift_ref[...]
    lat = jnp.dot(h.astype(BF16), win_ref[...], preferred_element_type=F32)
    cq = lat[:, :Q_LORA_RANK]
    ckv = lat[:, Q_LORA_RANK:Q_LORA_RANK + KV_LORA_RANK]
    kr = lat[:, Q_LORA_RANK + KV_LORA_RANK:Q_LORA_RANK + KV_LORA_RANK + LANES]
    kr_sw = lat[:, Q_LORA_RANK + KV_LORA_RANK + LANES:]
    cq = cq * lax.rsqrt(jnp.mean(cq * cq, axis=-1, keepdims=True) + NORM_EPS) * qnorm_ref[...]
    ckv = ckv * lax.rsqrt(jnp.mean(ckv * ckv, axis=-1, keepdims=True) + NORM_EPS) * kvnorm_ref[...]
    cq = cq.astype(BF16)
    ckv = ckv.astype(BF16)

    ang = pos_ref[...].astype(F32) * rope_ref[0:1, :]
    cos_t = jnp.cos(ang) * rope_ref[1:2, :]
    sin_t = jnp.sin(ang) * rope_ref[2:3, :]

    qn_ref[...] = jnp.dot(cq, wqn_ref[...], preferred_element_type=F32).astype(BF16)
    q_rope = jnp.dot(cq, wqr_ref[...], preferred_element_type=F32)
    q_swap = jnp.dot(cq, wqs_ref[...], preferred_element_type=F32)
    cos_h = jnp.tile(cos_t, (1, MLA_HEADS))
    sin_h = jnp.tile(sin_t, (1, MLA_HEADS))
    qr_ref[...] = (q_rope * cos_h + q_swap * sin_h).astype(BF16)
    kn_ref[...] = jnp.dot(ckv, wkn_ref[...], preferred_element_type=F32).astype(BF16)
    v_ref[...] = jnp.dot(ckv, wv_ref[...], preferred_element_type=F32).astype(BF16)
    kr_ref[...] = (kr * cos_t + kr_sw * sin_t).astype(BF16)


def _rope_table():
    half = QK_ROPE_DIM // 2
    inv_freq = ROPE_THETA ** (-jnp.arange(0, QK_ROPE_DIM, 2, dtype=F32) / QK_ROPE_DIM)
    zeros = jnp.zeros((LANES - QK_ROPE_DIM,), F32)
    freq_row = jnp.concatenate([inv_freq, inv_freq, zeros])
    cos_mask = jnp.concatenate([jnp.ones((QK_ROPE_DIM,), F32), zeros])
    sin_sign = jnp.concatenate([-jnp.ones((half,), F32), jnp.ones((half,), F32), zeros])
    pad = jnp.zeros((5, LANES), F32)
    return jnp.concatenate([jnp.stack([freq_row, cos_mask, sin_sign]), pad], axis=0)


def _mla_weights(w_in, w_uq, w_ukv):
    d = w_in.shape[0]
    h = MLA_HEADS
    half = QK_ROPE_DIM // 2
    lat = Q_LORA_RANK + KV_LORA_RANK
    kr = w_in[:, lat:]
    zpad = jnp.zeros((d, LANES - QK_ROPE_DIM), w_in.dtype)
    win = jnp.concatenate([w_in[:, :lat], kr, zpad, kr[:, half:], kr[:, :half], zpad], axis=1)
    wq = w_uq.reshape(Q_LORA_RANK, h, QK_NOPE_DIM + QK_ROPE_DIM)
    wqn = wq[:, :, :QK_NOPE_DIM].reshape(Q_LORA_RANK, h * QK_NOPE_DIM)
    rope = wq[:, :, QK_NOPE_DIM:]
    zq = jnp.zeros((Q_LORA_RANK, h, LANES - QK_ROPE_DIM), w_uq.dtype)
    wqr = jnp.concatenate([rope, zq], axis=-1).reshape(Q_LORA_RANK, h * LANES)
    wqs = jnp.concatenate([rope[..., half:], rope[..., :half], zq], axis=-1).reshape(Q_LORA_RANK, h * LANES)
    wkv = w_ukv.reshape(KV_LORA_RANK, h, QK_NOPE_DIM + V_HEAD_DIM)
    wkn = wkv[:, :, :QK_NOPE_DIM].reshape(KV_LORA_RANK, h * QK_NOPE_DIM)
    wv = wkv[:, :, QK_NOPE_DIM:].reshape(KV_LORA_RANK, h * V_HEAD_DIM)
    return tuple(a.astype(BF16) for a in (win, wqn, wqr, wqs, wkn, wv))


def _mla_proj(x2d, mod, k_mod, positions, w_in, q_norm, kv_norm, w_uq, w_ukv, seq, tm):
    t, d = x2d.shape
    tiles_per_batch = seq // tm
    win, wqn, wqr, wqs, wkn, wv = _mla_weights(w_in, w_uq, w_ukv)
    hn = MLA_HEADS * LANES
    tok = lambda n: pl.BlockSpec((tm, n), lambda i: (i, 0))
    weights_bytes = 2 * (win.size + wqn.size + wqr.size + wqs.size + wkn.size + wv.size)
    est = weights_bytes + 2 * tm * d * 4 + 2 * (4 * tm * hn * 2 + tm * LANES * 2) + 8 * tm * hn * 4
    outs = pl.pallas_call(
        _mla_proj_kernel,
        grid=(t // tm,),
        in_specs=[
            tok(d),
            _mod_spec(k_mod, 0, tiles_per_batch),
            _mod_spec(k_mod, 1, tiles_per_batch),
            pl.BlockSpec((tm, 1), lambda i: (i, 0)),
            _resident((8, LANES)),
            _resident(win.shape),
            _resident((1, Q_LORA_RANK)),
            _resident((1, KV_LORA_RANK)),
            _resident(wqn.shape), _resident(wqr.shape), _resident(wqs.shape),
            _resident(wkn.shape), _resident(wv.shape),
        ],
        out_specs=[tok(hn), tok(hn), tok(hn), tok(hn), tok(LANES)],
        out_shape=[jax.ShapeDtypeStruct((t, hn), BF16)] * 4 + [jax.ShapeDtypeStruct((t, LANES), BF16)],
        compiler_params=pltpu.CompilerParams(
            dimension_semantics=("arbitrary",), vmem_limit_bytes=_vmem_limit(est)),
        name="mla_proj",
    )(x2d, mod, mod, positions.reshape(t, 1), _rope_table(), win,
      q_norm.reshape(1, -1), kv_norm.reshape(1, -1), wqn, wqr, wqs, wkn, wv)
    return outs


def _attn_kernel(qn_ref, qr_ref, kn_ref, kr_ref, v_ref, o_ref, s_ref, *, tq, scale):
    seq = qn_ref.shape[0]
    row = lax.broadcasted_iota(jnp.int32, (tq, tq), 0)
    col = lax.broadcasted_iota(jnp.int32, (tq, tq), 1)
    causal = col <= row
    exp2_scale = scale * 1.4426950408889634

    def fold_lanes(a):
        return [a[:, g * LANES:(g + 1) * LANES] for g in range(tq // LANES)]

    for qi in range(seq // tq):
        rows = slice(qi * tq, (qi + 1) * tq)
        q = jnp.concatenate([qn_ref[rows, :], qr_ref[rows, :]], axis=-1)
        m_acc = jnp.full((tq, LANES), NEG_INF, F32)
        for j in range(qi + 1):
            cols = slice(j * tq, (j + 1) * tq)
            k = jnp.concatenate([kn_ref[cols, :], kr_ref[cols, :]], axis=-1)
            s = lax.dot_general(q, k, (((1,), (1,)), ((), ())), preferred_element_type=F32)
            if j == qi:
                s = jnp.where(causal, s, NEG_INF)
            s_ref[:, cols] = s
            for part in fold_lanes(s):
                m_acc = jnp.maximum(m_acc, part)
        m = jnp.max(m_acc, axis=-1, keepdims=True)
        l_acc = jnp.zeros((tq, LANES), F32)
        acc = jnp.zeros((tq, V_HEAD_DIM), F32)
        for j in range(qi + 1):
            cols = slice(j * tq, (j + 1) * tq)
            p = jnp.exp2((s_ref[:, cols] - m) * exp2_scale)
            for part in fold_lanes(p):
                l_acc = l_acc + part
            acc = acc + jnp.dot(p.astype(BF16), v_ref[cols, :], preferred_element_type=F32)
        l = jnp.sum(l_acc, axis=-1, keepdims=True)
        o_ref[rows, :] = (acc / l).astype(BF16)


def _attention(qn, qr, kn, kr, v, batch, seq, tq):
    t = qn.shape[0]
    head = lambda: pl.BlockSpec((seq, LANES), lambda b, h: (b, h))
    scale = (QK_NOPE_DIM + QK_ROPE_DIM) ** -0.5
    est = 2 * 6 * seq * LANES * 2 + tq * seq * 4 + 6 * tq * tq * 4
    return pl.pallas_call(
        functools.partial(_attn_kernel, tq=tq, scale=scale),
        grid=(batch, MLA_HEADS),
        in_specs=[head(), head(), head(), pl.BlockSpec((seq, LANES), lambda b, h: (b, 0)), head()],
        out_specs=head(),
        out_shape=jax.ShapeDtypeStruct((t, MLA_HEADS * V_HEAD_DIM), BF16),
        scratch_shapes=[pltpu.VMEM((tq, seq), F32)],
        compiler_params=pltpu.CompilerParams(
            dimension_semantics=("arbitrary", "arbitrary"), vmem_limit_bytes=_vmem_limit(est)),
        name="mla_attention",
    )(qn, qr, kn, kr, v)


SLABS = D_MODEL // LANES


PACKED_SLABS = SLABS // 2


def _store_slabs(ref, val, first_slab=0, first_row=0, pitch=SLABS):
    rows = val.shape[0]
    for s in range(val.shape[1] // LANES):
        ref[pl.ds(first_row * pitch + first_slab + s, rows, stride=pitch), :] = val[:, s * LANES:(s + 1) * LANES]


def _pack_bf16_pairs(hi, lo):
    hi = pltpu.bitcast(hi.astype(BF16).astype(F32), jnp.uint32)
    lo = pltpu.bitcast(lo.astype(BF16).astype(F32), jnp.uint32)
    return hi | (lo >> 16)


def _unpack_bf16_pairs(u):
    return pltpu.bitcast(u & jnp.uint32(0xFFFF0000), F32), pltpu.bitcast(u << 16, F32)


def _layer_norm(z, g, b):
    mu = jnp.mean(z, axis=-1, keepdims=True)
    zc = z - mu
    var = jnp.mean(zc * zc, axis=-1, keepdims=True)
    return zc * lax.rsqrt(var + NORM_EPS) * g + b


def _post_kernel(a_ref, x_ref, gate_ref, lng_ref, lnb_ref, shift_ref, scale_ref, wo_ref, wr_ref, br_ref,
                 x1_ref, h2_ref, topi_ref, topg_ref, *, sub):
    starts = range(0, a_ref.shape[0], sub)
    ys = [jnp.dot(a_ref[r0:r0 + sub, :], wo_ref[...], preferred_element_type=F32) for r0 in starts]
    for y, r0 in zip(ys, starts):
        _post_rows(slice(r0, r0 + sub), r0, y, x_ref, gate_ref, lng_ref, lnb_ref, shift_ref, scale_ref,
                   wr_ref, br_ref, x1_ref, h2_ref, topi_ref, topg_ref)


def _post_rows(rows, r0, y, x_ref, gate_ref, lng_ref, lnb_ref, shift_ref, scale_ref, wr_ref,
               br_ref, x1_ref, h2_ref, topi_ref, topg_ref):
    z = DEEPNORM_ALPHA * x_ref[rows, :] + gate_ref[...] * y
    x1 = _layer_norm(z, lng_ref[...], lnb_ref[...])
    x1_ref[rows, :] = x1
    h2 = x1 * (1.0 + scale_ref[...]) + shift_ref[...]
    half = h2.shape[1] // 2
    _store_slabs(h2_ref, _pack_bf16_pairs(h2[:, :half], h2[:, half:]), first_row=r0, pitch=PACKED_SLABS)
    h_hi = h2.astype(BF16)
    h_lo = (h2 - h_hi.astype(F32)).astype(BF16)
    t_hi = jnp.dot(h_hi, wr_ref[...], preferred_element_type=F32)
    t_lo = jnp.dot(h_lo, wr_ref[:, :LANES], preferred_element_type=F32)
    logits = t_hi[:, :LANES] + (t_hi[:, LANES:] + t_lo) + br_ref[...]
    tm = logits.shape[0]
    lane = lax.broadcasted_iota(jnp.int32, (tm, LANES), 1)
    vals, idxs = [], []
    for _ in range(TOP_K):
        mk = jnp.max(logits, axis=-1, keepdims=True)
        ik = jnp.min(jnp.where(logits == mk, lane, LANES), axis=-1, keepdims=True)
        vals.append(mk)
        idxs.append(ik)
        logits = jnp.where(lane == ik, NEG_INF, logits)
    exps = [jnp.exp(v - vals[0]) for v in vals]
    den = exps[0]
    for e in exps[1:]:
        den = den + e
    topi = jnp.zeros((tm, LANES), jnp.int32)
    topg = jnp.zeros((tm, LANES), F32)
    for k in range(TOP_K):
        topi = jnp.where(lane == k, idxs[k], topi)
        topg = jnp.where(lane == k, exps[k] / den, topg)
    topi_ref[rows, :] = topi
    topg_ref[rows, :] = topg


def _post(a, x2d, mod, k_mix, k_ffn, ln_g, ln_b, w_o, w_router, b_router, seq, tm):
    t, d = x2d.shape
    tiles_per_batch = seq // tm
    wr = jnp.concatenate([w_router, jnp.zeros((d, LANES - N_EXPERTS), F32)], axis=1)
    wr_hi = wr.astype(BF16)
    wr_lo = (wr - wr_hi.astype(F32)).astype(BF16)
    wr = jnp.concatenate([wr_hi, wr_lo], axis=1)
    br = jnp.concatenate([b_router, jnp.full((LANES - N_EXPERTS,), NEG_INF, F32)]).reshape(1, LANES)
    tok = lambda n: pl.BlockSpec((tm, n), lambda i: (i, 0))
    sub = min(256, tm)
    est = d * d * 2 + d * LANES * 4 + 2 * tm * d * (2 + 4 + 4 + 4) + 6 * sub * d * 4
    return pl.pallas_call(
        functools.partial(_post_kernel, sub=sub),
        grid=(t // tm,),
        in_specs=[
            tok(d), tok(d),
            _mod_spec(k_mix, 2, tiles_per_batch),
            _resident((1, d)), _resident((1, d)),
            _mod_spec(k_ffn, 0, tiles_per_batch),
            _mod_spec(k_ffn, 1, tiles_per_batch),
            _resident((d, d)), _resident((d, 2 * LANES)), _resident((1, LANES)),
        ],
        out_specs=[tok(d), pl.BlockSpec((tm * PACKED_SLABS, LANES), lambda i: (i, 0)), tok(LANES), tok(LANES)],
        out_shape=[jax.ShapeDtypeStruct((t, d), F32), jax.ShapeDtypeStruct((t * PACKED_SLABS, LANES), jnp.uint32),
                   jax.ShapeDtypeStruct((t, LANES), jnp.int32), jax.ShapeDtypeStruct((t, LANES), F32)],
        compiler_params=pltpu.CompilerParams(
            dimension_semantics=("arbitrary",), vmem_limit_bytes=_vmem_limit(est)),
        name="mixer_out_ln_router",
    )(a, x2d, mod, ln_g.reshape(1, d), ln_b.reshape(1, d), mod, mod, w_o.astype(BF16), wr, br)


def _route(topi, tm_moe):
    t = topi.shape[0]
    m = t * TOP_K
    e_tk = topi[:, :TOP_K]
    a = jnp.sum((e_tk[:, :, None] == jnp.arange(N_EXPERTS, dtype=jnp.int32)).astype(F32), axis=1)
    blk = min(256, t)
    ab = a.reshape(t // blk, blk, N_EXPERTS)
    tri = jnp.tril(jnp.ones((blk, blk), F32))
    within = jnp.einsum("ts,bse->bte", tri, ab)
    btot = within[:, -1, :]
    boff = jnp.cumsum(btot, axis=0) - btot
    excl = (within + boff[:, None, :] - ab).reshape(t, N_EXPERTS)
    counts = (boff[-1] + btot[-1]).astype(jnp.int32)
    rank = jnp.take_along_axis(excl, e_tk, axis=1).astype(jnp.int32)
    padded = ((counts + tm_moe - 1) // tm_moe) * tm_moe
    pends = jnp.cumsum(padded)
    pstarts = pends - padded
    dest = (pstarts[e_tk] + rank).astype(jnp.int32).reshape(m)
    n_tiles = m // tm_moe + N_EXPERTS
    order = jnp.argsort(e_tk.reshape(m), stable=True).astype(jnp.int32)
    starts = jnp.cumsum(counts) - counts
    tile_first = jnp.arange(n_tiles, dtype=jnp.int32) * tm_moe
    e_tile = jnp.minimum(jnp.sum((tile_first[:, None] >= pends[None, :]).astype(jnp.int32), axis=1),
                         N_EXPERTS - 1)
    off = (tile_first - pstarts[e_tile])[:, None] + jnp.arange(tm_moe, dtype=jnp.int32)[None, :]
    valid = off < counts[e_tile][:, None]
    src = jnp.where(valid, starts[e_tile][:, None] + off, 0).reshape(n_tiles * tm_moe)
    row_tok = jnp.where(valid.reshape(n_tiles * tm_moe), order[src] // TOP_K, 0)
    tile_start = (pstarts // tm_moe).astype(jnp.int32)
    tile_count = (padded // tm_moe).astype(jnp.int32)
    n_used = (pends[-1] // tm_moe).astype(jnp.int32).reshape(1)
    return dest, row_tok, tile_start, tile_count, n_used


def _gather_issue(idx_ref, src_hbm, buf, sem, slot, n_rows):
    def body(i, carry):
        for u in range(2):
            r = 2 * i + u
            pltpu.make_async_copy(src_hbm.at[idx_ref[0, 0, r]], buf.at[slot, :, r, :],
                                  sem.at[slot]).start(priority=u)
        return carry
    lax.fori_loop(0, n_rows // 2, body, 0, unroll=4)


def _gather_wait(buf, sem, slot):
    pltpu.make_async_copy(buf.at[slot], buf.at[slot], sem.at[slot]).wait()


def _dispatch_kernel(ns_ref, idx_ref, nxt_ref, src_hbm, o_ref, buf, sem, *, n_rows):
    i = pl.program_id(0)
    n_active = ns_ref[0]
    slot = lax.rem(i, 2)

    @pl.when(jnp.logical_and(i == 0, n_active > 0))
    def _():
        _gather_issue(idx_ref, src_hbm, buf, sem, 0, n_rows)

    @pl.when(i + 1 < n_active)
    def _():
        _gather_issue(nxt_ref, src_hbm, buf, sem, 1 - slot, n_rows)

    @pl.when(i < n_active)
    def _():
        _gather_wait(buf, sem, slot)
        hi, lo = _unpack_bf16_pairs(buf[slot])
        o_ref[0:PACKED_SLABS] = hi.astype(BF16)
        o_ref[PACKED_SLABS:SLABS] = lo.astype(BF16)

    @pl.when(i >= n_active)
    def _():
        o_ref[...] = jnp.zeros(o_ref.shape, o_ref.dtype)


def _dispatch(h2_packed, row_tok, n_used, r):
    t, packed_slabs, _ = h2_packed.shape
    slabs = 2 * packed_slabs
    n = row_tok.shape[0] // r
    idx = row_tok.reshape(n, 1, r)
    smem = lambda f: pl.BlockSpec((1, 1, r), f, memory_space=pltpu.SMEM)
    return pl.pallas_call(
        functools.partial(_dispatch_kernel, n_rows=r),
        grid_spec=pltpu.PrefetchScalarGridSpec(
            num_scalar_prefetch=1,
            grid=(n,),
            in_specs=[smem(lambda i, ns: (i, 0, 0)),
                      smem(lambda i, ns: (jnp.minimum(i + 1, n - 1), 0, 0)),
                      pl.BlockSpec(memory_space=pl.ANY)],
            out_specs=pl.BlockSpec((None, slabs, r, LANES), lambda i, ns: (i, 0, 0, 0)),
            scratch_shapes=[pltpu.VMEM((2, packed_slabs, r, LANES), jnp.uint32),
                            pltpu.SemaphoreType.DMA((2,))]),
        out_shape=jax.ShapeDtypeStruct((n, slabs, r, LANES), BF16),
        compiler_params=pltpu.CompilerParams(
            dimension_semantics=("arbitrary",),
            vmem_limit_bytes=_vmem_limit(4 * r * slabs * LANES * 2)),
        name="moe_dispatch_gather",
    )(n_used, idx, idx, h2_packed)


def _expert_weights(step, n_steps, copies_for, stage, w_bf):
    @pl.when(step == 0)
    def _():
        for c in copies_for(step):
            c.start(priority=1)

    for c in copies_for(step):
        c.wait()
    w_bf[...] = stage[...].astype(BF16)

    @pl.when(step + 1 < n_steps)
    def _():
        for c in copies_for(step + 1):
            c.start(priority=1)


def _start_first_tile(n_tiles, in_copies):
    @pl.when(n_tiles > 0)
    def _():
        for c in in_copies(0, 0):
            c.start()


def _tile_loop(n_tiles, in_copies, out_copy, compute):
    def start_in(k, slot):
        for c in in_copies(k, slot):
            c.start()

    def tile(k, slot):
        for c in in_copies(k, slot):
            c.wait()

        @pl.when(k + 1 < n_tiles)
        def _():
            start_in(k + 1, 1 - slot)

        @pl.when(k >= 2)
        def _():
            out_copy(k - 2, slot).wait()

        compute(slot)
        out_copy(k, slot).start()

    def pair(p, carry):
        tile(2 * p, 0)

        @pl.when(2 * p + 1 < n_tiles)
        def _():
            tile(2 * p + 1, 1)
        return carry

    lax.fori_loop(0, (n_tiles + 1) // 2, pair, 0)

    for slot in (0, 1):
        @pl.when(jnp.logical_and(n_tiles >= 2, lax.rem(n_tiles, 2) == slot))
        def _():
            out_copy(n_tiles - 2, slot).wait()

        @pl.when(jnp.logical_and(n_tiles >= 1, lax.rem(n_tiles + 1, 2) == slot))
        def _():
            out_copy(n_tiles - 1, slot).wait()


def _zero_tail_tiles(first, n_total, zero_src, dst_for, sem):
    def body(t, carry):
        cp = pltpu.make_async_copy(zero_src, dst_for(t), sem)
        cp.start()
        cp.wait()
        return carry
    lax.fori_loop(first, n_total, body, 0)


def _gemm1_kernel(ts_ref, nt_ref, nu_ref, bg_ref, bu_ref, w_hbm, x_hbm, o_hbm,
                  stage, w_bf, xbuf, obuf, wsem, xsem, osem, *, layer, tf, tc):
    j = pl.program_id(0)
    e = pl.program_id(1)
    n_e = pl.num_programs(1)
    f = D_EXPERT

    def weight_copies(step):
        jj = step // n_e
        ee = step - jj * n_e
        return [pltpu.make_async_copy(
            w_hbm.at[layer, ee, :, pl.ds(pl.multiple_of(half * f + jj * tf, tf), tf)],
            stage.at[half], wsem.at[half]) for half in range(2)]

    t0 = ts_ref[e]

    def x_copies(k, slot):
        return [pltpu.make_async_copy(x_hbm.at[t0 + k], xbuf.at[slot], xsem.at[slot])]

    _start_first_tile(nt_ref[e], x_copies)
    _expert_weights(j * n_e + e, pl.num_programs(0) * n_e, weight_copies, stage, w_bf)

    def o_copy(k, slot):
        return pltpu.make_async_copy(obuf.at[slot], o_hbm.at[j, t0 + k], osem.at[slot])

    def compute(slot):
        x = jnp.concatenate([xbuf[slot, s] for s in range(xbuf.shape[1])], axis=-1)
        for c in range(tf // tc):
            cols = slice(c * tc, (c + 1) * tc)
            g = jnp.dot(x, w_bf[0, :, cols], preferred_element_type=F32) + bg_ref[:, cols]
            u = jnp.dot(x, w_bf[1, :, cols], preferred_element_type=F32) + bu_ref[:, cols]
            g = jnp.minimum(g, SWIGLU_LIMIT)
            u = jnp.clip(u, -SWIGLU_LIMIT, SWIGLU_LIMIT)
            obuf[slot, :, cols] = (g * jax.nn.sigmoid(SWIGLU_ALPHA * g) * (u + 1.0)).astype(BF16)

    _tile_loop(nt_ref[e], x_copies, o_copy, compute)

    @pl.when(e == n_e - 1)
    def _():
        obuf[0] = jnp.zeros(obuf.shape[1:], obuf.dtype)
        _zero_tail_tiles(nu_ref[0], o_hbm.shape[1], obuf.at[0], lambda t: o_hbm.at[j, t], osem.at[0])


def _gemm1(x_rows, tile_start, tile_count, n_used, w_gu, b_gu, layer, tf, tc):
    n_tiles, slabs, tm, _ = x_rows.shape
    d = slabs * LANES
    f = D_EXPERT
    nf = f // tf
    bias = b_gu.reshape(b_gu.shape[0], N_EXPERTS, 1, 2 * f)
    bspec = lambda off: pl.BlockSpec((None, None, 1, tf), lambda j, e, ts, nt, nu: (layer, e, 0, j + off))
    any_spec = pl.BlockSpec(memory_space=pl.ANY)
    est = 2 * d * tf * (4 + 2) + 2 * tm * d * 2 + 2 * tm * tf * 2 + 6 * tm * tc * 4
    return pl.pallas_call(
        functools.partial(_gemm1_kernel, layer=layer, tf=tf, tc=tc),
        grid_spec=pltpu.PrefetchScalarGridSpec(
            num_scalar_prefetch=3,
            grid=(nf, N_EXPERTS),
            in_specs=[bspec(0), bspec(nf), any_spec, any_spec],
            out_specs=any_spec,
            scratch_shapes=[pltpu.VMEM((2, d, tf), F32), pltpu.VMEM((2, d, tf), BF16),
                            pltpu.VMEM((2, slabs, tm, LANES), BF16), pltpu.VMEM((2, tm, tf), BF16),
                            pltpu.SemaphoreType.DMA((2,)), pltpu.SemaphoreType.DMA((2,)),
                            pltpu.SemaphoreType.DMA((2,))]),
        out_shape=jax.ShapeDtypeStruct((nf, n_tiles, tm, tf), BF16),
        compiler_params=pltpu.CompilerParams(
            dimension_semantics=("arbitrary", "arbitrary"), vmem_limit_bytes=_vmem_limit(est)),
        name="moe_gate_up_swiglu",
    )(tile_start, tile_count, n_used, bias, bias, w_gu, x_rows)


def _gemm2_kernel(ts_ref, nt_ref, nu_ref, b_ref, w_hbm, a_hbm, o_hbm,
                  stage, w_bf, abuf, obuf, wsem, asem, osem, *, layer, tn):
    e = pl.program_id(0)
    nf, tm = abuf.shape[1], abuf.shape[2]
    tile_rows = tm * PACKED_SLABS

    def weight_copies(step):
        return [pltpu.make_async_copy(w_hbm.at[layer, step], stage, wsem.at[0])]

    t0 = ts_ref[e]

    def a_copies(k, slot):
        return [pltpu.make_async_copy(a_hbm.at[c, t0 + k], abuf.at[slot, c], asem.at[slot, c])
                for c in range(nf)]

    _start_first_tile(nt_ref[e], a_copies)
    _expert_weights(e, pl.num_programs(0), weight_copies, stage, w_bf)

    def o_tile(t):
        return o_hbm.at[pl.ds(pl.multiple_of(t * tile_rows, tile_rows), tile_rows), :]

    def o_copy(k, slot):
        return pltpu.make_async_copy(obuf.at[pl.ds(slot * tile_rows, tile_rows), :],
                                     o_tile(t0 + k), osem.at[slot])

    def compute(slot):
        a = jnp.concatenate([abuf[slot, c] for c in range(nf)], axis=-1)
        half = w_bf.shape[1] // 2
        for c in range(half // tn):
            res = [jnp.dot(a, w_bf[:, cols], preferred_element_type=F32) + b_ref[:, cols]
                   for cols in (slice(c * tn, (c + 1) * tn), slice(half + c * tn, half + (c + 1) * tn))]
            _store_slabs(obuf, _pack_bf16_pairs(*res), first_slab=c * tn // LANES, first_row=slot * tm,
                         pitch=PACKED_SLABS)

    _tile_loop(nt_ref[e], a_copies, o_copy, compute)

    @pl.when(e == pl.num_programs(0) - 1)
    def _():
        obuf[0:tile_rows, :] = jnp.zeros((tile_rows, LANES), obuf.dtype)
        _zero_tail_tiles(nu_ref[0], o_hbm.shape[0] // tile_rows, obuf.at[0:tile_rows, :], o_tile, osem.at[0])


def _gemm2(act, tile_start, tile_count, n_used, w_down, b_down, layer, tn):
    nf, n_tiles, tm, tf = act.shape
    f = nf * tf
    d = D_MODEL
    bias = b_down.reshape(b_down.shape[0], N_EXPERTS, 1, d)
    any_spec = pl.BlockSpec(memory_space=pl.ANY)
    est = f * d * (4 + 2) + 2 * tm * f * 2 + 2 * tm * d * 2 + 6 * tm * tn * 4
    return pl.pallas_call(
        functools.partial(_gemm2_kernel, layer=layer, tn=tn),
        grid_spec=pltpu.PrefetchScalarGridSpec(
            num_scalar_prefetch=3,
            grid=(N_EXPERTS,),
            in_specs=[pl.BlockSpec((None, None, 1, d), lambda e, ts, nt, nu: (layer, e, 0, 0)),
                      any_spec, any_spec],
            out_specs=any_spec,
            scratch_shapes=[pltpu.VMEM((f, d), F32), pltpu.VMEM((f, d), BF16),
                            pltpu.VMEM((2, nf, tm, tf), BF16),
                            pltpu.VMEM((2 * tm * PACKED_SLABS, LANES), jnp.uint32),
                            pltpu.SemaphoreType.DMA((1,)), pltpu.SemaphoreType.DMA((2, nf)),
                            pltpu.SemaphoreType.DMA((2,))]),
        out_shape=jax.ShapeDtypeStruct((n_tiles * tm * PACKED_SLABS, LANES), jnp.uint32),
        compiler_params=pltpu.CompilerParams(
            dimension_semantics=("arbitrary",), vmem_limit_bytes=_vmem_limit(est)),
        name="moe_down",
    )(tile_start, tile_count, n_used, bias, w_down, act)


def _combine_kernel(idx_ref, nxt_ref, rows_hbm, g_ref, x_ref, gate_ref, lng_ref, lnb_ref, o_ref,
                    buf, sem, *, tc):
    i = pl.program_id(0)
    n = pl.num_programs(0)
    slot = lax.rem(i, 2)
    n_rows = TOP_K * tc

    @pl.when(i == 0)
    def _():
        _gather_issue(idx_ref, rows_hbm, buf, sem, 0, n_rows)

    @pl.when(i + 1 < n)
    def _():
        _gather_issue(nxt_ref, rows_hbm, buf, sem, 1 - slot, n_rows)

    _gather_wait(buf, sem, slot)
    g = g_ref[...]
    gk = [jnp.broadcast_to(g[:, k:k + 1], (tc, LANES)) for k in range(TOP_K)]
    hi_parts, lo_parts = [], []
    for s in range(buf.shape[1]):
        acc_hi = acc_lo = None
        for k in range(TOP_K):
            hi, lo = _unpack_bf16_pairs(buf[slot, s, pl.ds(k * tc, tc), :])
            acc_hi = gk[k] * hi if acc_hi is None else acc_hi + gk[k] * hi
            acc_lo = gk[k] * lo if acc_lo is None else acc_lo + gk[k] * lo
        hi_parts.append(acc_hi)
        lo_parts.append(acc_lo)
    y = jnp.concatenate(hi_parts + lo_parts, axis=-1)
    z = DEEPNORM_ALPHA * x_ref[...] + gate_ref[...] * y
    o_ref[...] = _layer_norm(z, lng_ref[...], lnb_ref[...])


def _combine(out_rows, dest, topg, x1, mod, k_ffn, ln_g, ln_b, seq, tc):
    t, d = x1.shape
    n = t // tc
    tiles_per_batch = seq // tc
    idx = dest.reshape(n, tc, TOP_K).transpose(0, 2, 1).reshape(n, 1, TOP_K * tc)
    smem = lambda f: pl.BlockSpec((1, 1, TOP_K * tc), f, memory_space=pltpu.SMEM)
    tok = lambda w: pl.BlockSpec((tc, w), lambda i: (i, 0))
    return pl.pallas_call(
        functools.partial(_combine_kernel, tc=tc),
        grid=(n,),
        in_specs=[smem(lambda i: (i, 0, 0)),
                  smem(lambda i: (jnp.minimum(i + 1, n - 1), 0, 0)),
                  pl.BlockSpec(memory_space=pl.ANY),
                  tok(LANES), tok(d),
                  _mod_spec(k_ffn, 2, tiles_per_batch),
                  _resident((1, d)), _resident((1, d))],
        out_specs=tok(d),
        out_shape=jax.ShapeDtypeStruct((t, d), F32),
        scratch_shapes=[pltpu.VMEM((2, PACKED_SLABS, TOP_K * tc, LANES), jnp.uint32),
                        pltpu.SemaphoreType.DMA((2,))],
        compiler_params=pltpu.CompilerParams(
            dimension_semantics=("arbitrary",),
            vmem_limit_bytes=_vmem_limit(2 * TOP_K * tc * d * 4 + 6 * tc * d * 4)),
        name="moe_combine_ln",
    )(idx, idx, out_rows, topg, x1, mod, ln_g.reshape(1, d), ln_b.reshape(1, d))


def _moe(h2, topi, topg, x1, mod, k_ffn, ln_g, ln_b, w_gu, b_gu, w_down, b_down, layer, seq, cfg):
    dest, row_tok, tile_start, tile_count, n_used = _route(topi, cfg["tm_moe"])
    x_rows = _dispatch(h2.reshape(-1, PACKED_SLABS, LANES), row_tok, n_used, cfg["tm_moe"])
    act = _gemm1(x_rows, tile_start, tile_count, n_used, w_gu, b_gu, layer, cfg["tf"], cfg["tc_gemm"])
    out_rows = _gemm2(act, tile_start, tile_count, n_used, w_down, b_down, layer, cfg["tn"])
    out_rows = out_rows.reshape(-1, PACKED_SLABS, LANES)
    return _combine(out_rows, dest, topg, x1, mod, k_ffn, ln_g, ln_b, seq, cfg["tc"])


def _softcap(g):
    return GATE_SOFTCAP * jnp.tanh(g / GATE_SOFTCAP)


def _log_sigmoid(x):
    return jnp.minimum(x, 0.0) - jnp.log(1.0 + jnp.exp(-jnp.abs(x)))


def _mlstm_proj_kernel(x_ref, shift_ref, scale_ref, wq_ref, wk_ref, wv_ref, wo_ref, wg_ref, wgt_ref,
                       bg_ref, bgt_ref, q_ref, k_ref, v_ref, og_ref, gl_ref, glt_ref):
    h = (x_ref[...] * (1.0 + scale_ref[...]) + shift_ref[...]).astype(BF16)
    q_ref[...] = (jnp.dot(h, wq_ref[...], preferred_element_type=F32) * MLSTM_QK_DIM ** -0.5).astype(BF16)
    k_ref[...] = jnp.dot(h, wk_ref[...], preferred_element_type=F32).astype(BF16)
    v_ref[...] = jnp.dot(h, wv_ref[...], preferred_element_type=F32).astype(BF16)
    og_ref[...] = jax.nn.sigmoid(jnp.dot(h, wo_ref[...], preferred_element_type=F32)).astype(BF16)
    g = _softcap(jnp.dot(h, wg_ref[...], preferred_element_type=F32) + bg_ref[...])
    lane = lax.broadcasted_iota(jnp.int32, g.shape, 1)
    gl_ref[...] = jnp.where(lane < MLSTM_HEADS, g,
                            jnp.where(lane < 2 * MLSTM_HEADS, _log_sigmoid(g), 0.0))
    gt = _softcap(lax.dot_general(wgt_ref[...], h, (((1,), (1,)), ((), ())),
                                  preferred_element_type=F32) + bgt_ref[...])
    sub = lax.broadcasted_iota(jnp.int32, gt.shape, 0)
    glt_ref[...] = jnp.where(sub < MLSTM_HEADS, gt, _log_sigmoid(gt))


def _mlstm_proj(x2d, mod, k_mod, w_in, b_gates, seq, tm):
    t, d = x2d.shape
    tiles_per_batch = seq // tm
    hk = MLSTM_HEADS * MLSTM_QK_DIM
    hv = MLSTM_HEADS * MLSTM_V_DIM
    ng = 2 * MLSTM_HEADS
    o1, o2, o3, o4 = hk, 2 * hk, 2 * hk + hv, 2 * hk + 2 * hv
    wq, wk, wv, wo = (w_in[:, a:b].astype(BF16) for a, b in ((0, o1), (o1, o2), (o2, o3), (o3, o4)))
    wg = jnp.concatenate([w_in[:, o4:], jnp.zeros((d, LANES - ng), F32)], axis=1).astype(BF16)
    wgt = w_in[:, o4:].T.astype(BF16)
    bg = jnp.concatenate([b_gates, jnp.zeros((LANES - ng,), F32)]).reshape(1, LANES)
    bgt = b_gates.reshape(ng, 1)
    tok = lambda n: pl.BlockSpec((tm, n), lambda i: (i, 0))
    est = 2 * d * (2 * hk + 2 * hv + LANES + ng) + 2 * tm * d * 4 + 2 * tm * (2 * hk + 2 * hv) * 2 \
        + 6 * tm * hv * 4
    return pl.pallas_call(
        _mlstm_proj_kernel,
        grid=(t // tm,),
        in_specs=[tok(d), _mod_spec(k_mod, 0, tiles_per_batch), _mod_spec(k_mod, 1, tiles_per_batch),
                  _resident(wq.shape), _resident(wk.shape), _resident(wv.shape), _resident(wo.shape),
                  _resident(wg.shape), _resident(wgt.shape), _resident((1, LANES)), _resident((ng, 1))],
        out_specs=[tok(hk), tok(hk), tok(hv), tok(hv), tok(LANES),
                   pl.BlockSpec((ng, tm), lambda i: (0, i))],
        out_shape=[jax.ShapeDtypeStruct((t, hk), BF16), jax.ShapeDtypeStruct((t, hk), BF16),
                   jax.ShapeDtypeStruct((t, hv), BF16), jax.ShapeDtypeStruct((t, hv), BF16),
                   jax.ShapeDtypeStruct((t, LANES), F32), jax.ShapeDtypeStruct((ng, t), F32)],
        compiler_params=pltpu.CompilerParams(
            dimension_semantics=("arbitrary",), vmem_limit_bytes=_vmem_limit(est)),
        name="mlstm_proj",
    )(x2d, mod, mod, wq, wk, wv, wo, wg, wgt, bg, bgt)


def _mlstm_scan_kernel(q_ref, k_ref, v_ref, og_ref, gl_ref, glt_ref, hn_ref, o_ref,
                       c_ref, n_ref, m_ref, *, chunk, heads_per_step):
    seq = q_ref.shape[0]
    L = chunk
    dk, dv = MLSTM_QK_DIM, MLSTM_V_DIM
    ng = 2 * MLSTM_HEADS
    head0 = pl.program_id(1) * heads_per_step
    hi = lax.Precision.HIGHEST

    c_ref[...] = jnp.zeros(c_ref.shape, F32)
    n_ref[...] = jnp.zeros(n_ref.shape, F32)
    m_ref[...] = jnp.zeros(m_ref.shape, F32)

    row = lax.broadcasted_iota(jnp.int32, (L, L), 0)
    col = lax.broadcasted_iota(jnp.int32, (L, L), 1)
    causal = col <= row
    tri = causal.astype(F32)
    tri_t = (row <= col).astype(F32)
    lane = lax.broadcasted_iota(jnp.int32, (L, LANES), 1)
    sub = lax.broadcasted_iota(jnp.int32, (ng, L), 0)

    def chunk_body(c, carry):
        r0 = pl.multiple_of(c * L, L)
        gl = gl_ref[pl.ds(r0, L), :]
        glt = glt_ref[:, pl.ds(r0, L)]
        for hh in range(heads_per_step):
            head = head0 + hh
            li_col = jnp.sum(jnp.where(lane == head, gl, 0.0), axis=-1, keepdims=True)
            lf_col = jnp.sum(jnp.where(lane == head + MLSTM_HEADS, gl, 0.0), axis=-1, keepdims=True)
            li_row = jnp.sum(jnp.where(sub == head, glt, 0.0), axis=0, keepdims=True)
            lf_row = jnp.sum(jnp.where(sub == head + MLSTM_HEADS, glt, 0.0), axis=0, keepdims=True)
            b_rep = jnp.dot(tri, jnp.broadcast_to(lf_col, (L, LANES)), precision=hi,
                            preferred_element_type=F32)
            b_col = b_rep[:, 0:1]
            b_row = jnp.dot(jnp.broadcast_to(lf_row, (8, L)), tri_t, precision=hi,
                            preferred_element_type=F32)[0:1, :]
            m_prev = m_ref[hh]
            d = jnp.where(causal, (b_rep if L == LANES else b_col) - b_row + li_row, NEG_INF)
            m_t = jnp.maximum(b_col + m_prev, jnp.max(d, axis=-1, keepdims=True))
            w = jnp.exp(d - m_t)
            inter = jnp.exp(b_col + m_prev - m_t)
            q = q_ref[pl.ds(r0, L), hh * dk:(hh + 1) * dk]
            k = k_ref[pl.ds(r0, L), hh * dk:(hh + 1) * dk]
            v = v_ref[pl.ds(r0, L), hh * dv:(hh + 1) * dv]
            qk = lax.dot_general(q, k, (((1,), (1,)), ((), ())), preferred_element_type=F32) * w
            c_state = c_ref[hh]
            n_state = n_ref[hh]
            num = inter * jnp.dot(q, c_state.astype(BF16), preferred_element_type=F32) \
                + jnp.dot(qk.astype(BF16), v, preferred_element_type=F32)
            den = inter * jnp.sum(q.astype(F32) * n_state, axis=-1, keepdims=True) \
                + jnp.sum(qk, axis=-1, keepdims=True)
            hs = num / jnp.maximum(jnp.abs(den), jnp.exp(-m_t))
            m_new = m_t[L - 1:L, :]
            b_last = b_col[L - 1:L, :]
            decay = jnp.exp(b_last + m_prev - m_new)
            ws = jnp.exp(b_last - b_col + li_col - m_new)
            kw = k.astype(F32) * ws
            c_ref[hh] = decay * c_state + jnp.dot(kw.T.astype(BF16), v, preferred_element_type=F32)
            n_ref[hh] = decay * n_state + jnp.sum(kw, axis=0, keepdims=True)
            m_ref[hh] = m_new
            y = hs * lax.rsqrt(jnp.mean(hs * hs, axis=-1, keepdims=True) + NORM_EPS) \
                * hn_ref[:, hh * dv:(hh + 1) * dv]
            o_ref[pl.ds(r0, L), hh * dv:(hh + 1) * dv] = (
                og_ref[pl.ds(r0, L), hh * dv:(hh + 1) * dv].astype(F32) * y).astype(BF16)
        return carry

    lax.fori_loop(0, seq // L, chunk_body, 0)


def _mlstm_scan(q, k, v, og, gl, glt, head_norm, batch, seq, chunk, heads_per_step):
    t = q.shape[0]
    hps = heads_per_step
    dk, dv = MLSTM_QK_DIM, MLSTM_V_DIM
    ng = 2 * MLSTM_HEADS
    est = 2 * seq * hps * (2 * dk + 3 * dv) * 2 + 2 * seq * LANES * 4 + 2 * ng * seq * 4 \
        + hps * dk * dv * 4 + 16 * chunk * chunk * 4
    return pl.pallas_call(
        functools.partial(_mlstm_scan_kernel, chunk=chunk, heads_per_step=hps),
        grid=(batch, MLSTM_HEADS // hps),
        in_specs=[pl.BlockSpec((seq, hps * dk), lambda b, g: (b, g)),
                  pl.BlockSpec((seq, hps * dk), lambda b, g: (b, g)),
                  pl.BlockSpec((seq, hps * dv), lambda b, g: (b, g)),
                  pl.BlockSpec((seq, hps * dv), lambda b, g: (b, g)),
                  pl.BlockSpec((seq, LANES), lambda b, g: (b, 0)),
                  pl.BlockSpec((ng, seq), lambda b, g: (0, b)),
                  pl.BlockSpec((1, hps * dv), lambda b, g: (0, g))],
        out_specs=pl.BlockSpec((seq, hps * dv), lambda b, g: (b, g)),
        out_shape=jax.ShapeDtypeStruct((t, MLSTM_HEADS * dv), BF16),
        scratch_shapes=[pltpu.VMEM((hps, dk, dv), F32), pltpu.VMEM((hps, 1, dk), F32),
                        pltpu.VMEM((hps, 1, 1), F32)],
        compiler_params=pltpu.CompilerParams(
            dimension_semantics=("arbitrary", "arbitrary"), vmem_limit_bytes=_vmem_limit(est)),
        name="mlstm_scan",
    )(q, k, v, og, gl, glt, head_norm.reshape(1, -1))


def kernel(x, c, positions, mla_w_in, mla_q_norm, mla_kv_norm, mla_w_uq, mla_w_ukv, mla_w_o, mlstm_w_in, mlstm_b_gates, mlstm_head_norm, mlstm_w_out, moe_w_router, moe_b_router, moe_w_gu, moe_b_gu, moe_w_down, moe_b_down, ada_w, ada_b, ln_g, ln_b):
    batch, seq, d = x.shape
    assert d == D_MODEL and ada_w.shape[0] == DEPTH
    cfg = _tiles(batch, seq)
    t = batch * seq
    tm = cfg["tm_tok"]
    mod = _adaln(c, ada_w, ada_b, cfg["tn_ada"])
    lng = ln_g.reshape(2 * DEPTH, d)
    lnb = ln_b.reshape(2 * DEPTH, d)
    xt = x.reshape(t, d)
    for i in range(DEPTH):
        j = i // 2
        k_mix, k_ffn = 2 * i, 2 * i + 1
        if i % 2 == 0:
            qn, qr, kn, v, kr = _mla_proj(xt, mod, k_mix, positions, mla_w_in[j], mla_q_norm[j],
                                          mla_kv_norm[j], mla_w_uq[j], mla_w_ukv[j], seq, tm)
            a = _attention(qn, qr, kn, kr, v, batch, seq, cfg["tq"])
            w_o = mla_w_o[j]
        else:
            q, k, v, og, gl, glt = _mlstm_proj(xt, mod, k_mix, mlstm_w_in[j], mlstm_b_gates[j], seq, tm)
            a = _mlstm_scan(q, k, v, og, gl, glt, mlstm_head_norm[j], batch, seq,
                            cfg["chunk"], cfg["heads_per_step"])
            w_o = mlstm_w_out[j]
        x1, h2, topi, topg = _post(a, xt, mod, k_mix, k_ffn, lng[k_mix], lnb[k_mix], w_o,
                                   moe_w_router[i], moe_b_router[i], seq, cfg["tm_post"])
        xt = _moe(h2, topi, topg, x1, mod, k_ffn, lng[k_ffn], lnb[k_ffn],
                  moe_w_gu, moe_b_gu, moe_w_down, moe_b_down, i, seq, cfg)
    return xt.reshape(batch, seq, d)
```

```python
import functools

import jax
import jax.numpy as jnp
from jax import lax
from jax.experimental import pallas as pl
from jax.experimental.pallas import tpu as pltpu

F32 = jnp.float32
BF16 = jnp.bfloat16

D_MODEL = 2048
DEPTH = 2

MLA_HEADS = 16
QK_NOPE_DIM = 128
QK_ROPE_DIM = 64
V_HEAD_DIM = 128
Q_LORA_RANK = 512
KV_LORA_RANK = 512
ROPE_THETA = 10000.0

MLSTM_HEADS = 8
MLSTM_QK_DIM = D_MODEL // (2 * MLSTM_HEADS)
MLSTM_V_DIM = D_MODEL // MLSTM_HEADS
GATE_SOFTCAP = 15.0

N_EXPERTS = 32
TOP_K = 4
D_EXPERT = D_MODEL
SWIGLU_ALPHA = 1.702
SWIGLU_LIMIT = 7.0

DEEPNORM_ALPHA = (2 * DEPTH) ** 0.25
NORM_EPS = 1e-6

LANES = 128
V7X_VMEM_BYTES = 64 * 1024 * 1024
NEG_INF = float("-inf")


def _vmem_limit(estimate_bytes):
    return int(min(estimate_bytes + (12 << 20), V7X_VMEM_BYTES - (8 << 20)))


def _tiles(batch, seq):
    t = batch * seq
    cfg = dict(
        tm_tok=min(256, seq),
        tm_post=min(512, seq),
        tq=min(256, seq),
        chunk=min(128, seq),
        heads_per_step=4,
        tm_moe=min(512, t * TOP_K // 8),
        tf=1024,
        tc_gemm=512,
        tn=512,
        tc=min(128, seq),
        tn_ada=768,
    )
    return cfg


def _adaln_kernel(c_ref, w_ref, b_ref, o_ref):
    c = c_ref[...]
    cond = c * jax.nn.sigmoid(c)
    o_ref[...] = jnp.dot(cond.astype(BF16), w_ref[...].astype(BF16),
                         preferred_element_type=F32) + b_ref[...]


def _adaln(c, ada_w, ada_b, tn):
    b, d = c.shape
    n_mod = ada_w.shape[0] * ada_w.shape[1]
    w = ada_w.reshape(n_mod, d, 3 * d)
    bias = ada_b.reshape(n_mod, 1, 3 * d)
    out = pl.pallas_call(
        _adaln_kernel,
        grid=(n_mod, 3 * d // tn),
        in_specs=[
            pl.BlockSpec((b, d), lambda i, j: (0, 0)),
            pl.BlockSpec((None, d, tn), lambda i, j: (i, 0, j)),
            pl.BlockSpec((None, 1, tn), lambda i, j: (i, 0, j)),
        ],
        out_specs=pl.BlockSpec((None, b, tn), lambda i, j: (i, 0, j)),
        out_shape=jax.ShapeDtypeStruct((n_mod, b, 3 * d), F32),
        compiler_params=pltpu.CompilerParams(
            dimension_semantics=("arbitrary", "arbitrary"),
            vmem_limit_bytes=_vmem_limit(2 * d * tn * 4 + d * tn * 2)),
        name="adaln",
    )(c, w, bias)
    return out.reshape(n_mod, b, 1, 3 * d)


def _mod_spec(k, part, tiles_per_batch):
    return pl.BlockSpec((None, None, 1, D_MODEL),
                        lambda i, *_: (k, i // tiles_per_batch, 0, part))


def _row_spec(k):
    return pl.BlockSpec((None, 1, D_MODEL), lambda i, *_: (k, 0, 0))


def _resident(shape):
    nd = len(shape)
    return pl.BlockSpec(shape, lambda *_: (0,) * nd, pipeline_mode=pl.Buffered(1))


def _mla_proj_kernel(x_ref, shift_ref, scale_ref, pos_ref, rope_ref, win_ref, qnorm_ref, kvnorm_ref,
                     wqn_ref, wqr_ref, wqs_ref, wkn_ref, wv_ref,
                     qn_ref, qr_ref, kn_ref, v_ref, kr_ref):
    h = x_ref[...] * (1.0 + scale_ref[...]) + shift_ref[...]
    lat = jnp.dot(h.astype(BF16), win_ref[...], preferred_element_type=F32)
    cq = lat[:, :Q_LORA_RANK]
    ckv = lat[:, Q_LORA_RANK:Q_LORA_RANK + KV_LORA_RANK]
    kr = lat[:, Q_LORA_RANK + KV_LORA_RANK:Q_LORA_RANK + KV_LORA_RANK + LANES]
    kr_sw = lat[:, Q_LORA_RANK + KV_LORA_RANK + LANES:]
    cq = cq * lax.rsqrt(jnp.mean(cq * cq, axis=-1, keepdims=True) + NORM_EPS) * qnorm_ref[...]
    ckv = ckv * lax.rsqrt(jnp.mean(ckv * ckv, axis=-1, keepdims=True) + NORM_EPS) * kvnorm_ref[...]
    cq = cq.astype(BF16)
    ckv = ckv.astype(BF16)

    ang = pos_ref[...].astype(F32) * rope_ref[0:1, :]
    cos_t = jnp.cos(ang) * rope_ref[1:2, :]
    sin_t = jnp.sin(ang) * rope_ref[2:3, :]

    qn_ref[...] = jnp.dot(cq, wqn_ref[...], preferred_element_type=F32).astype(BF16)
    q_rope = jnp.dot(cq, wqr_ref[...], preferred_element_type=F32)
    q_swap = jnp.dot(cq, wqs_ref[...], preferred_element_type=F32)
    cos_h = jnp.tile(cos_t, (1, MLA_HEADS))
    sin_h = jnp.tile(sin_t, (1, MLA_HEADS))
    qr_ref[...] = (q_rope * cos_h + q_swap * sin_h).astype(BF16)
    kn_ref[...] = jnp.dot(ckv, wkn_ref[...], preferred_element_type=F32).astype(BF16)
    v_ref[...] = jnp.dot(ckv, wv_ref[...], preferred_element_type=F32).astype(BF16)
    kr_ref[...] = (kr * cos_t + kr_sw * sin_t).astype(BF16)


def _rope_table():
    half = QK_ROPE_DIM // 2
    inv_freq = ROPE_THETA ** (-jnp.arange(0, QK_ROPE_DIM, 2, dtype=F32) / QK_ROPE_DIM)
    zeros = jnp.zeros((LANES - QK_ROPE_DIM,), F32)
    freq_row = jnp.concatenate([inv_freq, inv_freq, zeros])
    cos_mask = jnp.concatenate([jnp.ones((QK_ROPE_DIM,), F32), zeros])
    sin_sign = jnp.concatenate([-jnp.ones((half,), F32), jnp.ones((half,), F32), zeros])
    pad = jnp.zeros((5, LANES), F32)
    return jnp.concatenate([jnp.stack([freq_row, cos_mask, sin_sign]), pad], axis=0)


def _mla_weights(w_in, w_uq, w_ukv):
    d = w_in.shape[0]
    h = MLA_HEADS
    half = QK_ROPE_DIM // 2
    lat = Q_LORA_RANK + KV_LORA_RANK
    kr = w_in[:, lat:]
    zpad = jnp.zeros((d, LANES - QK_ROPE_DIM), w_in.dtype)
    win = jnp.concatenate([w_in[:, :lat], kr, zpad, kr[:, half:], kr[:, :half], zpad], axis=1)
    wq = w_uq.reshape(Q_LORA_RANK, h, QK_NOPE_DIM + QK_ROPE_DIM)
    wqn = wq[:, :, :QK_NOPE_DIM].reshape(Q_LORA_RANK, h * QK_NOPE_DIM)
    rope = wq[:, :, QK_NOPE_DIM:]
    zq = jnp.zeros((Q_LORA_RANK, h, LANES - QK_ROPE_DIM), w_uq.dtype)
    wqr = jnp.concatenate([rope, zq], axis=-1).reshape(Q_LORA_RANK, h * LANES)
    wqs = jnp.concatenate([rope[..., half:], rope[..., :half], zq], axis=-1).reshape(Q_LORA_RANK, h * LANES)
    wkv = w_ukv.reshape(KV_LORA_RANK, h, QK_NOPE_DIM + V_HEAD_DIM)
    wkn = wkv[:, :, :QK_NOPE_DIM].reshape(KV_LORA_RANK, h * QK_NOPE_DIM)
    wv = wkv[:, :, QK_NOPE_DIM:].reshape(KV_LORA_RANK, h * V_HEAD_DIM)
    return tuple(a.astype(BF16) for a in (win, wqn, wqr, wqs, wkn, wv))


def _mla_proj(x2d, mod, k_mod, positions, w_in, q_norm, kv_norm, w_uq, w_ukv, seq, tm):
    t, d = x2d.shape
    tiles_per_batch = seq // tm
    win, wqn, wqr, wqs, wkn, wv = _mla_weights(w_in, w_uq, w_ukv)
    hn = MLA_HEADS * LANES
    tok = lambda n: pl.BlockSpec((tm, n), lambda i: (i, 0))
    weights_bytes = 2 * (win.size + wqn.size + wqr.size + wqs.size + wkn.size + wv.size)
    est = weights_bytes + 2 * tm * d * 4 + 2 * (4 * tm * hn * 2 + tm * LANES * 2) + 8 * tm * hn * 4
    outs = pl.pallas_call(
        _mla_proj_kernel,
        grid=(t // tm,),
        in_specs=[
            tok(d),
            _mod_spec(k_mod, 0, tiles_per_batch),
            _mod_spec(k_mod, 1, tiles_per_batch),
            pl.BlockSpec((tm, 1), lambda i: (i, 0)),
            _resident((8, LANES)),
            _resident(win.shape),
            _resident((1, Q_LORA_RANK)),
            _resident((1, KV_LORA_RANK)),
            _resident(wqn.shape), _resident(wqr.shape), _resident(wqs.shape),
            _resident(wkn.shape), _resident(wv.shape),
        ],
        out_specs=[tok(hn), tok(hn), tok(hn), tok(hn), tok(LANES)],
        out_shape=[jax.ShapeDtypeStruct((t, hn), BF16)] * 4 + [jax.ShapeDtypeStruct((t, LANES), BF16)],
        compiler_params=pltpu.CompilerParams(
            dimension_semantics=("arbitrary",), vmem_limit_bytes=_vmem_limit(est)),
        name="mla_proj",
    )(x2d, mod, mod, positions.reshape(t, 1), _rope_table(), win,
      q_norm.reshape(1, -1), kv_norm.reshape(1, -1), wqn, wqr, wqs, wkn, wv)
    return outs


def _attn_kernel(qn_ref, qr_ref, kn_ref, kr_ref, v_ref, o_ref, s_ref, *, tq, scale):
    seq = qn_ref.shape[0]
    row = lax.broadcasted_iota(jnp.int32, (tq, tq), 0)
    col = lax.broadcasted_iota(jnp.int32, (tq, tq), 1)
    causal = col <= row
    exp2_scale = scale * 1.4426950408889634

    def fold_lanes(a):
        return [a[:, g * LANES:(g + 1) * LANES] for g in range(tq // LANES)]

    for qi in range(seq // tq):
        rows = slice(qi * tq, (qi + 1) * tq)
        q = jnp.concatenate([qn_ref[rows, :], qr_ref[rows, :]], axis=-1)
        m_acc = jnp.full((tq, LANES), NEG_INF, F32)
        for j in range(qi + 1):
            cols = slice(j * tq, (j + 1) * tq)
            k = jnp.concatenate([kn_ref[cols, :], kr_ref[cols, :]], axis=-1)
            s = lax.dot_general(q, k, (((1,), (1,)), ((), ())), preferred_element_type=F32)
            if j == qi:
                s = jnp.where(causal, s, NEG_INF)
            s_ref[:, cols] = s
            for part in fold_lanes(s):
                m_acc = jnp.maximum(m_acc, part)
        m = jnp.max(m_acc, axis=-1, keepdims=True)
        l_acc = jnp.zeros((tq, LANES), F32)
        acc = jnp.zeros((tq, V_HEAD_DIM), F32)
        for j in range(qi + 1):
            cols = slice(j * tq, (j + 1) * tq)
            p = jnp.exp2((s_ref[:, cols] - m) * exp2_scale)
            for part in fold_lanes(p):
                l_acc = l_acc + part
            acc = acc + jnp.dot(p.astype(BF16), v_ref[cols, :], preferred_element_type=F32)
        l = jnp.sum(l_acc, axis=-1, keepdims=True)
        o_ref[rows, :] = (acc / l).astype(BF16)


def _attention(qn, qr, kn, kr, v, batch, seq, tq):
    t = qn.shape[0]
    head = lambda: pl.BlockSpec((seq, LANES), lambda b, h: (b, h))
    scale = (QK_NOPE_DIM + QK_ROPE_DIM) ** -0.5
    est = 2 * 6 * seq * LANES * 2 + tq * seq * 4 + 6 * tq * tq * 4
    return pl.pallas_call(
        functools.partial(_attn_kernel, tq=tq, scale=scale),
        grid=(batch, MLA_HEADS),
        in_specs=[head(), head(), head(), pl.BlockSpec((seq, LANES), lambda b, h: (b, 0)), head()],
        out_specs=head(),
        out_shape=jax.ShapeDtypeStruct((t, MLA_HEADS * V_HEAD_DIM), BF16),
        scratch_shapes=[pltpu.VMEM((tq, seq), F32)],
        compiler_params=pltpu.CompilerParams(
            dimension_semantics=("arbitrary", "arbitrary"), vmem_limit_bytes=_vmem_limit(est)),
        name="mla_attention",
    )(qn, qr, kn, kr, v)


SLABS = D_MODEL // LANES


PACKED_SLABS = SLABS // 2


def _store_slabs(ref, val, first_slab=0, first_row=0, pitch=SLABS):
    rows = val.shape[0]
    for s in range(val.shape[1] // LANES):
        ref[pl.ds(first_row * pitch + first_slab + s, rows, stride=pitch), :] = val[:, s * LANES:(s + 1) * LANES]


def _pack_bf16_pairs(hi, lo):
    hi = pltpu.bitcast(hi.astype(BF16).astype(F32), jnp.uint32)
    lo = pltpu.bitcast(lo.astype(BF16).astype(F32), jnp.uint32)
    return hi | (lo >> 16)


def _unpack_bf16_pairs(u):
    return pltpu.bitcast(u & jnp.uint32(0xFFFF0000), F32), pltpu.bitcast(u << 16, F32)


def _layer_norm(z, g, b):
    mu = jnp.mean(z, axis=-1, keepdims=True)
    zc = z - mu
    var = jnp.mean(zc * zc, axis=-1, keepdims=True)
    return zc * lax.rsqrt(var + NORM_EPS) * g + b


def _post_kernel(a_ref, x_ref, gate_ref, lng_ref, lnb_ref, shift_ref, scale_ref, wo_ref, wr_ref, br_ref,
                 x1_ref, h2_ref, topi_ref, topg_ref, *, sub):
    starts = range(0, a_ref.shape[0], sub)
    ys = [jnp.dot(a_ref[r0:r0 + sub, :], wo_ref[...], preferred_element_type=F32) for r0 in starts]
    for y, r0 in zip(ys, starts):
        _post_rows(slice(r0, r0 + sub), r0, y, x_ref, gate_ref, lng_ref, lnb_ref, shift_ref, scale_ref,
                   wr_ref, br_ref, x1_ref, h2_ref, topi_ref, topg_ref)


def _post_rows(rows, r0, y, x_ref, gate_ref, lng_ref, lnb_ref, shift_ref, scale_ref, wr_ref,
               br_ref, x1_ref, h2_ref, topi_ref, topg_ref):
    z = DEEPNORM_ALPHA * x_ref[rows, :] + gate_ref[...] * y
    x1 = _layer_norm(z, lng_ref[...], lnb_ref[...])
    x1_ref[rows, :] = x1
    h2 = x1 * (1.0 + scale_ref[...]) + shift_ref[...]
    half = h2.shape[1] // 2
    _store_slabs(h2_ref, _pack_bf16_pairs(h2[:, :half], h2[:, half:]), first_row=r0, pitch=PACKED_SLABS)
    h_hi = h2.astype(BF16)
    h_lo = (h2 - h_hi.astype(F32)).astype(BF16)
    t_hi = jnp.dot(h_hi, wr_ref[...], preferred_element_type=F32)
    t_lo = jnp.dot(h_lo, wr_ref[:, :LANES], preferred_element_type=F32)
    logits = t_hi[:, :LANES] + (t_hi[:, LANES:] + t_lo) + br_ref[...]
    tm = logits.shape[0]
    lane = lax.broadcasted_iota(jnp.int32, (tm, LANES), 1)
    vals, idxs = [], []
    for _ in range(TOP_K):
        mk = jnp.max(logits, axis=-1, keepdims=True)
        ik = jnp.min(jnp.where(logits == mk, lane, LANES), axis=-1, keepdims=True)
        vals.append(mk)
        idxs.append(ik)
        logits = jnp.where(lane == ik, NEG_INF, logits)
    exps = [jnp.exp(v - vals[0]) for v in vals]
    den = exps[0]
    for e in exps[1:]:
        den = den + e
    topi = jnp.zeros((tm, LANES), jnp.int32)
    topg = jnp.zeros((tm, LANES), F32)
    for k in range(TOP_K):
        topi = jnp.where(lane == k, idxs[k], topi)
        topg = jnp.where(lane == k, exps[k] / den, topg)
    topi_ref[rows, :] = topi
    topg_ref[rows, :] = topg


def _post(a, x2d, mod, k_mix, k_ffn, ln_g, ln_b, w_o, w_router, b_router, seq, tm):
    t, d = x2d.shape
    tiles_per_batch = seq // tm
    wr = jnp.concatenate([w_router, jnp.zeros((d, LANES - N_EXPERTS), F32)], axis=1)
    wr_hi = wr.astype(BF16)
    wr_lo = (wr - wr_hi.astype(F32)).astype(BF16)
    wr = jnp.concatenate([wr_hi, wr_lo], axis=1)
    br = jnp.concatenate([b_router, jnp.full((LANES - N_EXPERTS,), NEG_INF, F32)]).reshape(1, LANES)
    tok = lambda n: pl.BlockSpec((tm, n), lambda i: (i, 0))
    sub = min(256, tm)
    est = d * d * 2 + d * LANES * 4 + 2 * tm * d * (2 + 4 + 4 + 4) + 6 * sub * d * 4
    return pl.pallas_call(
        functools.partial(_post_kernel, sub=sub),
        grid=(t // tm,),
        in_specs=[
            tok(d), tok(d),
            _mod_spec(k_mix, 2, tiles_per_batch),
            _resident((1, d)), _resident((1, d)),
            _mod_spec(k_ffn, 0, tiles_per_batch),
            _mod_spec(k_ffn, 1, tiles_per_batch),
            _resident((d, d)), _resident((d, 2 * LANES)), _resident((1, LANES)),
        ],
        out_specs=[tok(d), pl.BlockSpec((tm * PACKED_SLABS, LANES), lambda i: (i, 0)), tok(LANES), tok(LANES)],
        out_shape=[jax.ShapeDtypeStruct((t, d), F32), jax.ShapeDtypeStruct((t * PACKED_SLABS, LANES), jnp.uint32),
                   jax.ShapeDtypeStruct((t, LANES), jnp.int32), jax.ShapeDtypeStruct((t, LANES), F32)],
        compiler_params=pltpu.CompilerParams(
            dimension_semantics=("arbitrary",), vmem_limit_bytes=_vmem_limit(est)),
        name="mixer_out_ln_router",
    )(a, x2d, mod, ln_g.reshape(1, d), ln_b.reshape(1, d), mod, mod, w_o.astype(BF16), wr, br)


def _route(topi, tm_moe):
    t = topi.shape[0]
    m = t * TOP_K
    e_tk = topi[:, :TOP_K]
    a = jnp.sum((e_tk[:, :, None] == jnp.arange(N_EXPERTS, dtype=jnp.int32)).astype(F32), axis=1)
    blk = min(256, t)
    ab = a.reshape(t // blk, blk, N_EXPERTS)
    tri = jnp.tril(jnp.ones((blk, blk), F32))
    within = jnp.einsum("ts,bse->bte", tri, ab)
    btot = within[:, -1, :]
    boff = jnp.cumsum(btot, axis=0) - btot
    excl = (within + boff[:, None, :] - ab).reshape(t, N_EXPERTS)
    counts = (boff[-1] + btot[-1]).astype(jnp.int32)
    rank = jnp.take_along_axis(excl, e_tk, axis=1).astype(jnp.int32)
    padded = ((counts + tm_moe - 1) // tm_moe) * tm_moe
    pends = jnp.cumsum(padded)
    pstarts = pends - padded
    dest = (pstarts[e_tk] + rank).astype(jnp.int32).reshape(m)
    n_tiles = m // tm_moe + N_EXPERTS
    order = jnp.argsort(e_tk.reshape(m), stable=True).astype(jnp.int32)
    starts = jnp.cumsum(counts) - counts
    tile_first = jnp.arange(n_tiles, dtype=jnp.int32) * tm_moe
    e_tile = jnp.minimum(jnp.sum((tile_first[:, None] >= pends[None, :]).astype(jnp.int32), axis=1),
                         N_EXPERTS - 1)
    off = (tile_first - pstarts[e_tile])[:, None] + jnp.arange(tm_moe, dtype=jnp.int32)[None, :]
    valid = off < counts[e_tile][:, None]
    src = jnp.where(valid, starts[e_tile][:, None] + off, 0).reshape(n_tiles * tm_moe)
    row_tok = jnp.where(valid.reshape(n_tiles * tm_moe), order[src] // TOP_K, 0)
    tile_start = (pstarts // tm_moe).astype(jnp.int32)
    tile_count = (padded // tm_moe).astype(jnp.int32)
    n_used = (pends[-1] // tm_moe).astype(jnp.int32).reshape(1)
    return dest, row_tok, tile_start, tile_count, n_used


def _gather_issue(idx_ref, src_hbm, buf, sem, slot, n_rows):
    def body(i, carry):
        for u in range(2):
            r = 2 * i + u
            pltpu.make_async_copy(src_hbm.at[idx_ref[0, 0, r]], buf.at[slot, :, r, :],
                                  sem.at[slot]).start(priority=u)
        return carry
    lax.fori_loop(0, n_rows // 2, body, 0, unroll=4)


def _gather_wait(buf, sem, slot):
    pltpu.make_async_copy(buf.at[slot], buf.at[slot], sem.at[slot]).wait()


def _dispatch_kernel(ns_ref, idx_ref, nxt_ref, src_hbm, o_ref, buf, sem, *, n_rows):
    i = pl.program_id(0)
    n_active = ns_ref[0]
    slot = lax.rem(i, 2)

    @pl.when(jnp.logical_and(i == 0, n_active > 0))
    def _():
        _gather_issue(idx_ref, src_hbm, buf, sem, 0, n_rows)

    @pl.when(i + 1 < n_active)
    def _():
        _gather_issue(nxt_ref, src_hbm, buf, sem, 1 - slot, n_rows)

    @pl.when(i < n_active)
    def _():
        _gather_wait(buf, sem, slot)
        hi, lo = _unpack_bf16_pairs(buf[slot])
        o_ref[0:PACKED_SLABS] = hi.astype(BF16)
        o_ref[PACKED_SLABS:SLABS] = lo.astype(BF16)

    @pl.when(i >= n_active)
    def _():
        o_ref[...] = jnp.zeros(o_ref.shape, o_ref.dtype)


def _dispatch(h2_packed, row_tok, n_used, r):
    t, packed_slabs, _ = h2_packed.shape
    slabs = 2 * packed_slabs
    n = row_tok.shape[0] // r
    idx = row_tok.reshape(n, 1, r)
    smem = lambda f: pl.BlockSpec((1, 1, r), f, memory_space=pltpu.SMEM)
    return pl.pallas_call(
        functools.partial(_dispatch_kernel, n_rows=r),
        grid_spec=pltpu.PrefetchScalarGridSpec(
            num_scalar_prefetch=1,
            grid=(n,),
            in_specs=[smem(lambda i, ns: (i, 0, 0)),
                      smem(lambda i, ns: (jnp.minimum(i + 1, n - 1), 0, 0)),
                      pl.BlockSpec(memory_space=pl.ANY)],
            out_specs=pl.BlockSpec((None, slabs, r, LANES), lambda i, ns: (i, 0, 0, 0)),
            scratch_shapes=[pltpu.VMEM((2, packed_slabs, r, LANES), jnp.uint32),
                            pltpu.SemaphoreType.DMA((2,))]),
        out_shape=jax.ShapeDtypeStruct((n, slabs, r, LANES), BF16),
        compiler_params=pltpu.CompilerParams(
            dimension_semantics=("arbitrary",),
            vmem_limit_bytes=_vmem_limit(4 * r * slabs * LANES * 2)),
        name="moe_dispatch_gather",
    )(n_used, idx, idx, h2_packed)


def _expert_weights(step, n_steps, copies_for, stage, w_bf):
    @pl.when(step == 0)
    def _():
        for c in copies_for(step):
            c.start(priority=1)

    for c in copies_for(step):
        c.wait()
    w_bf[...] = stage[...].astype(BF16)

    @pl.when(step + 1 < n_steps)
    def _():
        for c in copies_for(step + 1):
            c.start(priority=1)


def _start_first_tile(n_tiles, in_copies):
    @pl.when(n_tiles > 0)
    def _():
        for c in in_copies(0, 0):
            c.start()


def _tile_loop(n_tiles, in_copies, out_copy, compute):
    def start_in(k, slot):
        for c in in_copies(k, slot):
            c.start()

    def tile(k, slot):
        for c in in_copies(k, slot):
            c.wait()

        @pl.when(k + 1 < n_tiles)
        def _():
            start_in(k + 1, 1 - slot)

        @pl.when(k >= 2)
        def _():
            out_copy(k - 2, slot).wait()

        compute(slot)
        out_copy(k, slot).start()

    def pair(p, carry):
        tile(2 * p, 0)

        @pl.when(2 * p + 1 < n_tiles)
        def _():
            tile(2 * p + 1, 1)
        return carry

    lax.fori_loop(0, (n_tiles + 1) // 2, pair, 0)

    for slot in (0, 1):
        @pl.when(jnp.logical_and(n_tiles >= 2, lax.rem(n_tiles, 2) == slot))
        def _():
            out_copy(n_tiles - 2, slot).wait()

        @pl.when(jnp.logical_and(n_tiles >= 1, lax.rem(n_tiles + 1, 2) == slot))
        def _():
            out_copy(n_tiles - 1, slot).wait()


def _zero_tail_tiles(first, n_total, zero_src, dst_for, sem):
    def body(t, carry):
        cp = pltpu.make_async_copy(zero_src, dst_for(t), sem)
        cp.start()
        cp.wait()
        return carry
    lax.fori_loop(first, n_total, body, 0)


def _gemm1_kernel(ts_ref, nt_ref, nu_ref, bg_ref, bu_ref, w_hbm, x_hbm, o_hbm,
                  stage, w_bf, xbuf, obuf, wsem, xsem, osem, *, layer, tf, tc):
    j = pl.program_id(0)
    e = pl.program_id(1)
    n_e = pl.num_programs(1)
    f = D_EXPERT

    def weight_copies(step):
        jj = step // n_e
        ee = step - jj * n_e
        return [pltpu.make_async_copy(
            w_hbm.at[layer, ee, :, pl.ds(pl.multiple_of(half * f + jj * tf, tf), tf)],
            stage.at[half], wsem.at[half]) for half in range(2)]

    t0 = ts_ref[e]

    def x_copies(k, slot):
        return [pltpu.make_async_copy(x_hbm.at[t0 + k], xbuf.at[slot], xsem.at[slot])]

    _start_first_tile(nt_ref[e], x_copies)
    _expert_weights(j * n_e + e, pl.num_programs(0) * n_e, weight_copies, stage, w_bf)

    def o_copy(k, slot):
        return pltpu.make_async_copy(obuf.at[slot], o_hbm.at[j, t0 + k], osem.at[slot])

    def compute(slot):
        x = jnp.concatenate([xbuf[slot, s] for s in range(xbuf.shape[1])], axis=-1)
        for c in range(tf // tc):
            cols = slice(c * tc, (c + 1) * tc)
            g = jnp.dot(x, w_bf[0, :, cols], preferred_element_type=F32) + bg_ref[:, cols]
            u = jnp.dot(x, w_bf[1, :, cols], preferred_element_type=F32) + bu_ref[:, cols]
            g = jnp.minimum(g, SWIGLU_LIMIT)
            u = jnp.clip(u, -SWIGLU_LIMIT, SWIGLU_LIMIT)
            obuf[slot, :, cols] = (g * jax.nn.sigmoid(SWIGLU_ALPHA * g) * (u + 1.0)).astype(BF16)

    _tile_loop(nt_ref[e], x_copies, o_copy, compute)

    @pl.when(e == n_e - 1)
    def _():
        obuf[0] = jnp.zeros(obuf.shape[1:], obuf.dtype)
        _zero_tail_tiles(nu_ref[0], o_hbm.shape[1], obuf.at[0], lambda t: o_hbm.at[j, t], osem.at[0])


def _gemm1(x_rows, tile_start, tile_count, n_used, w_gu, b_gu, layer, tf, tc):
    n_tiles, slabs, tm, _ = x_rows.shape
    d = slabs * LANES
    f = D_EXPERT
    nf = f // tf
    bias = b_gu.reshape(b_gu.shape[0], N_EXPERTS, 1, 2 * f)
    bspec = lambda off: pl.BlockSpec((None, None, 1, tf), lambda j, e, ts, nt, nu: (layer, e, 0, j + off))
    any_spec = pl.BlockSpec(memory_space=pl.ANY)
    est = 2 * d * tf * (4 + 2) + 2 * tm * d * 2 + 2 * tm * tf * 2 + 6 * tm * tc * 4
    return pl.pallas_call(
        functools.partial(_gemm1_kernel, layer=layer, tf=tf, tc=tc),
        grid_spec=pltpu.PrefetchScalarGridSpec(
            num_scalar_prefetch=3,
            grid=(nf, N_EXPERTS),
            in_specs=[bspec(0), bspec(nf), any_spec, any_spec],
            out_specs=any_spec,
            scratch_shapes=[pltpu.VMEM((2, d, tf), F32), pltpu.VMEM((2, d, tf), BF16),
                            pltpu.VMEM((2, slabs, tm, LANES), BF16), pltpu.VMEM((2, tm, tf), BF16),
                            pltpu.SemaphoreType.DMA((2,)), pltpu.SemaphoreType.DMA((2,)),
                            pltpu.SemaphoreType.DMA((2,))]),
        out_shape=jax.ShapeDtypeStruct((nf, n_tiles, tm, tf), BF16),
        compiler_params=pltpu.CompilerParams(
            dimension_semantics=("arbitrary", "arbitrary"), vmem_limit_bytes=_vmem_limit(est)),
        name="moe_gate_up_swiglu",
    )(tile_start, tile_count, n_used, bias, bias, w_gu, x_rows)


def _gemm2_kernel(ts_ref, nt_ref, nu_ref, b_ref, w_hbm, a_hbm, o_hbm,
                  stage, w_bf, abuf, obuf, wsem, asem, osem, *, layer, tn):
    e = pl.program_id(0)
    nf, tm = abuf.shape[1], abuf.shape[2]
    tile_rows = tm * PACKED_SLABS

    def weight_copies(step):
        return [pltpu.make_async_copy(w_hbm.at[layer, step], stage, wsem.at[0])]

    t0 = ts_ref[e]

    def a_copies(k, slot):
        return [pltpu.make_async_copy(a_hbm.at[c, t0 + k], abuf.at[slot, c], asem.at[slot, c])
                for c in range(nf)]

    _start_first_tile(nt_ref[e], a_copies)
    _expert_weights(e, pl.num_programs(0), weight_copies, stage, w_bf)

    def o_tile(t):
        return o_hbm.at[pl.ds(pl.multiple_of(t * tile_rows, tile_rows), tile_rows), :]

    def o_copy(k, slot):
        return pltpu.make_async_copy(obuf.at[pl.ds(slot * tile_rows, tile_rows), :],
                                     o_tile(t0 + k), osem.at[slot])

    def compute(slot):
        a = jnp.concatenate([abuf[slot, c] for c in range(nf)], axis=-1)
        half = w_bf.shape[1] // 2
        for c in range(half // tn):
            res = [jnp.dot(a, w_bf[:, cols], preferred_element_type=F32) + b_ref[:, cols]
                   for cols in (slice(c * tn, (c + 1) * tn), slice(half + c * tn, half + (c + 1) * tn))]
            _store_slabs(obuf, _pack_bf16_pairs(*res), first_slab=c * tn // LANES, first_row=slot * tm,
                         pitch=PACKED_SLABS)

    _tile_loop(nt_ref[e], a_copies, o_copy, compute)

    @pl.when(e == pl.num_programs(0) - 1)
    def _():
        obuf[0:tile_rows, :] = jnp.zeros((tile_rows, LANES), obuf.dtype)
        _zero_tail_tiles(nu_ref[0], o_hbm.shape[0] // tile_rows, obuf.at[0:tile_rows, :], o_tile, osem.at[0])


def _gemm2(act, tile_start, tile_count, n_used, w_down, b_down, layer, tn):
    nf, n_tiles, tm, tf = act.shape
    f = nf * tf
    d = D_MODEL
    bias = b_down.reshape(b_down.shape[0], N_EXPERTS, 1, d)
    any_spec = pl.BlockSpec(memory_space=pl.ANY)
    est = f * d * (4 + 2) + 2 * tm * f * 2 + 2 * tm * d * 2 + 6 * tm * tn * 4
    return pl.pallas_call(
        functools.partial(_gemm2_kernel, layer=layer, tn=tn),
        grid_spec=pltpu.PrefetchScalarGridSpec(
            num_scalar_prefetch=3,
            grid=(N_EXPERTS,),
            in_specs=[pl.BlockSpec((None, None, 1, d), lambda e, ts, nt, nu: (layer, e, 0, 0)),
                      any_spec, any_spec],
            out_specs=any_spec,
            scratch_shapes=[pltpu.VMEM((f, d), F32), pltpu.VMEM((f, d), BF16),
                            pltpu.VMEM((2, nf, tm, tf), BF16),
                            pltpu.VMEM((2 * tm * PACKED_SLABS, LANES), jnp.uint32),
                            pltpu.SemaphoreType.DMA((1,)), pltpu.SemaphoreType.DMA((2, nf)),
                            pltpu.SemaphoreType.DMA((2,))]),
        out_shape=jax.ShapeDtypeStruct((n_tiles * tm * PACKED_SLABS, LANES), jnp.uint32),
        compiler_params=pltpu.CompilerParams(
            dimension_semantics=("arbitrary",), vmem_limit_bytes=_vmem_limit(est)),
        name="moe_down",
    )(tile_start, tile_count, n_used, bias, w_down, act)


def _combine_kernel(idx_ref, nxt_ref, rows_hbm, g_ref, x_ref, gate_ref, lng_ref, lnb_ref, o_ref,
                    buf, sem, *, tc):
    i = pl.program_id(0)
    n = pl.num_programs(0)
    slot = lax.rem(i, 2)
    n_rows = TOP_K * tc

    @pl.when(i == 0)
    def _():
        _gather_issue(idx_ref, rows_hbm, buf, sem, 0, n_rows)

    _gather_wait(buf, sem, slot)
    g = g_ref[...]
    gk = [jnp.broadcast_to(g[:, k:k + 1], (tc, LANES)) for k in range(TOP_K)]
    n_slabs = buf.shape[1]
    rows_per_slab = n_rows // n_slabs
    hi_parts, lo_parts = [], []
    for s in range(n_slabs):
        acc_hi = acc_lo = None
        for k in range(TOP_K):
            hi, lo = _unpack_bf16_pairs(buf[slot, s, pl.ds(k * tc, tc), :])
            acc_hi = gk[k] * hi if acc_hi is None else acc_hi + gk[k] * hi
            acc_lo = gk[k] * lo if acc_lo is None else acc_lo + gk[k] * lo
        hi_parts.append(acc_hi)
        lo_parts.append(acc_lo)
        for r in range(s * rows_per_slab, (s + 1) * rows_per_slab):
            pltpu.make_async_copy(rows_hbm.at[nxt_ref[0, 0, r]], buf.at[1 - slot, :, r, :],
                                  sem.at[1 - slot]).start(priority=r % 2)
    y = jnp.concatenate(hi_parts + lo_parts, axis=-1)
    z = DEEPNORM_ALPHA * x_ref[...] + gate_ref[...] * y
    o_ref[...] = _layer_norm(z, lng_ref[...], lnb_ref[...])

    @pl.when(i == n - 1)
    def _():
        _gather_wait(buf, sem, 1 - slot)


def _combine(out_rows, dest, topg, x1, mod, k_ffn, ln_g, ln_b, seq, tc):
    t, d = x1.shape
    n = t // tc
    tiles_per_batch = seq // tc
    idx = dest.reshape(n, tc, TOP_K).transpose(0, 2, 1).reshape(n, 1, TOP_K * tc)
    smem = lambda f: pl.BlockSpec((1, 1, TOP_K * tc), f, memory_space=pltpu.SMEM)
    tok = lambda w: pl.BlockSpec((tc, w), lambda i: (i, 0))
    return pl.pallas_call(
        functools.partial(_combine_kernel, tc=tc),
        grid=(n,),
        in_specs=[smem(lambda i: (i, 0, 0)),
                  smem(lambda i: (jnp.minimum(i + 1, n - 1), 0, 0)),
                  pl.BlockSpec(memory_space=pl.ANY),
                  tok(LANES), tok(d),
                  _mod_spec(k_ffn, 2, tiles_per_batch),
                  _resident((1, d)), _resident((1, d))],
        out_specs=tok(d),
        out_shape=jax.ShapeDtypeStruct((t, d), F32),
        scratch_shapes=[pltpu.VMEM((2, PACKED_SLABS, TOP_K * tc, LANES), jnp.uint32),
                        pltpu.SemaphoreType.DMA((2,))],
        compiler_params=pltpu.CompilerParams(
            dimension_semantics=("arbitrary",),
            vmem_limit_bytes=_vmem_limit(2 * TOP_K * tc * d * 4 + 6 * tc * d * 4)),
        name="moe_combine_ln",
    )(idx, idx, out_rows, topg, x1, mod, ln_g.reshape(1, d), ln_b.reshape(1, d))


def _moe(h2, topi, topg, x1, mod, k_ffn, ln_g, ln_b, w_gu, b_gu, w_down, b_down, layer, seq, cfg):
    dest, row_tok, tile_start, tile_count, n_used = _route(topi, cfg["tm_moe"])
    x_rows = _dispatch(h2.reshape(-1, PACKED_SLABS, LANES), row_tok, n_used, cfg["tm_moe"])
    act = _gemm1(x_rows, tile_start, tile_count, n_used, w_gu, b_gu, layer, cfg["tf"], cfg["tc_gemm"])
    out_rows = _gemm2(act, tile_start, tile_count, n_used, w_down, b_down, layer, cfg["tn"])
    out_rows = out_rows.reshape(-1, PACKED_SLABS, LANES)
    return _combine(out_rows, dest, topg, x1, mod, k_ffn, ln_g, ln_b, seq, cfg["tc"])


def _softcap(g):
    return GATE_SOFTCAP * jnp.tanh(g / GATE_SOFTCAP)


def _log_sigmoid(x):
    return jnp.minimum(x, 0.0) - jnp.log(1.0 + jnp.exp(-jnp.abs(x)))


def _mlstm_proj_kernel(x_ref, shift_ref, scale_ref, wq_ref, wk_ref, wv_ref, wo_ref, wg_ref, wgt_ref,
                       bg_ref, bgt_ref, q_ref, k_ref, v_ref, og_ref, gl_ref, glt_ref):
    h = (x_ref[...] * (1.0 + scale_ref[...]) + shift_ref[...]).astype(BF16)
    q_ref[...] = (jnp.dot(h, wq_ref[...], preferred_element_type=F32) * MLSTM_QK_DIM ** -0.5).astype(BF16)
    k_ref[...] = jnp.dot(h, wk_ref[...], preferred_element_type=F32).astype(BF16)
    v_ref[...] = jnp.dot(h, wv_ref[...], preferred_element_type=F32).astype(BF16)
    og_ref[...] = jax.nn.sigmoid(jnp.dot(h, wo_ref[...], preferred_element_type=F32)).astype(BF16)
    g = _softcap(jnp.dot(h, wg_ref[...], preferred_element_type=F32) + bg_ref[...])
    lane = lax.broadcasted_iota(jnp.int32, g.shape, 1)
    gl_ref[...] = jnp.where(lane < MLSTM_HEADS, g,
                            jnp.where(lane < 2 * MLSTM_HEADS, _log_sigmoid(g), 0.0))
    gt = _softcap(lax.dot_general(wgt_ref[...], h, (((1,), (1,)), ((), ())),
                                  preferred_element_type=F32) + bgt_ref[...])
    sub = lax.broadcasted_iota(jnp.int32, gt.shape, 0)
    glt_ref[...] = jnp.where(sub < MLSTM_HEADS, gt, _log_sigmoid(gt))


def _mlstm_proj(x2d, mod, k_mod, w_in, b_gates, seq, tm):
    t, d = x2d.shape
    tiles_per_batch = seq // tm
    hk = MLSTM_HEADS * MLSTM_QK_DIM
    hv = MLSTM_HEADS * MLSTM_V_DIM
    ng = 2 * MLSTM_HEADS
    o1, o2, o3, o4 = hk, 2 * hk, 2 * hk + hv, 2 * hk + 2 * hv
    wq, wk, wv, wo = (w_in[:, a:b].astype(BF16) for a, b in ((0, o1), (o1, o2), (o2, o3), (o3, o4)))
    wg = jnp.concatenate([w_in[:, o4:], jnp.zeros((d, LANES - ng), F32)], axis=1).astype(BF16)
    wgt = w_in[:, o4:].T.astype(BF16)
    bg = jnp.concatenate([b_gates, jnp.zeros((LANES - ng,), F32)]).reshape(1, LANES)
    bgt = b_gates.reshape(ng, 1)
    tok = lambda n: pl.BlockSpec((tm, n), lambda i: (i, 0))
    est = 2 * d * (2 * hk + 2 * hv + LANES + ng) + 2 * tm * d * 4 + 2 * tm * (2 * hk + 2 * hv) * 2 \
        + 6 * tm * hv * 4
    return pl.pallas_call(
        _mlstm_proj_kernel,
        grid=(t // tm,),
        in_specs=[tok(d), _mod_spec(k_mod, 0, tiles_per_batch), _mod_spec(k_mod, 1, tiles_per_batch),
                  _resident(wq.shape), _resident(wk.shape), _resident(wv.shape), _resident(wo.shape),
                  _resident(wg.shape), _resident(wgt.shape), _resident((1, LANES)), _resident((ng, 1))],
        out_specs=[tok(hk), tok(hk), tok(hv), tok(hv), tok(LANES),
                   pl.BlockSpec((ng, tm), lambda i: (0, i))],
        out_shape=[jax.ShapeDtypeStruct((t, hk), BF16), jax.ShapeDtypeStruct((t, hk), BF16),
                   jax.ShapeDtypeStruct((t, hv), BF16), jax.ShapeDtypeStruct((t, hv), BF16),
                   jax.ShapeDtypeStruct((t, LANES), F32), jax.ShapeDtypeStruct((ng, t), F32)],
        compiler_params=pltpu.CompilerParams(
            dimension_semantics=("arbitrary",), vmem_limit_bytes=_vmem_limit(est)),
        name="mlstm_proj",
    )(x2d, mod, mod, wq, wk, wv, wo, wg, wgt, bg, bgt)


def _mlstm_scan_kernel(q_ref, k_ref, v_ref, og_ref, gl_ref, glt_ref, hn_ref, o_ref,
                       c_ref, n_ref, m_ref, *, chunk, heads_per_step):
    seq = q_ref.shape[0]
    L = chunk
    dk, dv = MLSTM_QK_DIM, MLSTM_V_DIM
    ng = 2 * MLSTM_HEADS
    head0 = pl.program_id(1) * heads_per_step
    hi = lax.Precision.HIGHEST

    c_ref[...] = jnp.zeros(c_ref.shape, F32)
    n_ref[...] = jnp.zeros(n_ref.shape, F32)
    m_ref[...] = jnp.zeros(m_ref.shape, F32)

    row = lax.broadcasted_iota(jnp.int32, (L, L), 0)
    col = lax.broadcasted_iota(jnp.int32, (L, L), 1)
    causal = col <= row
    tri = causal.astype(F32)
    tri_t = (row <= col).astype(F32)
    lane = lax.broadcasted_iota(jnp.int32, (L, LANES), 1)
    sub = lax.broadcasted_iota(jnp.int32, (ng, L), 0)

    def chunk_body(c, carry):
        r0 = pl.multiple_of(c * L, L)
        gl = gl_ref[pl.ds(r0, L), :]
        glt = glt_ref[:, pl.ds(r0, L)]
        for hh in range(heads_per_step):
            head = head0 + hh
            li_col = jnp.sum(jnp.where(lane == head, gl, 0.0), axis=-1, keepdims=True)
            lf_col = jnp.sum(jnp.where(lane == head + MLSTM_HEADS, gl, 0.0), axis=-1, keepdims=True)
            li_row = jnp.sum(jnp.where(sub == head, glt, 0.0), axis=0, keepdims=True)
            lf_row = jnp.sum(jnp.where(sub == head + MLSTM_HEADS, glt, 0.0), axis=0, keepdims=True)
            b_rep = jnp.dot(tri, jnp.broadcast_to(lf_col, (L, LANES)), precision=hi,
                            preferred_element_type=F32)
            b_col = b_rep[:, 0:1]
            b_row = jnp.dot(jnp.broadcast_to(lf_row, (8, L)), tri_t, precision=hi,
                            preferred_element_type=F32)[0:1, :]
            m_prev = m_ref[hh]
            d = jnp.where(causal, (b_rep if L == LANES else b_col) - b_row + li_row, NEG_INF)
            m_t = jnp.maximum(b_col + m_prev, jnp.max(d, axis=-1, keepdims=True))
            w = jnp.exp(d - m_t)
            inter = jnp.exp(b_col + m_prev - m_t)
            q = q_ref[pl.ds(r0, L), hh * dk:(hh + 1) * dk]
            k = k_ref[pl.ds(r0, L), hh * dk:(hh + 1) * dk]
            v = v_ref[pl.ds(r0, L), hh * dv:(hh + 1) * dv]
            qk = lax.dot_general(q, k, (((1,), (1,)), ((), ())), preferred_element_type=F32) * w
            c_state = c_ref[hh]
            n_state = n_ref[hh]
            num = inter * jnp.dot(q, c_state.astype(BF16), preferred_element_type=F32) \
                + jnp.dot(qk.astype(BF16), v, preferred_element_type=F32)
            den = inter * jnp.sum(q.astype(F32) * n_state, axis=-1, keepdims=True) \
                + jnp.sum(qk, axis=-1, keepdims=True)
            hs = num / jnp.maximum(jnp.abs(den), jnp.exp(-m_t))
            m_new = m_t[L - 1:L, :]
            b_last = b_col[L - 1:L, :]
            decay = jnp.exp(b_last + m_prev - m_new)
            ws = jnp.exp(b_last - b_col + li_col - m_new)
            kw = k.astype(F32) * ws
            c_ref[hh] = decay * c_state + jnp.dot(kw.T.astype(BF16), v, preferred_element_type=F32)
            n_ref[hh] = decay * n_state + jnp.sum(kw, axis=0, keepdims=True)
            m_ref[hh] = m_new
            y = hs * lax.rsqrt(jnp.mean(hs * hs, axis=-1, keepdims=True) + NORM_EPS) \
                * hn_ref[:, hh * dv:(hh + 1) * dv]
            o_ref[pl.ds(r0, L), hh * dv:(hh + 1) * dv] = (
                og_ref[pl.ds(r0, L), hh * dv:(hh + 1) * dv].astype(F32) * y).astype(BF16)
        return carry

    lax.fori_loop(0, seq // L, chunk_body, 0)


def _mlstm_scan(q, k, v, og, gl, glt, head_norm, batch, seq, chunk, heads_per_step):
    t = q.shape[0]
    hps = heads_per_step
    dk, dv = MLSTM_QK_DIM, MLSTM_V_DIM
    ng = 2 * MLSTM_HEADS
    est = 2 * seq * hps * (2 * dk + 3 * dv) * 2 + 2 * seq * LANES * 4 + 2 * ng * seq * 4 \
        + hps * dk * dv * 4 + 16 * chunk * chunk * 4
    return pl.pallas_call(
        functools.partial(_mlstm_scan_kernel, chunk=chunk, heads_per_step=hps),
        grid=(batch, MLSTM_HEADS // hps),
        in_specs=[pl.BlockSpec((seq, hps * dk), lambda b, g: (b, g)),
                  pl.BlockSpec((seq, hps * dk), lambda b, g: (b, g)),
                  pl.BlockSpec((seq, hps * dv), lambda b, g: (b, g)),
                  pl.BlockSpec((seq, hps * dv), lambda b, g: (b, g)),
                  pl.BlockSpec((seq, LANES), lambda b, g: (b, 0)),
                  pl.BlockSpec((ng, seq), lambda b, g: (0, b)),
                  pl.BlockSpec((1, hps * dv), lambda b, g: (0, g))],
        out_specs=pl.BlockSpec((seq, hps * dv), lambda b, g: (b, g)),
        out_shape=jax.ShapeDtypeStruct((t, MLSTM_HEADS * dv), BF16),
        scratch_shapes=[pltpu.VMEM((hps, dk, dv), F32), pltpu.VMEM((hps, 1, dk), F32),
                        pltpu.VMEM((hps, 1, 1), F32)],
        compiler_params=pltpu.CompilerParams(
            dimension_semantics=("arbitrary", "arbitrary"), vmem_limit_bytes=_vmem_limit(est)),
        name="mlstm_scan",
    )(q, k, v, og, gl, glt, head_norm.reshape(1, -1))


def kernel(x, c, positions, mla_w_in, mla_q_norm, mla_kv_norm, mla_w_uq, mla_w_ukv, mla_w_o, mlstm_w_in, mlstm_b_gates, mlstm_head_norm, mlstm_w_out, moe_w_router, moe_b_router, moe_w_gu, moe_b_gu, moe_w_down, moe_b_down, ada_w, ada_b, ln_g, ln_b):
    batch, seq, d = x.shape
    assert d == D_MODEL and ada_w.shape[0] == DEPTH
    cfg = _tiles(batch, seq)
    t = batch * seq
    tm = cfg["tm_tok"]
    mod = _adaln(c, ada_w, ada_b, cfg["tn_ada"])
    lng = ln_g.reshape(2 * DEPTH, d)
    lnb = ln_b.reshape(2 * DEPTH, d)
    xt = x.reshape(t, d)
    for i in range(DEPTH):
        j = i // 2
        k_mix, k_ffn = 2 * i, 2 * i + 1
        if i % 2 == 0:
            qn, qr, kn, v, kr = _mla_proj(xt, mod, k_mix, positions, mla_w_in[j], mla_q_norm[j],
                                          mla_kv_norm[j], mla_w_uq[j], mla_w_ukv[j], seq, tm)
            a = _attention(qn, qr, kn, kr, v, batch, seq, cfg["tq"])
            w_o = mla_w_o[j]
        else:
            q, k, v, og, gl, glt = _mlstm_proj(xt, mod, k_mix, mlstm_w_in[j], mlstm_b_gates[j], seq, tm)
            a = _mlstm_scan(q, k, v, og, gl, glt, mlstm_head_norm[j], batch, seq,
                            cfg["chunk"], cfg["heads_per_step"])
            w_o = mlstm_w_out[j]
        x1, h2, topi, topg = _post(a, xt, mod, k_mix, k_ffn, lng[k_mix], lnb[k_mix], w_o,
                                   moe_w_router[i], moe_b_router[i], seq, cfg["tm_post"])
        xt = _moe(h2, topi, topg, x1, mod, k_ffn, lng[k_ffn], lnb[k_ffn],
                  moe_w_gu, moe_b_gu, moe_w_down, moe_b_down, i, seq, cfg)
    return xt.reshape(batch, seq, d)
```
